```python
import jax, jax.numpy as jnp
from jax import lax
import numpy as np

D_MODEL = 1024
BATCH = 16
SEQ = 4096
DEPTH = 2
DEC_BATCH = 8
DEC_SEQ = 4096
PAST_LEN = 128

N_MEM = 256
MLSTM_HEADS = 4
MLSTM_DK = D_MODEL // 4
MLSTM_DV = D_MODEL // 2
MLSTM_QK = MLSTM_HEADS * MLSTM_DK
MLSTM_V = MLSTM_HEADS * MLSTM_DV
MLSTM_CHUNK = 128
D_RNN = D_MODEL
LRU_BLOCKS = 8
LRU_BW = D_RNN // LRU_BLOCKS
LRU_C = 8.0
CONV_W = 4
XATTN_HEADS = 4
XATTN_HD = D_MODEL // XATTN_HEADS
D_FF = 2816
DN_ALPHA = (2.0 * DEPTH) ** 0.25
DN_BETA = (8.0 * DEPTH) ** -0.25
LN_EPS = 1e-5

OFF_V = 2 * MLSTM_QK
OFF_O = OFF_V + MLSTM_V
OFF_GATE = OFF_O + MLSTM_V
OFF_XR = OFF_GATE + 4 * MLSTM_HEADS
OFF_YR = OFF_XR + D_RNN
OFF_MG = OFF_YR + D_RNN
D_IN = OFF_MG + 2 * D_MODEL
SPLITS = (OFF_V, OFF_O, OFF_GATE, OFF_XR, OFF_YR, OFF_MG)

kernel_name = 'hybrid_mlstm_rglru_encoder'


def layer_norm(x, g, b):
    xf = x.astype(jnp.float32)
    mu = xf.mean(-1, keepdims=True)
    var = jnp.square(xf - mu).mean(-1, keepdims=True)
    y = (xf - mu) * lax.rsqrt(var + LN_EPS)
    return (y * g.astype(jnp.float32) + b.astype(jnp.float32)).astype(x.dtype)


def swiglu_ffn(x, w_in, w_out):
    gate, up = jnp.split(x @ w_in, 2, axis=-1)
    return (jax.nn.silu(gate) * up) @ w_out


def centred_dwconv(x, w, b):
    S = x.shape[1]
    xp = jnp.pad(x, ((0, 0), (CONV_W // 2 - 1, CONV_W // 2), (0, 0)))
    out = b
    for j in range(CONV_W):
        out = out + xp[:, j:j + S] * w[j]
    return out


def mlstm_scan(q, k, v, ig, lf):
    B, S, H, DK = q.shape
    DV = v.shape[-1]
    NC = S // MLSTM_CHUNK

    def to_chunks(a):
        a = a.reshape((B, NC, MLSTM_CHUNK) + a.shape[2:])
        return jnp.moveaxis(jnp.swapaxes(a, 2, 3), 1, 0)

    causal = jnp.tril(jnp.ones((MLSTM_CHUNK, MLSTM_CHUNK), dtype=bool))

    def step(carry, xs):
        C, n, m = carry
        qc, kc, vc, ic, fc = xs
        g = jnp.cumsum(fc, axis=-1)
        G = g[..., -1]
        log_d = jnp.where(causal, g[..., :, None] - g[..., None, :] + ic[..., None, :], -jnp.inf)
        log_inter = g + m[..., None]
        m_t = jnp.maximum(log_inter, log_d.max(-1))
        s = jnp.einsum('bhtd,bhsd->bhts', qc, kc) * jnp.exp(log_d - m_t[..., None])
        w_inter = jnp.exp(log_inter - m_t)
        num = jnp.einsum('bhts,bhsv->bhtv', s, vc) + w_inter[..., None] * jnp.einsum('bhtd,bhdv->bhtv', qc, C)
        den = s.sum(-1) + w_inter * jnp.einsum('bhtd,bhd->bht', qc, n)
        h = num / jnp.maximum(jnp.abs(den), jnp.exp(-m_t))[..., None]
        log_w = G[..., None] - g + ic
        m_new = jnp.maximum(G + m, log_w.max(-1))
        wk = jnp.exp(log_w - m_new[..., None])[..., None] * kc
        decay = jnp.exp(G + m - m_new)
        C = decay[..., None, None] * C + jnp.einsum('bhsd,bhsv->bhdv', wk, vc)
        n = decay[..., None] * n + wk.sum(-2)
        return (C, n, m_new), h

    init = (jnp.zeros((B, H, DK, DV), jnp.float32), jnp.zeros((B, H, DK), jnp.float32),
            jnp.zeros((B, H), jnp.float32))
    _, h = lax.scan(step, init, (to_chunks(q), to_chunks(k), to_chunks(v), to_chunks(ig), to_chunks(lf)))
    h = jnp.swapaxes(jnp.moveaxis(h, 0, 1), 2, 3)
    return h.reshape(B, S, H, DV)


def rglru(x, w_a, b_a, w_x, b_x, lam, reverse):
    B, S, _ = x.shape
    xb = x.reshape(B, S, LRU_BLOCKS, LRU_BW)
    r = jax.nn.sigmoid(jnp.einsum('bsni,nij->bsnj', xb, w_a).reshape(B, S, D_RNN) + b_a)
    i = jax.nn.sigmoid(jnp.einsum('bsni,nij->bsnj', xb, w_x).reshape(B, S, D_RNN) + b_x)
    log_a = -LRU_C * r * jax.nn.softplus(-lam)
    a = jnp.exp(log_a)
    u = jnp.sqrt(-jnp.expm1(2.0 * log_a)) * (i * x)

    def combine(left, right):
        a1, b1 = left
        a2, b2 = right
        return a1 * a2, a2 * b1 + b2

    _, h = lax.associative_scan(combine, (a, u), reverse=reverse, axis=1)
    return h


def parallel_mixer(x, w_in, b_in, w_conv_qk, b_conv_qk, w_conv_r, b_conv_r, mh_gain,
                   lru_wa, lru_ba, lru_wx, lru_bx, lru_lam, w_pm, w_pr, w_out):
    B, S, _ = x.shape
    f32 = jnp.float32
    u = x @ w_in + b_in
    qk, v, o, gates, xr, yr, mg = jnp.split(u, SPLITS, axis=-1)
    qk = jax.nn.silu(centred_dwconv(qk, w_conv_qk, b_conv_qk)).astype(f32)
    q, k = jnp.split(qk, 2, axis=-1)
    q = q.reshape(B, S, MLSTM_HEADS, MLSTM_DK) * (MLSTM_DK ** -0.5)
    k = k.reshape(B, S, MLSTM_HEADS, MLSTM_DK)
    v = v.astype(f32).reshape(B, S, MLSTM_HEADS, MLSTM_DV)
    gates = gates.astype(f32).reshape(B, S, 4, MLSTM_HEADS)
    h_f = mlstm_scan(q, k, v, gates[:, :, 0], jax.nn.log_sigmoid(gates[:, :, 1]))
    rev = lambda a: jnp.flip(a, axis=1)
    h_b = rev(mlstm_scan(rev(q), rev(k), rev(v), rev(gates[:, :, 2]), rev(jax.nn.log_sigmoid(gates[:, :, 3]))))
    h = h_f + h_b
    mu = h.mean(-1, keepdims=True)
    var = jnp.square(h - mu).mean(-1, keepdims=True)
    h = ((h - mu) * lax.rsqrt(var + LN_EPS)).reshape(B, S, MLSTM_V) * mh_gain.astype(f32)
    h_m = (jax.nn.sigmoid(o.astype(f32)) * h).astype(x.dtype)
    xr = centred_dwconv(xr, w_conv_r, b_conv_r).astype(f32)
    h_r = (rglru(xr, lru_wa[0], lru_ba[0], lru_wx[0], lru_bx[0], lru_lam[0], False)
           + rglru(xr, lru_wa[1], lru_ba[1], lru_wx[1], lru_bx[1], lru_lam[1], True))
    h_r = (h_r * jax.nn.gelu(yr.astype(f32))).astype(x.dtype)
    g_m, g_r = jnp.split(jax.nn.sigmoid(mg), 2, axis=-1)
    merged = g_m * (h_m @ w_pm) + g_r * (h_r @ w_pr)
    return merged @ w_out


def memory_cross_attention(x, mem, w_q, w_kv, w_o):
    B, S, _ = x.shape
    M = mem.shape[1]
    q = (x @ w_q).reshape(B, S, XATTN_HEADS, XATTN_HD)
    k, v = jnp.split(mem @ w_kv, 2, axis=-1)
    k = k.reshape(B, M, XATTN_HEADS, XATTN_HD)
    v = v.reshape(B, M, XATTN_HEADS, XATTN_HD)
    s = jnp.einsum('bshd,bmhd->bhsm', q, k).astype(jnp.float32) * (XATTN_HD ** -0.5)
    p = jax.nn.softmax(s, axis=-1).astype(x.dtype)
    out = jnp.einsum('bhsm,bmhd->bshd', p, v).reshape(B, S, D_MODEL)
    return out @ w_o


def encoder_trunk(x, mem, p):
    for l in range(DEPTH):
        x = layer_norm(DN_ALPHA * x + 0.5 * swiglu_ffn(x, p['ff1_in'][l], p['ff1_out'][l]),
                       p['ln_g'][l, 0], p['ln_b'][l, 0])
        mix = parallel_mixer(x, p['w_in'][l], p['b_in'][l], p['w_conv_qk'][l], p['b_conv_qk'][l],
                             p['w_conv_r'][l], p['b_conv_r'][l], p['mh_gain'][l],
                             p['lru_wa'][l], p['lru_ba'][l], p['lru_wx'][l], p['lru_bx'][l], p['lru_lam'][l],
                             p['w_pm'][l], p['w_pr'][l], p['w_out'][l])
        x = layer_norm(DN_ALPHA * x + mix, p['ln_g'][l, 1], p['ln_b'][l, 1])
        m = layer_norm(mem, p['mem_ln_g'][l], p['mem_ln_b'][l])
        xa = memory_cross_attention(x, m, p['xa_wq'][l], p['xa_wkv'][l], p['xa_wo'][l])
        x = layer_norm(DN_ALPHA * x + xa, p['ln_g'][l, 2], p['ln_b'][l, 2])
        x = layer_norm(DN_ALPHA * x + 0.5 * swiglu_ffn(x, p['ff2_in'][l], p['ff2_out'][l]),
                       p['ln_g'][l, 3], p['ln_b'][l, 3])
    return x


def setup_inputs(seed: int = 0) -> dict:
    key = jax.random.key(seed)
    ks = iter(jax.random.split(key, 40))
    f32 = jnp.float32
    D = D_MODEL

    def nrm(shape, scale):
        return jax.random.normal(next(ks), shape, f32) * scale

    x_prompt = nrm((BATCH, SEQ, D), 1.0)
    x_sample = nrm((DEC_BATCH, DEC_SEQ, D), 1.0)
    mem_prompt = nrm((BATCH, N_MEM, D), 1.0)
    mem_sample = nrm((DEC_BATCH, N_MEM, D), 1.0)

    w_in = nrm((DEPTH, D, D_IN), D ** -0.5)
    fgate_bias = jnp.linspace(3.0, 6.0, MLSTM_HEADS, dtype=f32)
    gate_bias = jnp.concatenate([nrm((MLSTM_HEADS,), 0.1), fgate_bias,
                                 nrm((MLSTM_HEADS,), 0.1), fgate_bias])
    b_in = nrm((DEPTH, D_IN), 0.02).at[:, OFF_GATE:OFF_XR].add(gate_bias)
    w_conv_qk = nrm((DEPTH, CONV_W, 2 * MLSTM_QK), CONV_W ** -0.5)
    b_conv_qk = nrm((DEPTH, 2 * MLSTM_QK), 0.02)
    w_conv_r = nrm((DEPTH, CONV_W, D_RNN), CONV_W ** -0.5)
    b_conv_r = nrm((DEPTH, D_RNN), 0.02)
    mh_gain = 1.0 + nrm((DEPTH, MLSTM_V), 0.02)
    lru_wa = nrm((DEPTH, 2, LRU_BLOCKS, LRU_BW, LRU_BW), LRU_BW ** -0.5)
    lru_ba = nrm((DEPTH, 2, D_RNN), 0.02)
    lru_wx = nrm((DEPTH, 2, LRU_BLOCKS, LRU_BW, LRU_BW), LRU_BW ** -0.5)
    lru_bx = nrm((DEPTH, 2, D_RNN), 0.02)
    a_c = jax.random.uniform(next(ks), (DEPTH, 2, D_RNN), f32, minval=0.9, maxval=0.999)
    sig = a_c ** (1.0 / LRU_C)
    lru_lam = jnp.log(sig) - jnp.log1p(-sig)
    w_pm = nrm((DEPTH, MLSTM_V, D), MLSTM_V ** -0.5)
    w_pr = nrm((DEPTH, D_RNN, D), D_RNN ** -0.5)
    w_out = nrm((DEPTH, D, D), D ** -0.5 * DN_BETA)
    xa_wq = nrm((DEPTH, D, D), D ** -0.5)
    xa_wkv = nrm((DEPTH, D, 2 * D), D ** -0.5)
    xa_wo = nrm((DEPTH, D, D), D ** -0.5 * DN_BETA)
    mem_ln_g = 1.0 + nrm((DEPTH, D), 0.02)
    mem_ln_b = nrm((DEPTH, D), 0.02)
    ff1_in = nrm((DEPTH, D, 2 * D_FF), D ** -0.5)
    ff1_out = nrm((DEPTH, D_FF, D), D_FF ** -0.5 * DN_BETA)
    ff2_in = nrm((DEPTH, D, 2 * D_FF), D ** -0.5)
    ff2_out = nrm((DEPTH, D_FF, D), D_FF ** -0.5 * DN_BETA)
    ln_g = 1.0 + nrm((DEPTH, 4, D), 0.02)
    ln_b = nrm((DEPTH, 4, D), 0.02)
    return {'x_prompt': x_prompt, 'x_sample': x_sample, 'mem_prompt': mem_prompt, 'mem_sample': mem_sample,
            'w_in': w_in, 'b_in': b_in, 'w_conv_qk': w_conv_qk, 'b_conv_qk': b_conv_qk,
            'w_conv_r': w_conv_r, 'b_conv_r': b_conv_r, 'mh_gain': mh_gain,
            'lru_wa': lru_wa, 'lru_ba': lru_ba, 'lru_wx': lru_wx, 'lru_bx': lru_bx, 'lru_lam': lru_lam,
            'w_pm': w_pm, 'w_pr': w_pr, 'w_out': w_out,
            'xa_wq': xa_wq, 'xa_wkv': xa_wkv, 'xa_wo': xa_wo, 'mem_ln_g': mem_ln_g, 'mem_ln_b': mem_ln_b,
            'ff1_in': ff1_in, 'ff1_out': ff1_out, 'ff2_in': ff2_in, 'ff2_out': ff2_out,
            'ln_g': ln_g, 'ln_b': ln_b}


def reference(x_prompt, x_sample, mem_prompt, mem_sample, w_in, b_in, w_conv_qk, b_conv_qk,
              w_conv_r, b_conv_r, mh_gain, lru_wa, lru_ba, lru_wx, lru_bx, lru_lam,
              w_pm, w_pr, w_out, xa_wq, xa_wkv, xa_wo, mem_ln_g, mem_ln_b,
              ff1_in, ff1_out, ff2_in, ff2_out, ln_g, ln_b):
    p = dict(w_in=w_in, b_in=b_in, w_conv_qk=w_conv_qk, b_conv_qk=b_conv_qk,
             w_conv_r=w_conv_r, b_conv_r=b_conv_r, mh_gain=mh_gain,
             lru_wa=lru_wa, lru_ba=lru_ba, lru_wx=lru_wx, lru_bx=lru_bx, lru_lam=lru_lam,
             w_pm=w_pm, w_pr=w_pr, w_out=w_out, xa_wq=xa_wq, xa_wkv=xa_wkv, xa_wo=xa_wo,
             mem_ln_g=mem_ln_g, mem_ln_b=mem_ln_b, ff1_in=ff1_in, ff1_out=ff1_out,
             ff2_in=ff2_in, ff2_out=ff2_out, ln_g=ln_g, ln_b=ln_b)
    y_prompt = encoder_trunk(x_prompt, mem_prompt, p)
    y_sample = encoder_trunk(x_sample, mem_sample, p)
    return (y_prompt, y_sample)
```

```python
import functools

import jax
import jax.numpy as jnp
from jax import lax
from jax.experimental import pallas as pl
from jax.experimental.pallas import tpu as pltpu

F32 = jnp.float32
BF16 = jnp.bfloat16

D_MODEL = 1024
DEPTH = 2
N_MEM = 256
MLSTM_HEADS = 4
MLSTM_DK = D_MODEL // 4
MLSTM_DV = D_MODEL // 2
MLSTM_QK = MLSTM_HEADS * MLSTM_DK
MLSTM_V = MLSTM_HEADS * MLSTM_DV
D_RNN = D_MODEL
LRU_BLOCKS = 8
LRU_BW = D_RNN // LRU_BLOCKS
LRU_C = 8.0
CONV_W = 4
XATTN_HEADS = 4
XATTN_HD = D_MODEL // XATTN_HEADS
D_FF = 2816
DN_ALPHA = (2.0 * DEPTH) ** 0.25
LN_EPS = 1e-5

OFF_V = 2 * MLSTM_QK
OFF_O = OFF_V + MLSTM_V
OFF_GATE = OFF_O + MLSTM_V
OFF_XR = OFF_GATE + 4 * MLSTM_HEADS
OFF_YR = OFF_XR + D_RNN
OFF_MG = OFF_YR + D_RNN
D_IN = OFF_MG + 2 * D_MODEL
N_GATES = 4 * MLSTM_HEADS

VMEM_LIMIT_BYTES = 56 * 1024 * 1024
SUBLANES = 8
BF16_ROWS = 16

TOKEN_TILE = 512
MERGE_TILE = 256
FF_CHUNK = D_FF // 2
MLSTM_CHUNK = 128
LRU_TCHUNK = 64
LRU_GROUP = SUBLANES
CONV_HALO = BF16_ROWS


def _dot(a, b):
    return jnp.dot(a, b, preferred_element_type=F32)


def _dot_nt(a, b):
    return lax.dot_general(a, b, (((1,), (1,)), ((), ())), preferred_element_type=F32)


def _dot_tn(a, b):
    return lax.dot_general(a, b, (((0,), (0,)), ((), ())), preferred_element_type=F32)


def _dot_f32(a, b):
    return jnp.dot(a, b, preferred_element_type=F32, precision=lax.Precision.HIGHEST)


def _layer_norm(y, g, b):
    mu = jnp.mean(y, axis=-1, keepdims=True)
    d = y - mu
    var = jnp.mean(d * d, axis=-1, keepdims=True)
    return d * lax.rsqrt(var + LN_EPS) * g + b


def _log_sigmoid(x):
    return jnp.minimum(x, 0.0) - jnp.log1p(jnp.exp(-jnp.abs(x)))


def _softplus(x):
    return jnp.maximum(x, 0.0) + jnp.log1p(jnp.exp(-jnp.abs(x)))


def _resident(shape):
    zeros = (0,) * len(shape)
    return pl.BlockSpec(shape, lambda *_: zeros, pipeline_mode=pl.Buffered(1))


def _params(*sem):
    return pltpu.CompilerParams(dimension_semantics=sem, vmem_limit_bytes=VMEM_LIMIT_BYTES)


def _ffn_ln_kernel(x_ref, win_ref, wout_ref, g_ref, b_ref, y_ref, yb_ref):
    x = x_ref[...]
    xb = x.astype(BF16)
    acc = jnp.zeros(x.shape, F32)
    for lo in range(0, D_FF, FF_CHUNK):
        gate = _dot(xb, win_ref[:, lo:lo + FF_CHUNK])
        up = _dot(xb, win_ref[:, D_FF + lo:D_FF + lo + FF_CHUNK])
        h = (gate * jax.nn.sigmoid(gate) * up).astype(BF16)
        acc = acc + _dot(h, wout_ref[lo:lo + FF_CHUNK, :])
    y = _layer_norm(DN_ALPHA * x + 0.5 * acc, g_ref[...], b_ref[...])
    y_ref[...] = y
    yb_ref[...] = y.astype(BF16)


def _ffn_ln(x, w_in, w_out, g, b):
    T = x.shape[0]
    tm = TOKEN_TILE
    row = lambda i: (i, 0)
    return pl.pallas_call(
        _ffn_ln_kernel,
        out_shape=(jax.ShapeDtypeStruct((T, D_MODEL), F32), jax.ShapeDtypeStruct((T, D_MODEL), BF16)),
        grid=(T // tm,),
        in_specs=[pl.BlockSpec((tm, D_MODEL), row), _resident((D_MODEL, 2 * D_FF)),
                  _resident((D_FF, D_MODEL)), _resident((1, D_MODEL)), _resident((1, D_MODEL))],
        out_specs=(pl.BlockSpec((tm, D_MODEL), row), pl.BlockSpec((tm, D_MODEL), row)),
        compiler_params=_params("arbitrary"),
        name="ffn_ln",
    )(x, w_in, w_out, g, b)


def _proj_kernel(x_ref, w_ref, b_ref, o_ref):
    o_ref[...] = (_dot(x_ref[...], w_ref[...]) + b_ref[...]).astype(o_ref.dtype)


def _proj(xb, w, b, out_dtype, tn, seq_len=None):
    T, N = xb.shape[0], w.shape[1]
    tm = TOKEN_TILE
    if seq_len is None:
        out_shape = (T, N)
        out_map = lambda n, i: (i, n)
    else:
        tiles_per_seq = seq_len // tm
        out_shape = (seq_len, (T // seq_len) * N)
        out_map = lambda n, i: (i % tiles_per_seq, (i // tiles_per_seq) * (N // tn) + n)
    return pl.pallas_call(
        _proj_kernel,
        out_shape=jax.ShapeDtypeStruct(out_shape, out_dtype),
        grid=(N // tn, T // tm),
        in_specs=[pl.BlockSpec((tm, D_MODEL), lambda n, i: (i, 0)),
                  pl.BlockSpec((D_MODEL, tn), lambda n, i: (0, n)),
                  pl.BlockSpec((1, tn), lambda n, i: (0, n))],
        out_specs=pl.BlockSpec((tm, tn), out_map),
        compiler_params=_params("arbitrary", "arbitrary"),
        name="proj",
    )(xb, w, b)


def _qk_conv_kernel(x_ref, xp_ref, xn_ref, w_ref, b_ref, cw_ref, cb_ref, sc_ref, o_ref,
                    xe_ref, ue_ref, *, tiles_per_seq):
    tm = x_ref.shape[0]
    H = CONV_HALO
    xe_ref[0:H, :] = xp_ref[...]
    xe_ref[H:H + tm, :] = x_ref[...]
    xe_ref[H + tm:, :] = xn_ref[...]
    ue_ref[...] = _dot(xe_ref[...], w_ref[...]) + b_ref[...]
    pos = pl.program_id(1) % tiles_per_seq

    @pl.when(pos == 0)
    def _():
        ue_ref[0:H, :] = jnp.zeros((H, ue_ref.shape[1]), F32)

    @pl.when(pos == tiles_per_seq - 1)
    def _():
        ue_ref[H + tm:, :] = jnp.zeros((H, ue_ref.shape[1]), F32)

    out = cb_ref[...]
    for j in range(CONV_W):
        out = out + ue_ref[pl.ds(H - 1 + j, tm), :] * cw_ref[j:j + 1, :]
    o_ref[...] = (out * jax.nn.sigmoid(out) * sc_ref[...]).astype(o_ref.dtype)


def _qk_conv(xb, w, b, cw, cb, scale, seq_len):
    T, N = xb.shape[0], w.shape[1]
    tm, tn, H = TOKEN_TILE, 1024, CONV_HALO
    tiles_per_seq = seq_len // tm
    hpt = tm // H
    last_halo = T // H - 1
    col = lambda n, i: (0, n)
    return pl.pallas_call(
        functools.partial(_qk_conv_kernel, tiles_per_seq=tiles_per_seq),
        out_shape=jax.ShapeDtypeStruct((T, N), BF16),
        grid=(N // tn, T // tm),
        in_specs=[pl.BlockSpec((tm, D_MODEL), lambda n, i: (i, 0)),
                  pl.BlockSpec((H, D_MODEL), lambda n, i: (jnp.maximum(i * hpt - 1, 0), 0)),
                  pl.BlockSpec((H, D_MODEL), lambda n, i: (jnp.minimum((i + 1) * hpt, last_halo), 0)),
                  pl.BlockSpec((D_MODEL, tn), col), pl.BlockSpec((1, tn), col),
                  pl.BlockSpec((CONV_W, tn), col), pl.BlockSpec((1, tn), col),
                  pl.BlockSpec((1, tn), col)],
        out_specs=pl.BlockSpec((tm, tn), lambda n, i: (i, n)),
        scratch_shapes=[pltpu.VMEM((tm + 2 * H, D_MODEL), BF16), pltpu.VMEM((tm + 2 * H, tn), F32)],
        compiler_params=_params("arbitrary", "arbitrary"),
        name="qk_conv",
    )(xb, xb, xb, w, b, cw, cb, scale)


def _gates_kernel(x_ref, w_ref, wt_ref, b_ref, bt_ref, g_ref, gt_ref):
    x = x_ref[...]
    g_ref[...] = _dot(x, w_ref[...]) + b_ref[...]
    gt_ref[...] = _dot_nt(wt_ref[...], x) + bt_ref[...]


def _gates(xb, w, wt, b, bt):
    T = xb.shape[0]
    tm = TOKEN_TILE
    return pl.pallas_call(
        _gates_kernel,
        out_shape=(jax.ShapeDtypeStruct((T, N_GATES), F32), jax.ShapeDtypeStruct((N_GATES, T), F32)),
        grid=(T // tm,),
        in_specs=[pl.BlockSpec((tm, D_MODEL), lambda i: (i, 0)),
                  _resident((D_MODEL, N_GATES)), _resident((N_GATES, D_MODEL)),
                  _resident((1, N_GATES)), _resident((N_GATES, 1))],
        out_specs=(pl.BlockSpec((tm, N_GATES), lambda i: (i, 0)),
                   pl.BlockSpec((N_GATES, tm), lambda i: (0, i))),
        compiler_params=_params("arbitrary"),
        name="mlstm_gates",
    )(xb, w, wt, b, bt)


def _mlstm_kernel(qf_ref, kf_ref, vf_ref, gf_ref, gtf_ref, qb_ref, kb_ref, vb_ref, gb_ref, gtb_ref,
                  hf_ref, hb_ref, c_ref, n_ref, m_ref):
    L = qf_ref.shape[0]
    H, DK, DV = MLSTM_HEADS, MLSTM_DK, MLSTM_DV

    @pl.when(pl.program_id(1) == 0)
    def _():
        c_ref[...] = jnp.zeros(c_ref.shape, F32)
        n_ref[...] = jnp.zeros(n_ref.shape, F32)
        m_ref[...] = jnp.zeros(m_ref.shape, F32)

    row = lax.broadcasted_iota(jnp.int32, (L, L), 0)
    col = lax.broadcasted_iota(jnp.int32, (L, L), 1)
    lower = col <= row
    upper = col >= row
    tri_l = lower.astype(F32)
    tri_u = upper.astype(F32)

    dirs = ((qf_ref, kf_ref, vf_ref, gf_ref, gtf_ref, hf_ref, lower, tri_l, tri_u, L - 1),
            (qb_ref, kb_ref, vb_ref, gb_ref, gtb_ref, hb_ref, upper, tri_u, tri_l, 0))
    for d, (q_ref, k_ref, v_ref, g_ref, gt_ref, h_ref, mask, cum_cols, cum_rows, last) in enumerate(dirs):
        gc = g_ref[...]
        gt = gt_ref[...]
        g_cols = _dot_f32(cum_cols, _log_sigmoid(gc))
        g_rows = _dot_f32(_log_sigmoid(gt), cum_rows)
        for h in range(H):
            ii = d * 2 * H + h
            fi = ii + H
            u = d * H + h
            gcol = g_cols[:, fi:fi + 1]
            grow = g_rows[fi:fi + 1, :]
            igcol = gc[:, ii:ii + 1]
            igrow = gt[ii:ii + 1, :]
            m = m_ref[u:u + 1, 0:1]
            qh = q_ref[:, h * DK:(h + 1) * DK]
            kh = k_ref[:, h * DK:(h + 1) * DK]
            vh = v_ref[:, h * DV:(h + 1) * DV]

            brow = igrow - grow
            log_d = jnp.where(mask, gcol + brow, -jnp.inf)
            log_inter = gcol + m
            m_t = jnp.maximum(log_inter, jnp.max(log_d, axis=-1, keepdims=True))
            s = _dot_nt(qh, kh) * jnp.exp(log_d - m_t)
            w_inter = jnp.exp(log_inter - m_t)
            c_old = c_ref[u]
            n_old = n_ref[u:u + 1, :]
            num = _dot(s.astype(BF16), vh) + w_inter * _dot(qh, c_old.astype(BF16))
            qn = jnp.sum(qh.astype(F32) * n_old, axis=-1, keepdims=True)
            den = jnp.sum(s, axis=-1, keepdims=True) + w_inter * qn
            h_ref[:, h * DV:(h + 1) * DV] = num / jnp.maximum(jnp.abs(den), jnp.exp(-m_t))

            g_tot = grow[:, last:last + 1]
            m_new = jnp.maximum(g_tot + m, jnp.max(g_tot + brow, axis=-1, keepdims=True))
            wk = jnp.exp(g_tot - gcol + igcol - m_new) * kh.astype(F32)
            decay = jnp.exp(g_tot + m - m_new)
            c_ref[u] = decay * c_old + _dot_tn(wk.astype(BF16), vh)
            n_ref[u:u + 1, :] = decay * n_old + jnp.sum(wk, axis=0, keepdims=True)
            m_ref[u:u + 1, :] = jnp.broadcast_to(m_new, (1, m_ref.shape[1]))


def _mlstm(qk, vom, g, gt, batch, seq_len):
    T = qk.shape[0]
    L = MLSTM_CHUNK
    nc = seq_len // L
    units = 2 * MLSTM_HEADS
    fwd = lambda b, c: b * nc + c
    bwd = lambda b, c: b * nc + nc - 1 - c

    def specs(pos):
        return [pl.BlockSpec((L, MLSTM_QK), lambda b, c: (pos(b, c), 0)),
                pl.BlockSpec((L, MLSTM_QK), lambda b, c: (pos(b, c), 1)),
                pl.BlockSpec((L, MLSTM_V), lambda b, c: (pos(b, c), 0)),
                pl.BlockSpec((L, N_GATES), lambda b, c: (pos(b, c), 0)),
                pl.BlockSpec((N_GATES, L), lambda b, c: (0, pos(b, c)))]

    out = jax.ShapeDtypeStruct((T, MLSTM_V), F32)
    return pl.pallas_call(
        _mlstm_kernel,
        out_shape=(out, out),
        grid=(batch, nc),
        in_specs=specs(fwd) + specs(bwd),
        out_specs=(pl.BlockSpec((L, MLSTM_V), lambda b, c: (fwd(b, c), 0)),
                   pl.BlockSpec((L, MLSTM_V), lambda b, c: (bwd(b, c), 0))),
        scratch_shapes=[pltpu.VMEM((units, MLSTM_DK, MLSTM_DV), F32),
                        pltpu.VMEM((units, MLSTM_DK), F32),
                        pltpu.VMEM((units, 128), F32)],
        compiler_params=_params("arbitrary", "arbitrary"),
        name="mlstm",
    )(qk, qk, vom, g, gt, qk, qk, vom, g, gt)


def _rglru_gates(xc, d, wa_ref, ba_ref, wx_ref, bx_ref, sp_ref, a_ref, u_ref):
    tc = xc.shape[0]
    x2 = xc.reshape(tc * LRU_GROUP, D_RNN)
    xb = x2.astype(BF16)
    for n in range(LRU_BLOCKS):
        sl = slice(n * LRU_BW, (n + 1) * LRU_BW)
        r = jax.nn.sigmoid(_dot(xb[:, sl], wa_ref[d, n]) + ba_ref[d:d + 1, sl])
        i = jax.nn.sigmoid(_dot(xb[:, sl], wx_ref[d, n]) + bx_ref[d:d + 1, sl])
        a = jnp.exp(-LRU_C * r * sp_ref[d:d + 1, sl])
        u = jnp.sqrt(1.0 - a * a) * (i * x2[:, sl])
        a_ref[:, :, sl] = a.reshape(tc, LRU_GROUP, LRU_BW)
        u_ref[:, :, sl] = u.reshape(tc, LRU_GROUP, LRU_BW)


def _rglru_kernel(x_ref, xp_ref, xn_ref, y_ref, cw_ref, cb_ref, wa_ref, ba_ref, wx_ref, bx_ref, lam_ref,
                  o_ref, a_ref, u_ref, hf_ref, carry_ref, state_ref, sp_ref, *, nc):
    tc = x_ref.shape[0]
    p = pl.program_id(1)
    c = pl.program_id(2)
    cc = jnp.where(p == 0, nc - 1 - c, c)

    @pl.when(c == 0)
    def _():
        state_ref[...] = jnp.zeros(state_ref.shape, F32)
        sp_ref[...] = _softplus(-lam_ref[...])

    prev = jnp.where(cc > 0, xp_ref[...], 0.0)
    nxt = jnp.where(cc < nc - 1, xn_ref[...], 0.0)
    xe = jnp.concatenate([prev, x_ref[...], nxt], axis=0)
    xc = cb_ref[...]
    for j in range(CONV_W):
        xc = xc + xe[j:j + tc] * cw_ref[j:j + 1, :]

    @pl.when(p == 0)
    def _():
        _rglru_gates(xc, 1, wa_ref, ba_ref, wx_ref, bx_ref, sp_ref, a_ref, u_ref)
        carry_ref[cc] = state_ref[...]

        def step(k, h):
            t = tc - 1 - k
            return a_ref[t] * h + u_ref[t]

        state_ref[...] = lax.fori_loop(0, tc, step, state_ref[...])

    @pl.when(p == 1)
    def _():
        _rglru_gates(xc, 0, wa_ref, ba_ref, wx_ref, bx_ref, sp_ref, a_ref, u_ref)

        def fstep(t, h):
            h = a_ref[t] * h + u_ref[t]
            hf_ref[t] = h
            return h

        state_ref[...] = lax.fori_loop(0, tc, fstep, state_ref[...])
        _rglru_gates(xc, 1, wa_ref, ba_ref, wx_ref, bx_ref, sp_ref, a_ref, u_ref)

        def bstep(k, h):
            t = tc - 1 - k
            h = a_ref[t] * h + u_ref[t]
            o_ref[t] = (hf_ref[t] + h) * jax.nn.gelu(y_ref[t])
            return h

        lax.fori_loop(0, tc, bstep, carry_ref[cc])


def _rglru(xy, cw, cb, wa, ba, wx, bx, lam, batch, seq_len):
    tc, G, C = LRU_TCHUNK, LRU_GROUP, D_RNN
    nc = seq_len // tc
    chunk = lambda g, p, c: jnp.where(p == 0, nc - 1 - c, c)
    prev_row = lambda g, p, c: jnp.maximum(chunk(g, p, c) * tc - 1, 0)
    next_pair = lambda g, p, c: jnp.minimum((chunk(g, p, c) + 1) * (tc // 2), seq_len // 2 - 1)
    return pl.pallas_call(
        functools.partial(_rglru_kernel, nc=nc),
        out_shape=jax.ShapeDtypeStruct((seq_len, batch, C), F32),
        grid=(batch // G, 2, nc),
        in_specs=[pl.BlockSpec((tc, G, C), lambda g, p, c: (chunk(g, p, c), g, 0)),
                  pl.BlockSpec((1, G, C), lambda g, p, c: (prev_row(g, p, c), g, 0)),
                  pl.BlockSpec((2, G, C), lambda g, p, c: (next_pair(g, p, c), g, 0)),
                  pl.BlockSpec((tc, G, C), lambda g, p, c: (chunk(g, p, c), g, 1)),
                  _resident((CONV_W, C)), _resident((1, C)),
                  _resident((2, LRU_BLOCKS, LRU_BW, LRU_BW)), _resident((2, C)),
                  _resident((2, LRU_BLOCKS, LRU_BW, LRU_BW)), _resident((2, C)),
                  _resident((2, C))],
        out_specs=pl.BlockSpec((tc, G, C), lambda g, p, c: (c * p, g, 0)),
        scratch_shapes=[pltpu.VMEM((tc, G, C), F32), pltpu.VMEM((tc, G, C), F32),
                        pltpu.VMEM((tc, G, C), F32), pltpu.VMEM((nc, G, C), F32),
                        pltpu.VMEM((G, C), F32), pltpu.VMEM((2, C), F32)],
        compiler_params=_params("arbitrary", "arbitrary", "arbitrary"),
        name="rglru",
    )(xy, xy, xy, xy, cw, cb, wa, ba, wx, bx, lam)


def _kv_kernel(m_ref, g_ref, b_ref, w_ref, o_ref):
    m = _layer_norm(m_ref[...], g_ref[...], b_ref[...])
    o_ref[...] = _dot(m.astype(BF16), w_ref[...]).astype(o_ref.dtype)


def _mem_kv(mem, g, b, w_kv):
    R = mem.shape[0]
    return pl.pallas_call(
        _kv_kernel,
        out_shape=jax.ShapeDtypeStruct((R, 2 * D_MODEL), BF16),
        grid=(R // N_MEM,),
        in_specs=[pl.BlockSpec((N_MEM, D_MODEL), lambda i: (i, 0)), _resident((1, D_MODEL)),
                  _resident((1, D_MODEL)), _resident((D_MODEL, 2 * D_MODEL))],
        out_specs=pl.BlockSpec((N_MEM, 2 * D_MODEL), lambda i: (i, 0)),
        compiler_params=_params("arbitrary"),
        name="mem_kv",
    )(mem, g, b, w_kv)


def _merge_xattn_kernel(x_ref, hf_ref, hb_ref, o_ref, mg_ref, hr_ref, kv_ref, gain_ref,
                        wpm_ref, wpr_ref, wout_ref, wq_ref, wo_ref, lng_ref, lnb_ref, y_ref):
    x = x_ref[...]
    pm = jnp.zeros(x.shape, F32)
    for h in range(MLSTM_HEADS):
        sl = slice(h * MLSTM_DV, (h + 1) * MLSTM_DV)
        hh = hf_ref[:, sl] + hb_ref[:, sl]
        mu = jnp.mean(hh, axis=-1, keepdims=True)
        d = hh - mu
        var = jnp.mean(d * d, axis=-1, keepdims=True)
        hn = d * lax.rsqrt(var + LN_EPS) * gain_ref[:, sl]
        hm = (jax.nn.sigmoid(o_ref[:, sl].astype(F32)) * hn).astype(BF16)
        pm = pm + _dot(hm, wpm_ref[sl, :])
    pr = _dot(hr_ref[...].astype(BF16), wpr_ref[...])
    g_m = jax.nn.sigmoid(mg_ref[:, :D_MODEL].astype(F32))
    g_r = jax.nn.sigmoid(mg_ref[:, D_MODEL:].astype(F32))
    mix = _dot((g_m * pm + g_r * pr).astype(BF16), wout_ref[...])
    x = _layer_norm(DN_ALPHA * x + mix, lng_ref[0:1, :], lnb_ref[0:1, :])

    q = _dot(x.astype(BF16), wq_ref[...]).astype(BF16)
    xa = jnp.zeros(x.shape, F32)
    for h in range(XATTN_HEADS):
        sl = slice(h * XATTN_HD, (h + 1) * XATTN_HD)
        s = _dot_nt(q[:, sl], kv_ref[:, sl]) * (XATTN_HD ** -0.5)
        e = jnp.exp(s - jnp.max(s, axis=-1, keepdims=True))
        p = e / jnp.sum(e, axis=-1, keepdims=True)
        oh = _dot(p.astype(BF16), kv_ref[:, D_MODEL + h * XATTN_HD:D_MODEL + (h + 1) * XATTN_HD])
        xa = xa + _dot(oh.astype(BF16), wo_ref[sl, :])
    y_ref[...] = _layer_norm(DN_ALPHA * x + xa, lng_ref[1:2, :], lnb_ref[1:2, :])


def _merge_xattn(x, hf, hb, vom, hr, kv, gain, wpm, wpr, wout, wq, wo, lng, lnb, seq_len):
    T = x.shape[0]
    tm = MERGE_TILE
    tiles_per_seq = seq_len // tm
    row = lambda i: (i, 0)
    return pl.pallas_call(
        _merge_xattn_kernel,
        out_shape=jax.ShapeDtypeStruct((T, D_MODEL), F32),
        grid=(T // tm,),
        in_specs=[pl.BlockSpec((tm, D_MODEL), row),
                  pl.BlockSpec((tm, MLSTM_V), row), pl.BlockSpec((tm, MLSTM_V), row),
                  pl.BlockSpec((tm, MLSTM_V), lambda i: (i, 1)),
                  pl.BlockSpec((tm, 2 * D_MODEL), lambda i: (i, 2)),
                  pl.BlockSpec((tm, D_RNN), lambda i: (i % tiles_per_seq, i // tiles_per_seq)),
                  pl.BlockSpec((N_MEM, 2 * D_MODEL), lambda i: (i // tiles_per_seq, 0)),
                  _resident((1, MLSTM_V)), _resident((MLSTM_V, D_MODEL)), _resident((D_RNN, D_MODEL)),
                  _resident((D_MODEL, D_MODEL)), _resident((D_MODEL, D_MODEL)),
                  _resident((D_MODEL, D_MODEL)), _resident((2, D_MODEL)), _resident((2, D_MODEL))],
        out_specs=pl.BlockSpec((tm, D_MODEL), row),
        compiler_params=_params("arbitrary"),
        name="merge_xattn",
    )(x, hf, hb, vom, vom, hr, kv, gain, wpm, wpr, wout, wq, wo, lng, lnb)


def _prep_layer(p, l):
    w_in, b_in = p["w_in"][l], p["b_in"][l]
    cols = lambda lo, hi: w_in[:, lo:hi]
    bias = lambda lo, hi: b_in[lo:hi]
    w_g = cols(OFF_GATE, OFF_XR).astype(BF16)
    b_g = bias(OFF_GATE, OFF_XR)
    q_scale = jnp.concatenate([jnp.full((MLSTM_QK,), MLSTM_DK ** -0.5, F32), jnp.ones((MLSTM_QK,), F32)])
    return dict(
        ff1_in=p["ff1_in"][l].astype(BF16), ff1_out=p["ff1_out"][l].astype(BF16),
        ff2_in=p["ff2_in"][l].astype(BF16), ff2_out=p["ff2_out"][l].astype(BF16),
        ln_g=p["ln_g"][l], ln_b=p["ln_b"][l],
        w_qk=cols(0, OFF_V).astype(BF16), b_qk=bias(0, OFF_V)[None],
        cw_qk=p["w_conv_qk"][l], cb_qk=p["b_conv_qk"][l][None], q_scale=q_scale[None],
        w_vom=jnp.concatenate([cols(OFF_V, OFF_GATE), cols(OFF_MG, D_IN)], axis=1).astype(BF16),
        b_vom=jnp.concatenate([bias(OFF_V, OFF_GATE), bias(OFF_MG, D_IN)])[None],
        w_xy=cols(OFF_XR, OFF_MG).astype(BF16), b_xy=bias(OFF_XR, OFF_MG)[None],
        w_g=w_g, w_gt=w_g.T, b_g=b_g[None], b_gt=b_g[:, None],
        cw_r=p["w_conv_r"][l], cb_r=p["b_conv_r"][l][None],
        lru_wa=p["lru_wa"][l].astype(BF16), lru_ba=p["lru_ba"][l],
        lru_wx=p["lru_wx"][l].astype(BF16), lru_bx=p["lru_bx"][l], lru_lam=p["lru_lam"][l],
        mh_gain=p["mh_gain"][l][None],
        w_pm=p["w_pm"][l].astype(BF16), w_pr=p["w_pr"][l].astype(BF16), w_out=p["w_out"][l].astype(BF16),
        xa_wq=p["xa_wq"][l].astype(BF16), xa_wkv=p["xa_wkv"][l].astype(BF16), xa_wo=p["xa_wo"][l].astype(BF16),
        mem_ln_g=p["mem_ln_g"][l][None], mem_ln_b=p["mem_ln_b"][l][None],
    )


def _trunk(x, mem, layers):
    B, S, _ = x.shape
    T = B * S
    x = x.reshape(T, D_MODEL)
    mem = mem.reshape(B * N_MEM, D_MODEL)
    for w in layers:
        x, xb = _ffn_ln(x, w["ff1_in"], w["ff1_out"], w["ln_g"][0:1], w["ln_b"][0:1])
        qk = _qk_conv(xb, w["w_qk"], w["b_qk"], w["cw_qk"], w["cb_qk"], w["q_scale"], S)
        vom = _proj(xb, w["w_vom"], w["b_vom"], BF16, 2048)
        xy = _proj(xb, w["w_xy"], w["b_xy"], F32, 1024, seq_len=S)
        g, gt = _gates(xb, w["w_g"], w["w_gt"], w["b_g"], w["b_gt"])
        hf, hb = _mlstm(qk, vom, g, gt, B, S)
        hr = _rglru(xy.reshape(S, B, 2 * D_RNN), w["cw_r"], w["cb_r"], w["lru_wa"], w["lru_ba"],
                    w["lru_wx"], w["lru_bx"], w["lru_lam"], B, S)
        kv = _mem_kv(mem, w["mem_ln_g"], w["mem_ln_b"], w["xa_wkv"])
        x = _merge_xattn(x, hf, hb, vom, hr.reshape(S, B * D_RNN), kv, w["mh_gain"], w["w_pm"], w["w_pr"],
                         w["w_out"], w["xa_wq"], w["xa_wo"], w["ln_g"][1:3], w["ln_b"][1:3], S)
        x, _ = _ffn_ln(x, w["ff2_in"], w["ff2_out"], w["ln_g"][3:4], w["ln_b"][3:4])
    return x.reshape(B, S, D_MODEL)


def kernel(x_prompt, x_sample, mem_prompt, mem_sample, w_in, b_in, w_conv_qk, b_conv_qk, w_conv_r, b_conv_r,
           mh_gain, lru_wa, lru_ba, lru_wx, lru_bx, lru_lam, w_pm, w_pr, w_out, xa_wq, xa_wkv, xa_wo,
           mem_ln_g, mem_ln_b, ff1_in, ff1_out, ff2_in, ff2_out, ln_g, ln_b):
    p = dict(w_in=w_in, b_in=b_in, w_conv_qk=w_conv_qk, b_conv_qk=b_conv_qk, w_conv_r=w_conv_r,
             b_conv_r=b_conv_r, mh_gain=mh_gain, lru_wa=lru_wa, lru_ba=lru_ba, lru_wx=lru_wx,
             lru_bx=lru_bx, lru_lam=lru_lam, w_pm=w_pm, w_pr=w_pr, w_out=w_out, xa_wq=xa_wq,
             xa_wkv=xa_wkv, xa_wo=xa_wo, mem_ln_g=mem_ln_g, mem_ln_b=mem_ln_b, ff1_in=ff1_in,
             ff1_out=ff1_out, ff2_in=ff2_in, ff2_out=ff2_out, ln_g=ln_g, ln_b=ln_b)
    layers = [_prep_layer(p, l) for l in range(DEPTH)]
    return (_trunk(x_prompt, mem_prompt, layers), _trunk(x_sample, mem_sample, layers))
```

```python
import functools
import math

import jax
import jax.numpy as jnp
from jax import lax
from jax.experimental import pallas as pl
from jax.experimental.pallas import tpu as pltpu

F32 = jnp.float32
BF16 = jnp.bfloat16

D_MODEL = 1024
DEPTH = 2
N_MEM = 256
MLSTM_HEADS = 4
MLSTM_DK = D_MODEL // 4
MLSTM_DV = D_MODEL // 2
MLSTM_QK = MLSTM_HEADS * MLSTM_DK
MLSTM_V = MLSTM_HEADS * MLSTM_DV
D_RNN = D_MODEL
LRU_BLOCKS = 8
LRU_BW = D_RNN // LRU_BLOCKS
LRU_C = 8.0
CONV_W = 4
XATTN_HEADS = 4
XATTN_HD = D_MODEL // XATTN_HEADS
D_FF = 2816
DN_ALPHA = (2.0 * DEPTH) ** 0.25
LN_EPS = 1e-5

OFF_V = 2 * MLSTM_QK
OFF_O = OFF_V + MLSTM_V
OFF_GATE = OFF_O + MLSTM_V
OFF_XR = OFF_GATE + 4 * MLSTM_HEADS
OFF_YR = OFF_XR + D_RNN
OFF_MG = OFF_YR + D_RNN
D_IN = OFF_MG + 2 * D_MODEL
N_GATES = 4 * MLSTM_HEADS

VMEM_LIMIT_BYTES = 56 * 1024 * 1024
LANES = 128
SUBLANES = 8
BF16_ROWS = 16

TOKEN_TILE = 512
MERGE_TILE = 512
FF_CHUNK = D_FF // 2
MLSTM_CHUNK = 256
LRU_TCHUNK = 64
LRU_GROUP = SUBLANES
LRU_UNROLL = 8
CONV_HALO = BF16_ROWS
MLSTM_RW = MLSTM_DV + LANES


def _dot(a, b):
    return jnp.dot(a, b, preferred_element_type=F32)


def _dot_nt(a, b):
    return lax.dot_general(a, b, (((1,), (1,)), ((), ())), preferred_element_type=F32)


def _layer_norm(y, g, b):
    mu = jnp.mean(y, axis=-1, keepdims=True)
    d = y - mu
    var = jnp.mean(d * d, axis=-1, keepdims=True)
    return d * lax.rsqrt(var + LN_EPS) * g + b


def _sigmoid(x):
    return 0.5 + 0.5 * jnp.tanh(0.5 * x)


def _silu(x):
    h = 0.5 * x
    return h + h * jnp.tanh(h)


def _log_sigmoid(x):
    return jnp.minimum(x, 0.0) - jnp.log1p(jnp.exp(-jnp.abs(x)))


def _softplus(x):
    return jnp.maximum(x, 0.0) + jnp.log1p(jnp.exp(-jnp.abs(x)))


def _split3(x):
    hi = x.astype(BF16)
    r = x - hi.astype(F32)
    mid = r.astype(BF16)
    lo = (r - mid.astype(F32)).astype(BF16)
    return hi, mid, lo


def _resident(shape):
    zeros = (0,) * len(shape)
    return pl.BlockSpec(shape, lambda *_: zeros, pipeline_mode=pl.Buffered(1))


def _params(*sem):
    return pltpu.CompilerParams(dimension_semantics=sem, vmem_limit_bytes=VMEM_LIMIT_BYTES)


def _ffn_ln_kernel(x_ref, win_ref, wout_ref, g_ref, b_ref, y_ref, yb_ref):
    x = x_ref[...]
    xb = x.astype(BF16)
    chunks = range(0, D_FF, FF_CHUNK)
    gates = [_dot(xb, win_ref[:, lo:lo + FF_CHUNK]) for lo in chunks]
    ups = [_dot(xb, win_ref[:, D_FF + lo:D_FF + lo + FF_CHUNK]) for lo in chunks]
    acc = jnp.zeros(x.shape, F32)
    for lo, gate, up in zip(chunks, gates, ups):
        h = (_silu(gate) * up).astype(BF16)
        acc = acc + _dot(h, wout_ref[lo:lo + FF_CHUNK, :])
    y = _layer_norm(DN_ALPHA * x + 0.5 * acc, g_ref[...], b_ref[...])
    y_ref[...] = y
    yb_ref[...] = y.astype(BF16)


def _ffn_ln(x, w_in, w_out, g, b):
    T = x.shape[0]
    tm = TOKEN_TILE
    row = lambda i: (i, 0)
    return pl.pallas_call(
        _ffn_ln_kernel,
        out_shape=(jax.ShapeDtypeStruct((T, D_MODEL), F32), jax.ShapeDtypeStruct((T, D_MODEL), BF16)),
        grid=(T // tm,),
        in_specs=[pl.BlockSpec((tm, D_MODEL), row), _resident((D_MODEL, 2 * D_FF)),
                  _resident((D_FF, D_MODEL)), _resident((1, D_MODEL)), _resident((1, D_MODEL))],
        out_specs=(pl.BlockSpec((tm, D_MODEL), row), pl.BlockSpec((tm, D_MODEL), row)),
        compiler_params=_params("arbitrary"),
        name="ffn_ln",
    )(x, w_in, w_out, g, b)


def _proj_kernel(x_ref, w_ref, b_ref, o_ref):
    o_ref[...] = (_dot(x_ref[...], w_ref[...]) + b_ref[...]).astype(o_ref.dtype)


def _proj(xb, w, b, out_dtype, tn, seq_len=None):
    T, N = xb.shape[0], w.shape[1]
    tm = TOKEN_TILE
    if seq_len is None:
        out_shape = (T, N)
        out_map = lambda n, i: (i, n)
    else:
        tiles_per_seq = seq_len // tm
        out_shape = (seq_len, (T // seq_len) * N)
        out_map = lambda n, i: (i % tiles_per_seq, (i // tiles_per_seq) * (N // tn) + n)
    return pl.pallas_call(
        _proj_kernel,
        out_shape=jax.ShapeDtypeStruct(out_shape, out_dtype),
        grid=(N // tn, T // tm),
        in_specs=[pl.BlockSpec((tm, D_MODEL), lambda n, i: (i, 0)),
                  pl.BlockSpec((D_MODEL, tn), lambda n, i: (0, n)),
                  pl.BlockSpec((1, tn), lambda n, i: (0, n))],
        out_specs=pl.BlockSpec((tm, tn), out_map),
        compiler_params=_params("arbitrary", "arbitrary"),
        name="proj",
    )(xb, w, b)


def _conv_proj_kernel(x_ref, xp_ref, xn_ref, w_ref, b_ref, cw_ref, cb_ref, o_ref, xe_ref, ue_ref,
                      *, tiles_per_seq, scale, transpose_out):
    tm = x_ref.shape[0]
    H = CONV_HALO
    xe_ref[0:H, :] = xp_ref[...]
    xe_ref[H:H + tm, :] = x_ref[...]
    xe_ref[H + tm:, :] = xn_ref[...]
    ue_ref[...] = _dot(xe_ref[...], w_ref[...]) + b_ref[...]
    pos = pl.program_id(0) % tiles_per_seq

    @pl.when(pos == 0)
    def _():
        ue_ref[0:H, :] = jnp.zeros((H, ue_ref.shape[1]), F32)

    @pl.when(pos == tiles_per_seq - 1)
    def _():
        ue_ref[H + tm:, :] = jnp.zeros((H, ue_ref.shape[1]), F32)

    out = cb_ref[...]
    for j in range(CONV_W):
        out = out + ue_ref[pl.ds(H - 1 + j, tm), :] * cw_ref[j:j + 1, :]
    out = _silu(out)
    if scale != 1.0:
        out = out * scale
    if transpose_out:
        out = out.T
    o_ref[...] = out.astype(o_ref.dtype)


def _conv_proj(xb, w, b, cw, cb, seq_len, scale, transpose_out):
    T, N = xb.shape[0], w.shape[1]
    tm, H = TOKEN_TILE, CONV_HALO
    tiles_per_seq = seq_len // tm
    hpt = tm // H
    last_halo = T // H - 1
    if transpose_out:
        out_shape, out_spec = (N, T), pl.BlockSpec((N, tm), lambda i: (0, i))
    else:
        out_shape, out_spec = (T, N), pl.BlockSpec((tm, N), lambda i: (i, 0))
    return pl.pallas_call(
        functools.partial(_conv_proj_kernel, tiles_per_seq=tiles_per_seq, scale=scale,
                          transpose_out=transpose_out),
        out_shape=jax.ShapeDtypeStruct(out_shape, BF16),
        grid=(T // tm,),
        in_specs=[pl.BlockSpec((tm, D_MODEL), lambda i: (i, 0)),
                  pl.BlockSpec((H, D_MODEL), lambda i: (jnp.maximum(i * hpt - 1, 0), 0)),
                  pl.BlockSpec((H, D_MODEL), lambda i: (jnp.minimum((i + 1) * hpt, last_halo), 0)),
                  _resident((D_MODEL, N)), _resident((1, N)), _resident((CONV_W, N)), _resident((1, N))],
        out_specs=out_spec,
        scratch_shapes=[pltpu.VMEM((tm + 2 * H, D_MODEL), BF16), pltpu.VMEM((tm + 2 * H, N), F32)],
        compiler_params=_params("arbitrary"),
        name="conv_proj",
    )(xb, xb, xb, w, b, cw, cb)


def _gates_kernel(x_ref, w_ref, wt_ref, b_ref, bt_ref, gc_ref, gr_ref):
    L = MLSTM_CHUNK
    H = MLSTM_HEADS
    x = x_ref[...]
    g = _dot(x, w_ref[...]) + b_ref[...]
    gt = _dot_nt(wt_ref[...], x) + bt_ref[...]
    ls_c = _log_sigmoid(g)
    ls_r = _log_sigmoid(gt)
    row = lax.broadcasted_iota(jnp.int32, (L, L), 0)
    col = lax.broadcasted_iota(jnp.int32, (L, L), 1)
    tril = (col <= row).astype(BF16)
    triu = (col >= row).astype(BF16)
    c_idx = lax.broadcasted_iota(jnp.int32, (L, N_GATES), 1)
    r_idx = lax.broadcasted_iota(jnp.int32, (N_GATES, L), 0)
    for k in range(x.shape[0] // L):
        sl = slice(k * L, (k + 1) * L)
        lc = ls_c[sl, :]
        pre = sum(_dot(tril, p) for p in _split3(lc))
        suf = pre[L - 1:L, :] - pre + lc
        gc_ref[sl, :] = jnp.where(c_idx >= 2 * H, suf, pre)
        lr = ls_r[:, sl]
        pre = sum(_dot(p, triu) for p in _split3(lr))
        suf = pre[:, L - 1:L] - pre + lr
        cum = jnp.where(r_idx >= 2 * H, suf, pre)
        gr_ref[:, sl] = jnp.where(r_idx % (2 * H) >= H, cum, gt[:, sl])


def _gates(xb, w, wt, b, bt):
    T = xb.shape[0]
    tm = TOKEN_TILE
    return pl.pallas_call(
        _gates_kernel,
        out_shape=(jax.ShapeDtypeStruct((T, N_GATES), F32), jax.ShapeDtypeStruct((N_GATES, T), F32)),
        grid=(T // tm,),
        in_specs=[pl.BlockSpec((tm, D_MODEL), lambda i: (i, 0)),
                  _resident((D_MODEL, N_GATES)), _resident((N_GATES, D_MODEL)),
                  _resident((1, N_GATES)), _resident((N_GATES, 1))],
        out_specs=(pl.BlockSpec((tm, N_GATES), lambda i: (i, 0)),
                   pl.BlockSpec((N_GATES, tm), lambda i: (0, i))),
        compiler_params=_params("arbitrary"),
        name="mlstm_gates",
    )(xb, w, wt, b, bt)


def _mlstm_kernel(q_ref, kt_ref, v_ref, o_ref, gc_ref, gr_ref, gain_ref, out_ref,
                  hb_ref, hacc_ref, c_ref, r_ref, n_ref, m_ref, *, nc):
    L = q_ref.shape[0]
    H, DK, DV = MLSTM_HEADS, MLSTM_DK, MLSTM_DV
    p = pl.program_id(1)
    c = pl.program_id(2)
    cc = jnp.where(p == 0, nc - 1 - c, c)
    fwd = p == 1

    @pl.when(c == 0)
    def _():
        c_ref[...] = jnp.zeros(c_ref.shape, F32)
        n_ref[...] = jnp.zeros(n_ref.shape, F32)
        m_ref[...] = jnp.zeros(m_ref.shape, F32)
        r_ref[:, DK:, :] = jnp.zeros((H, DK, MLSTM_RW), BF16)
        r_ref[:, :DK, DV:] = jnp.ones((H, DK, LANES), BF16)

    row = lax.broadcasted_iota(jnp.int32, (L, L), 0)
    col = lax.broadcasted_iota(jnp.int32, (L, L), 1)
    mask = (row - col) * (2 * p - 1) >= 0
    gc = jnp.where(fwd, gc_ref[:, 0:2 * H], gc_ref[:, 2 * H:])
    gr = jnp.where(fwd, gr_ref[0:2 * H, :], gr_ref[2 * H:, :])

    gcols = [gc[:, H + h:H + h + 1] for h in range(H)]
    brows = [gr[h:h + 1, :] - gr[H + h:H + h + 1, :] for h in range(H)]
    ms = [m_ref[h:h + 1, 0:1] for h in range(H)]
    qs = [q_ref[:, h * DK:(h + 1) * DK] for h in range(H)]
    kts = [kt_ref[h * DK:(h + 1) * DK, :] for h in range(H)]
    vs = [v_ref[:, h * DV:(h + 1) * DV] for h in range(H)]

    qks = [_dot(qs[h], kts[h]) for h in range(H)]

    for h in range(H):
        log_d = jnp.where(mask, gcols[h] + brows[h], -jnp.inf)
        log_inter = gcols[h] + ms[h]
        m_t = jnp.maximum(log_inter, jnp.max(log_d, axis=-1, keepdims=True))
        s = qks[h] * jnp.exp(log_d - m_t)
        w_inter = jnp.exp(log_inter - m_t)
        lhs = jnp.concatenate([s.astype(BF16), (w_inter * qs[h].astype(F32)).astype(BF16)], axis=1)
        r_ref[h, :DK, :DV] = vs[h]
        res = _dot(lhs, r_ref[h])
        den = res[:, DV:DV + 1]
        inv = 1.0 / jnp.maximum(jnp.abs(den), jnp.exp(-m_t))
        hacc_ref[:, h * DV:(h + 1) * DV] = res[:, :DV] * inv

    for h in range(H):
        grow = gr[H + h:H + h + 1, :]
        g_tot = jnp.where(fwd, grow[:, L - 1:L], grow[:, 0:1])
        m_new = jnp.maximum(g_tot + ms[h], jnp.max(g_tot + brows[h], axis=-1, keepdims=True))
        wkt = kts[h].astype(F32) * jnp.exp(g_tot + brows[h] - m_new)
        decay = jnp.exp(g_tot + ms[h] - m_new)
        c_new = decay * c_ref[h] + _dot(wkt.astype(BF16), vs[h])
        n_new = decay * n_ref[h] + jnp.sum(wkt, axis=-1, keepdims=True)
        c_ref[h] = c_new
        n_ref[h] = n_new
        r_ref[h, DK:, :DV] = c_new.astype(BF16)
        r_ref[h, DK:, DV:] = n_new.astype(BF16)
        m_ref[h:h + 1, :] = jnp.broadcast_to(m_new, (1, m_ref.shape[1]))

    @pl.when(p == 0)
    def _():
        hb_ref[cc] = hacc_ref[...].astype(BF16)

    @pl.when(p == 1)
    def _():
        for h in range(H):
            sl = slice(h * DV, (h + 1) * DV)
            hh = hacc_ref[:, sl] + hb_ref[cc, :, sl].astype(F32)
            mu = jnp.mean(hh, axis=-1, keepdims=True)
            d = hh - mu
            var = jnp.mean(d * d, axis=-1, keepdims=True)
            hn = d * lax.rsqrt(var + LN_EPS) * gain_ref[:, sl]
            out_ref[:, sl] = (_sigmoid(o_ref[:, sl].astype(F32)) * hn).astype(out_ref.dtype)


def _mlstm(q, kt, vom, gc, gr, gain, batch, seq_len):
    T = q.shape[0]
    L = MLSTM_CHUNK
    nc = seq_len // L
    H = MLSTM_HEADS
    pos = lambda b, p, c: b * nc + jnp.where(p == 0, nc - 1 - c, c)
    park = lambda b, p, c: b * nc + c * p
    return pl.pallas_call(
        functools.partial(_mlstm_kernel, nc=nc),
        out_shape=jax.ShapeDtypeStruct((T, MLSTM_V), BF16),
        grid=(batch, 2, nc),
        in_specs=[pl.BlockSpec((L, MLSTM_QK), lambda b, p, c: (pos(b, p, c), 0)),
                  pl.BlockSpec((MLSTM_QK, L), lambda b, p, c: (0, pos(b, p, c))),
                  pl.BlockSpec((L, MLSTM_V), lambda b, p, c: (pos(b, p, c), 0)),
                  pl.BlockSpec((L, MLSTM_V), lambda b, p, c: (park(b, p, c), 1)),
                  pl.BlockSpec((L, N_GATES), lambda b, p, c: (pos(b, p, c), 0)),
                  pl.BlockSpec((N_GATES, L), lambda b, p, c: (0, pos(b, p, c))),
                  _resident((1, MLSTM_V))],
        out_specs=pl.BlockSpec((L, MLSTM_V), lambda b, p, c: (park(b, p, c), 0)),
        scratch_shapes=[pltpu.VMEM((nc, L, MLSTM_V), BF16),
                        pltpu.VMEM((L, MLSTM_V), F32),
                        pltpu.VMEM((H, MLSTM_DK, MLSTM_DV), F32),
                        pltpu.VMEM((H, 2 * MLSTM_DK, MLSTM_RW), BF16),
                        pltpu.VMEM((H, MLSTM_DK, LANES), F32),
                        pltpu.VMEM((SUBLANES, LANES), F32)],
        compiler_params=_params("arbitrary", "arbitrary", "arbitrary"),
        name="mlstm",
    )(q, kt, vom, vom, gc, gr, gain)


def _rglru_kernel(*refs, nc, reverse):
    if reverse:
        x_ref, xp_ref, xn_ref, cw_ref, cb_ref, wa_ref, ba_ref, wx_ref, bx_ref, lam_ref, o_ref, \
            a_ref, u_ref, state_ref, k_ref = refs
    else:
        x_ref, xp_ref, xn_ref, y_ref, hb_ref, cw_ref, cb_ref, wa_ref, ba_ref, wx_ref, bx_ref, lam_ref, \
            o_ref, a_ref, u_ref, state_ref, k_ref, gy_ref = refs
    tc = x_ref.shape[0]
    c = pl.program_id(1)
    cc = nc - 1 - c if reverse else c

    @pl.when(c == 0)
    def _():
        state_ref[...] = jnp.zeros(state_ref.shape, F32)
        k_ref[...] = (-0.5 * LRU_C * math.log2(math.e)) * _softplus(-lam_ref[...])

    prev = jnp.where(cc > 0, xp_ref[...], 0.0)
    nxt = jnp.where(cc < nc - 1, xn_ref[...], 0.0)
    xe = jnp.concatenate([prev, x_ref[...], nxt], axis=0)
    xc = cb_ref[...]
    for j in range(CONV_W):
        xc = xc + xe[j:j + tc] * cw_ref[j:j + 1, :]

    x2 = xc.reshape(tc * LRU_GROUP, D_RNN)
    xb = x2.astype(BF16)
    for n in range(LRU_BLOCKS):
        sl = slice(n * LRU_BW, (n + 1) * LRU_BW)
        t_a = jnp.tanh(_dot(xb[:, sl], wa_ref[n]) + ba_ref[:, sl])
        t_i = jnp.tanh(_dot(xb[:, sl], wx_ref[n]) + bx_ref[:, sl])
        a = jnp.exp2(k_ref[:, sl] + k_ref[:, sl] * t_a)
        y = 1.0 - a * a
        root = jnp.where(y > 0.0, y * lax.rsqrt(y), 0.0)
        hx = 0.5 * x2[:, sl]
        u = root * (hx + hx * t_i)
        a_ref[:, :, sl] = a.reshape(tc, LRU_GROUP, LRU_BW)
        u_ref[:, :, sl] = u.reshape(tc, LRU_GROUP, LRU_BW)

    if reverse:
        def step(k, h):
            t = tc - 1 - k
            h = a_ref[t] * h + u_ref[t]
            o_ref[t] = h
            return h
    else:
        gy_ref[...] = jax.nn.gelu(y_ref[...])

        def step(t, h):
            h = a_ref[t] * h + u_ref[t]
            o_ref[t] = (h + hb_ref[t]) * gy_ref[t]
            return h

    state_ref[...] = lax.fori_loop(0, tc, step, state_ref[...], unroll=LRU_UNROLL)


def _rglru_dir(xy, hb, cw, cb, wa, ba, wx, bx, lam, batch, seq_len, reverse):
    tc, G, C = LRU_TCHUNK, LRU_GROUP, D_RNN
    nc = seq_len // tc
    chunk = (lambda c: nc - 1 - c) if reverse else (lambda c: c)
    blk = lambda col: pl.BlockSpec((tc, G, C), lambda g, c: (chunk(c), g, col))
    in_specs = [blk(0),
                pl.BlockSpec((1, G, C), lambda g, c: (jnp.maximum(chunk(c) * tc - 1, 0), g, 0)),
                pl.BlockSpec((2, G, C),
                             lambda g, c: (jnp.minimum((chunk(c) + 1) * (tc // 2), seq_len // 2 - 1), g, 0))]
    args = [xy, xy, xy]
    scratch = [pltpu.VMEM((tc, G, C), F32), pltpu.VMEM((tc, G, C), F32),
               pltpu.VMEM((G, C), F32), pltpu.VMEM((1, C), F32)]
    if not reverse:
        in_specs += [blk(1), blk(0)]
        args += [xy, hb]
        scratch += [pltpu.VMEM((tc, G, C), F32)]
    in_specs += [_resident((CONV_W, C)), _resident((1, C)),
                 _resident((LRU_BLOCKS, LRU_BW, LRU_BW)), _resident((1, C)),
                 _resident((LRU_BLOCKS, LRU_BW, LRU_BW)), _resident((1, C)), _resident((1, C))]
    args += [cw, cb, wa, ba, wx, bx, lam]
    return pl.pallas_call(
        functools.partial(_rglru_kernel, nc=nc, reverse=reverse),
        out_shape=jax.ShapeDtypeStruct((seq_len, batch, C), F32),
        grid=(batch // G, nc),
        in_specs=in_specs,
        out_specs=blk(0),
        scratch_shapes=scratch,
        compiler_params=_params("arbitrary", "arbitrary"),
        name="rglru_bwd" if reverse else "rglru_fwd",
    )(*args)


def _rglru(xy, w, batch, seq_len):
    common = (w["cw_r"], w["cb_r"])
    hb = _rglru_dir(xy, None, *common, w["lru_wa"][1], w["lru_ba"][1:2], w["lru_wx"][1], w["lru_bx"][1:2],
                    w["lru_lam"][1:2], batch, seq_len, True)
    return _rglru_dir(xy, hb, *common, w["lru_wa"][0], w["lru_ba"][0:1], w["lru_wx"][0], w["lru_bx"][0:1],
                      w["lru_lam"][0:1], batch, seq_len, False)


def _kv_kernel(m_ref, g_ref, b_ref, w_ref, o_ref):
    m = _layer_norm(m_ref[...], g_ref[...], b_ref[...])
    o_ref[...] = _dot(m.astype(BF16), w_ref[...]).astype(o_ref.dtype)


def _mem_kv(mem, g, b, w_kv):
    R = mem.shape[0]
    return pl.pallas_call(
        _kv_kernel,
        out_shape=jax.ShapeDtypeStruct((R, 2 * D_MODEL), BF16),
        grid=(R // N_MEM,),
        in_specs=[pl.BlockSpec((N_MEM, D_MODEL), lambda i: (i, 0)), _resident((1, D_MODEL)),
                  _resident((1, D_MODEL)), _resident((D_MODEL, 2 * D_MODEL))],
        out_specs=pl.BlockSpec((N_MEM, 2 * D_MODEL), lambda i: (i, 0)),
        compiler_params=_params("arbitrary"),
        name="mem_kv",
    )(mem, g, b, w_kv)


def _merge_xattn_kernel(x_ref, hm_ref, mg_ref, hr_ref, kv_ref,
                        wpm_ref, wpr_ref, wout_ref, wq_ref, wo_ref, lng_ref, lnb_ref, y_ref):
    half = x_ref.shape[0] // 2
    rows = [slice(0, half), slice(half, 2 * half)]
    pm = [_dot(hm_ref[r, :], wpm_ref[...]) for r in rows]
    pr = [_dot(hr_ref[r, :].astype(BF16), wpr_ref[...]) for r in rows]
    merged = []
    for r, a, b in zip(rows, pm, pr):
        g_m = _sigmoid(mg_ref[r, :D_MODEL].astype(F32))
        g_r = _sigmoid(mg_ref[r, D_MODEL:].astype(F32))
        merged.append((g_m * a + g_r * b).astype(BF16))
    mix = [_dot(m, wout_ref[...]) for m in merged]
    xs = [_layer_norm(DN_ALPHA * x_ref[r, :] + m, lng_ref[0:1, :], lnb_ref[0:1, :]) for r, m in zip(rows, mix)]

    qs = [_dot(x.astype(BF16), wq_ref[...]).astype(BF16) for x in xs]
    heads = [[], []]
    for h in range(XATTN_HEADS):
        sl = slice(h * XATTN_HD, (h + 1) * XATTN_HD)
        ss = [_dot_nt(q[:, sl], kv_ref[:, sl]) * (XATTN_HD ** -0.5) for q in qs]
        for i, s in enumerate(ss):
            e = jnp.exp(s - jnp.max(s, axis=-1, keepdims=True))
            p = e * (1.0 / jnp.sum(e, axis=-1, keepdims=True))
            oh = _dot(p.astype(BF16), kv_ref[:, D_MODEL + h * XATTN_HD:D_MODEL + (h + 1) * XATTN_HD])
            heads[i].append(oh.astype(BF16))
    for r, x, ohs in zip(rows, xs, heads):
        xa = _dot(jnp.concatenate(ohs, axis=1), wo_ref[...])
        y_ref[r, :] = _layer_norm(DN_ALPHA * x + xa, lng_ref[1:2, :], lnb_ref[1:2, :])


def _merge_xattn(x, hm, vom, hr, kv, wpm, wpr, wout, wq, wo, lng, lnb, seq_len):
    T = x.shape[0]
    tm = MERGE_TILE
    tiles_per_seq = seq_len // tm
    row = lambda i: (i, 0)
    return pl.pallas_call(
        _merge_xattn_kernel,
        out_shape=jax.ShapeDtypeStruct((T, D_MODEL), F32),
        grid=(T // tm,),
        in_specs=[pl.BlockSpec((tm, D_MODEL), row),
                  pl.BlockSpec((tm, MLSTM_V), row),
                  pl.BlockSpec((tm, 2 * D_MODEL), lambda i: (i, 2)),
                  pl.BlockSpec((tm, D_RNN), lambda i: (i % tiles_per_seq, i // tiles_per_seq)),
                  pl.BlockSpec((N_MEM, 2 * D_MODEL), lambda i: (i // tiles_per_seq, 0)),
                  _resident((MLSTM_V, D_MODEL)), _resident((D_RNN, D_MODEL)),
                  _resident((D_MODEL, D_MODEL)), _resident((D_MODEL, D_MODEL)),
                  _resident((D_MODEL, D_MODEL)), _resident((2, D_MODEL)), _resident((2, D_MODEL))],
        out_specs=pl.BlockSpec((tm, D_MODEL), row),
        compiler_params=_params("arbitrary"),
        name="merge_xattn",
    )(x, hm, vom, hr, kv, wpm, wpr, wout, wq, wo, lng, lnb)


def _prep_layer(p, l):
    w_in, b_in = p["w_in"][l], p["b_in"][l]
    cols = lambda lo, hi: w_in[:, lo:hi]
    bias = lambda lo, hi: b_in[lo:hi]
    w_g = cols(OFF_GATE, OFF_XR).astype(BF16)
    b_g = bias(OFF_GATE, OFF_XR)
    return dict(
        ff1_in=p["ff1_in"][l].astype(BF16), ff1_out=p["ff1_out"][l].astype(BF16),
        ff2_in=p["ff2_in"][l].astype(BF16), ff2_out=p["ff2_out"][l].astype(BF16),
        ln_g=p["ln_g"][l], ln_b=p["ln_b"][l],
        w_q=cols(0, MLSTM_QK).astype(BF16), b_q=bias(0, MLSTM_QK)[None],
        cw_q=p["w_conv_qk"][l][:, :MLSTM_QK], cb_q=p["b_conv_qk"][l][None, :MLSTM_QK],
        w_k=cols(MLSTM_QK, OFF_V).astype(BF16), b_k=bias(MLSTM_QK, OFF_V)[None],
        cw_k=p["w_conv_qk"][l][:, MLSTM_QK:], cb_k=p["b_conv_qk"][l][None, MLSTM_QK:],
        w_vom=jnp.concatenate([cols(OFF_V, OFF_GATE), cols(OFF_MG, D_IN)], axis=1).astype(BF16),
        b_vom=jnp.concatenate([bias(OFF_V, OFF_GATE), bias(OFF_MG, D_IN)])[None],
        w_xy=cols(OFF_XR, OFF_MG).astype(BF16), b_xy=bias(OFF_XR, OFF_MG)[None],
        w_g=w_g, w_gt=w_g.T, b_g=b_g[None], b_gt=b_g[:, None],
        cw_r=p["w_conv_r"][l], cb_r=p["b_conv_r"][l][None],
        lru_wa=(0.5 * p["lru_wa"][l]).astype(BF16), lru_ba=0.5 * p["lru_ba"][l],
        lru_wx=(0.5 * p["lru_wx"][l]).astype(BF16), lru_bx=0.5 * p["lru_bx"][l], lru_lam=p["lru_lam"][l],
        mh_gain=p["mh_gain"][l][None],
        w_pm=p["w_pm"][l].astype(BF16), w_pr=p["w_pr"][l].astype(BF16), w_out=p["w_out"][l].astype(BF16),
        xa_wq=p["xa_wq"][l].astype(BF16), xa_wkv=p["xa_wkv"][l].astype(BF16), xa_wo=p["xa_wo"][l].astype(BF16),
        mem_ln_g=p["mem_ln_g"][l][None], mem_ln_b=p["mem_ln_b"][l][None],
    )


def _trunk(x, mem, layers):
    B, S, _ = x.shape
    T = B * S
    x = x.reshape(T, D_MODEL)
    mem = mem.reshape(B * N_MEM, D_MODEL)
    for w in layers:
        x, xb = _ffn_ln(x, w["ff1_in"], w["ff1_out"], w["ln_g"][0:1], w["ln_b"][0:1])
        q = _conv_proj(xb, w["w_q"], w["b_q"], w["cw_q"], w["cb_q"], S, MLSTM_DK ** -0.5, False)
        kt = _conv_proj(xb, w["w_k"], w["b_k"], w["cw_k"], w["cb_k"], S, 1.0, True)
        vom = _proj(xb, w["w_vom"], w["b_vom"], BF16, 2048)
        xy = _proj(xb, w["w_xy"], w["b_xy"], F32, 1024, seq_len=S)
        gc, gr = _gates(xb, w["w_g"], w["w_gt"], w["b_g"], w["b_gt"])
        hm = _mlstm(q, kt, vom, gc, gr, w["mh_gain"], B, S)
        hr = _rglru(xy.reshape(S, B, 2 * D_RNN), w, B, S)
        kv = _mem_kv(mem, w["mem_ln_g"], w["mem_ln_b"], w["xa_wkv"])
        x = _merge_xattn(x, hm, vom, hr.reshape(S, B * D_RNN), kv, w["w_pm"], w["w_pr"],
                         w["w_out"], w["xa_wq"], w["xa_wo"], w["ln_g"][1:3], w["ln_b"][1:3], S)
        x, _ = _ffn_ln(x, w["ff2_in"], w["ff2_out"], w["ln_g"][3:4], w["ln_b"][3:4])
    return x.reshape(B, S, D_MODEL)


def kernel(x_prompt, x_sample, mem_prompt, mem_sample, w_in, b_in, w_conv_qk, b_conv_qk, w_conv_r, b_conv_r,
           mh_gain, lru_wa, lru_ba, lru_wx, lru_bx, lru_lam, w_pm, w_pr, w_out, xa_wq, xa_wkv, xa_wo,
           mem_ln_g, mem_ln_b, ff1_in, ff1_out, ff2_in, ff2_out, ln_g, ln_b):
    p = dict(w_in=w_in, b_in=b_in, w_conv_qk=w_conv_qk, b_conv_qk=b_conv_qk, w_conv_r=w_conv_r,
             b_conv_r=b_conv_r, mh_gain=mh_gain, lru_wa=lru_wa, lru_ba=lru_ba, lru_wx=lru_wx,
             lru_bx=lru_bx, lru_lam=lru_lam, w_pm=w_pm, w_pr=w_pr, w_out=w_out, xa_wq=xa_wq,
             xa_wkv=xa_wkv, xa_wo=xa_wo, mem_ln_g=mem_ln_g, mem_ln_b=mem_ln_b, ff1_in=ff1_in,
             ff1_out=ff1_out, ff2_in=ff2_in, ff2_out=ff2_out, ln_g=ln_g, ln_b=ln_b)
    layers = [_prep_layer(p, l) for l in range(DEPTH)]
    return (_trunk(x_prompt, mem_prompt, layers), _trunk(x_sample, mem_sample, layers))
```

```python
import functools
import math

import jax
import jax.numpy as jnp
from jax import lax
from jax.experimental import pallas as pl
from jax.experimental.pallas import tpu as pltpu

F32 = jnp.float32
BF16 = jnp.bfloat16

D_MODEL = 1024
DEPTH = 2
N_MEM = 256
MLSTM_HEADS = 4
MLSTM_DK = D_MODEL // 4
MLSTM_DV = D_MODEL // 2
MLSTM_QK = MLSTM_HEADS * MLSTM_DK
MLSTM_V = MLSTM_HEADS * MLSTM_DV
D_RNN = D_MODEL
LRU_BLOCKS = 8
LRU_BW = D_RNN // LRU_BLOCKS
LRU_C = 8.0
CONV_W = 4
XATTN_HEADS = 4
XATTN_HD = D_MODEL // XATTN_HEADS
D_FF = 2816
DN_ALPHA = (2.0 * DEPTH) ** 0.25
LN_EPS = 1e-5

OFF_V = 2 * MLSTM_QK
OFF_O = OFF_V + MLSTM_V
OFF_GATE = OFF_O + MLSTM_V
OFF_XR = OFF_GATE + 4 * MLSTM_HEADS
OFF_YR = OFF_XR + D_RNN
OFF_MG = OFF_YR + D_RNN
D_IN = OFF_MG + 2 * D_MODEL
N_GATES = 4 * MLSTM_HEADS

VMEM_LIMIT_BYTES = 56 * 1024 * 1024
LANES = 128
SUBLANES = 8
BF16_ROWS = 16

TOKEN_TILE = 512
PROJ_TILE = 1024
MERGE_TILE = 512
FF_CHUNK = D_FF // 2
MLSTM_CHUNK = 256
LRU_TCHUNK = 64
LRU_GROUP = SUBLANES
LRU_UNROLL = 8
CONV_HALO = BF16_ROWS
MLSTM_RW = MLSTM_DV + LANES
LOG2E = math.log2(math.e)


def _dot(a, b):
    return jnp.dot(a, b, preferred_element_type=F32)


def _dot_nt(a, b):
    return lax.dot_general(a, b, (((1,), (1,)), ((), ())), preferred_element_type=F32)


def _dot_tn(a, b):
    return lax.dot_general(a, b, (((0,), (0,)), ((), ())), preferred_element_type=F32)


def _layer_norm(y, g, b):
    mu = jnp.mean(y, axis=-1, keepdims=True)
    d = y - mu
    var = jnp.mean(d * d, axis=-1, keepdims=True)
    return d * lax.rsqrt(var + LN_EPS) * g + b


def _sigmoid(x):
    return 0.5 + 0.5 * jnp.tanh(0.5 * x)


def _silu(x):
    h = 0.5 * x
    return h + h * jnp.tanh(h)


def _log_sigmoid(x):
    return jnp.minimum(x, 0.0) - jnp.log1p(jnp.exp(-jnp.abs(x)))


def _softplus(x):
    return jnp.maximum(x, 0.0) + jnp.log1p(jnp.exp(-jnp.abs(x)))


def _split3(x):
    hi = x.astype(BF16)
    r = x - hi.astype(F32)
    mid = r.astype(BF16)
    lo = (r - mid.astype(F32)).astype(BF16)
    return hi, mid, lo


def _resident(shape):
    zeros = (0,) * len(shape)
    return pl.BlockSpec(shape, lambda *_: zeros, pipeline_mode=pl.Buffered(1))


def _params(*sem):
    return pltpu.CompilerParams(dimension_semantics=sem, vmem_limit_bytes=VMEM_LIMIT_BYTES)


def _ffn_ln_kernel(x_ref, win_ref, wout_ref, g_ref, b_ref, y_ref, yb_ref):
    x = x_ref[...]
    xb = x.astype(BF16)
    chunks = range(0, D_FF, FF_CHUNK)
    gates = [_dot(xb, win_ref[:, lo:lo + FF_CHUNK]) for lo in chunks]
    ups = [_dot(xb, win_ref[:, D_FF + lo:D_FF + lo + FF_CHUNK]) for lo in chunks]
    acc = jnp.zeros(x.shape, F32)
    for lo, gate, up in zip(chunks, gates, ups):
        h = (_silu(gate) * up).astype(BF16)
        acc = acc + _dot(h, wout_ref[lo:lo + FF_CHUNK, :])
    y = _layer_norm(DN_ALPHA * x + 0.5 * acc, g_ref[...], b_ref[...])
    y_ref[...] = y
    yb_ref[...] = y.astype(BF16)


def _ffn_ln(x, w_in, w_out, g, b):
    T = x.shape[0]
    tm = TOKEN_TILE
    row = lambda i: (i, 0)
    return pl.pallas_call(
        _ffn_ln_kernel,
        out_shape=(jax.ShapeDtypeStruct((T, D_MODEL), F32), jax.ShapeDtypeStruct((T, D_MODEL), BF16)),
        grid=(T // tm,),
        in_specs=[pl.BlockSpec((tm, D_MODEL), row), _resident((D_MODEL, 2 * D_FF)),
                  _resident((D_FF, D_MODEL)), _resident((1, D_MODEL)), _resident((1, D_MODEL))],
        out_specs=(pl.BlockSpec((tm, D_MODEL), row), pl.BlockSpec((tm, D_MODEL), row)),
        compiler_params=_params("arbitrary"),
        name="ffn_ln",
    )(x, w_in, w_out, g, b)


def _proj_kernel(x_ref, w_ref, b_ref, o_ref):
    o_ref[...] = (_dot(x_ref[...], w_ref[...]) + b_ref[...]).astype(o_ref.dtype)


def _proj(xb, w, b, out_dtype, tn, seq_len=None):
    T, N = xb.shape[0], w.shape[1]
    tm = PROJ_TILE
    if seq_len is None:
        out_shape = (T, N)
        out_map = lambda n, i: (i, n)
    else:
        tiles_per_seq = seq_len // tm
        out_shape = (seq_len, (T // seq_len) * N)
        out_map = lambda n, i: (i % tiles_per_seq, (i // tiles_per_seq) * (N // tn) + n)
    return pl.pallas_call(
        _proj_kernel,
        out_shape=jax.ShapeDtypeStruct(out_shape, out_dtype),
        grid=(N // tn, T // tm),
        in_specs=[pl.BlockSpec((tm, D_MODEL), lambda n, i: (i, 0)),
                  pl.BlockSpec((D_MODEL, tn), lambda n, i: (0, n)),
                  pl.BlockSpec((1, tn), lambda n, i: (0, n))],
        out_specs=pl.BlockSpec((tm, tn), out_map),
        compiler_params=_params("arbitrary", "arbitrary"),
        name="proj",
    )(xb, w, b)


def _proj_t_kernel(x_ref, wt_ref, b_ref, o_ref):
    o_ref[...] = (_dot_nt(wt_ref[...], x_ref[...]) + b_ref[...]).astype(o_ref.dtype)


def _proj_t(xb, wt, b_col, out_dtype, tn):
    T, N = xb.shape[0], wt.shape[0]
    tm = PROJ_TILE
    return pl.pallas_call(
        _proj_t_kernel,
        out_shape=jax.ShapeDtypeStruct((N, T), out_dtype),
        grid=(N // tn, T // tm),
        in_specs=[pl.BlockSpec((tm, D_MODEL), lambda n, i: (i, 0)),
                  pl.BlockSpec((tn, D_MODEL), lambda n, i: (n, 0)),
                  pl.BlockSpec((tn, 1), lambda n, i: (n, 0))],
        out_specs=pl.BlockSpec((tn, tm), lambda n, i: (n, i)),
        compiler_params=_params("arbitrary", "arbitrary"),
        name="proj_t",
    )(xb, wt, b_col)


def _conv_proj_kernel(x_ref, xp_ref, xn_ref, w_ref, b_ref, cw_ref, cb_ref, o_ref, xe_ref, ue_ref,
                      *, tiles_per_seq, scale, transpose_out):
    tm = x_ref.shape[0]
    H = CONV_HALO
    xe_ref[0:H, :] = xp_ref[...]
    xe_ref[H:H + tm, :] = x_ref[...]
    xe_ref[H + tm:, :] = xn_ref[...]
    ue_ref[...] = _dot(xe_ref[...], w_ref[...]) + b_ref[...]
    pos = pl.program_id(0) % tiles_per_seq

    @pl.when(pos == 0)
    def _():
        ue_ref[0:H, :] = jnp.zeros((H, ue_ref.shape[1]), F32)

    @pl.when(pos == tiles_per_seq - 1)
    def _():
        ue_ref[H + tm:, :] = jnp.zeros((H, ue_ref.shape[1]), F32)

    out = cb_ref[...]
    for j in range(CONV_W):
        out = out + ue_ref[pl.ds(H - 1 + j, tm), :] * cw_ref[j:j + 1, :]
    out = _silu(out)
    if scale != 1.0:
        out = out * scale
    if transpose_out:
        out = out.T
    o_ref[...] = out.astype(o_ref.dtype)


def _conv_proj(xb, w, b, cw, cb, seq_len, scale, transpose_out):
    T, N = xb.shape[0], w.shape[1]
    tm, H = TOKEN_TILE, CONV_HALO
    tiles_per_seq = seq_len // tm
    hpt = tm // H
    last_halo = T // H - 1
    if transpose_out:
        out_shape, out_spec = (N, T), pl.BlockSpec((N, tm), lambda i: (0, i))
    else:
        out_shape, out_spec = (T, N), pl.BlockSpec((tm, N), lambda i: (i, 0))
    return pl.pallas_call(
        functools.partial(_conv_proj_kernel, tiles_per_seq=tiles_per_seq, scale=scale,
                          transpose_out=transpose_out),
        out_shape=jax.ShapeDtypeStruct(out_shape, BF16),
        grid=(T // tm,),
        in_specs=[pl.BlockSpec((tm, D_MODEL), lambda i: (i, 0)),
                  pl.BlockSpec((H, D_MODEL), lambda i: (jnp.maximum(i * hpt - 1, 0), 0)),
                  pl.BlockSpec((H, D_MODEL), lambda i: (jnp.minimum((i + 1) * hpt, last_halo), 0)),
                  _resident((D_MODEL, N)), _resident((1, N)), _resident((CONV_W, N)), _resident((1, N))],
        out_specs=out_spec,
        scratch_shapes=[pltpu.VMEM((tm + 2 * H, D_MODEL), BF16), pltpu.VMEM((tm + 2 * H, N), F32)],
        compiler_params=_params("arbitrary"),
        name="conv_proj",
    )(xb, xb, xb, w, b, cw, cb)


def _gates_kernel(x_ref, w_ref, wt_ref, b_ref, bt_ref, gc_ref, gr_ref):
    L = MLSTM_CHUNK
    H = MLSTM_HEADS
    x = x_ref[...]
    g = _dot(x, w_ref[...]) + b_ref[...]
    gt = _dot_nt(wt_ref[...], x) + bt_ref[...]
    ls_c = _log_sigmoid(g)
    ls_r = _log_sigmoid(gt)
    row = lax.broadcasted_iota(jnp.int32, (L, L), 0)
    col = lax.broadcasted_iota(jnp.int32, (L, L), 1)
    tril = (col <= row).astype(BF16)
    triu = (col >= row).astype(BF16)
    c_idx = lax.broadcasted_iota(jnp.int32, (L, N_GATES), 1)
    r_idx = lax.broadcasted_iota(jnp.int32, (N_GATES, L), 0)
    for k in range(x.shape[0] // L):
        sl = slice(k * L, (k + 1) * L)
        lc = ls_c[sl, :]
        pre = sum(_dot(tril, p) for p in _split3(lc))
        suf = pre[L - 1:L, :] - pre + lc
        cum = jnp.where(c_idx >= 2 * H, suf, pre)
        gc_ref[sl, :] = jnp.where(c_idx % (2 * H) >= H, cum, g[sl, :])
        lr = ls_r[:, sl]
        pre = sum(_dot(p, triu) for p in _split3(lr))
        suf = pre[:, L - 1:L] - pre + lr
        cum = jnp.where(r_idx >= 2 * H, suf, pre)
        gr_ref[:, sl] = jnp.where(r_idx % (2 * H) >= H, cum, gt[:, sl])


def _gates(xb, w, wt, b, bt):
    T = xb.shape[0]
    tm = TOKEN_TILE
    return pl.pallas_call(
        _gates_kernel,
        out_shape=(jax.ShapeDtypeStruct((T, N_GATES), F32), jax.ShapeDtypeStruct((N_GATES, T), F32)),
        grid=(T // tm,),
        in_specs=[pl.BlockSpec((tm, D_MODEL), lambda i: (i, 0)),
                  _resident((D_MODEL, N_GATES)), _resident((N_GATES, D_MODEL)),
                  _resident((1, N_GATES)), _resident((N_GATES, 1))],
        out_specs=(pl.BlockSpec((tm, N_GATES), lambda i: (i, 0)),
                   pl.BlockSpec((N_GATES, tm), lambda i: (0, i))),
        compiler_params=_params("arbitrary"),
        name="mlstm_gates",
    )(xb, w, wt, b, bt)


def _mlstm_kernel(*refs, reverse):
    if reverse:
        qt_ref, k_ref, vt_ref, gc_ref, gr_ref, out_ref, ct_ref, rt_ref, n_ref, m_ref = refs
    else:
        qt_ref, k_ref, vt_ref, gc_ref, gr_ref, hbt_ref, ot_ref, gain_ref, out_ref, \
            ct_ref, rt_ref, n_ref, m_ref = refs
    L = k_ref.shape[0]
    H, DK, DV = MLSTM_HEADS, MLSTM_DK, MLSTM_DV
    base = 2 * H if reverse else 0

    @pl.when(pl.program_id(1) == 0)
    def _():
        ct_ref[...] = jnp.zeros(ct_ref.shape, F32)
        n_ref[...] = jnp.zeros(n_ref.shape, F32)
        m_ref[...] = jnp.zeros(m_ref.shape, F32)

    key = lax.broadcasted_iota(jnp.int32, (L, L), 0)
    qry = lax.broadcasted_iota(jnp.int32, (L, L), 1)
    mask = (key >= qry) if reverse else (key <= qry)
    gc = gc_ref[:, base:base + 2 * H] * LOG2E
    gr = gr_ref[base:base + 2 * H, :] * LOG2E
    last = 0 if reverse else L - 1

    g_rows = [gr[H + h:H + h + 1, :] for h in range(H)]
    b_rows = [gr[h:h + 1, :] - gr[H + h:H + h + 1, :] for h in range(H)]
    b_cols = [gc[:, h:h + 1] - gc[:, H + h:H + h + 1] for h in range(H)]
    ms = [m_ref[h:h + 1, 0:1] for h in range(H)]
    qts = [qt_ref[h * DK:(h + 1) * DK, :] for h in range(H)]
    ks = [k_ref[:, h * DK:(h + 1) * DK] for h in range(H)]
    vts = [vt_ref[h * DV:(h + 1) * DV, :] for h in range(H)]

    sts = [_dot(ks[h], qts[h]) for h in range(H)]

    for h in range(H):
        sl = slice(h * DV, (h + 1) * DV)
        log_d = jnp.where(mask, g_rows[h] + b_cols[h], -jnp.inf)
        log_inter = g_rows[h] + ms[h]
        m_t = jnp.maximum(log_inter, jnp.max(log_d, axis=0, keepdims=True))
        s = sts[h] * jnp.exp2(log_d - m_t)
        w_inter = jnp.exp2(log_inter - m_t)
        lhs = jnp.concatenate([s.astype(BF16), (qts[h].astype(F32) * w_inter).astype(BF16)], axis=0)
        rt_ref[h, :DV, :L] = vts[h]
        rt_ref[h, DV:, :L] = jnp.ones((LANES, L), BF16)
        rt_ref[h, :DV, L:] = ct_ref[h].astype(BF16)
        rt_ref[h, DV:, L:] = jnp.broadcast_to(n_ref[h][0:1, :].astype(BF16), (LANES, DK))
        res = _dot(rt_ref[h], lhs)
        den = res[DV:DV + 1, :]
        hh = res[:DV, :] * (1.0 / jnp.maximum(jnp.abs(den), jnp.exp2(-m_t)))
        if reverse:
            out_ref[sl, :] = hh.astype(out_ref.dtype)
        else:
            hh = hh + hbt_ref[sl, :].astype(F32)
            mu = jnp.mean(hh, axis=0, keepdims=True)
            d = hh - mu
            var = jnp.mean(d * d, axis=0, keepdims=True)
            hn = d * lax.rsqrt(var + LN_EPS) * gain_ref[sl, :]
            out_ref[sl, :] = (hn + hn * jnp.tanh(0.5 * ot_ref[sl, :].astype(F32))).astype(out_ref.dtype)

    for h in range(H):
        g_tot = g_rows[h][:, last:last + 1]
        m_new = jnp.maximum(g_tot + ms[h], jnp.max(g_tot + b_rows[h], axis=-1, keepdims=True))
        wk = ks[h].astype(F32) * jnp.exp2(g_tot + b_cols[h] - m_new)
        decay = jnp.exp2(g_tot + ms[h] - m_new)
        ct_new = decay * ct_ref[h] + _dot(vts[h], wk.astype(BF16))
        n_new = decay * n_ref[h] + jnp.sum(wk, axis=0, keepdims=True)
        ct_ref[h] = ct_new
        n_ref[h] = n_new
        m_ref[h:h + 1, :] = jnp.broadcast_to(m_new, (1, m_ref.shape[1]))


def _mlstm_dir(qt, k, vot, gc, gr, hbt, gain_half, batch, seq_len, reverse):
    T = k.shape[0]
    L = MLSTM_CHUNK
    nc = seq_len // L
    H = MLSTM_HEADS
    pos = (lambda b, c: b * nc + nc - 1 - c) if reverse else (lambda b, c: b * nc + c)
    cols = lambda height, blk: pl.BlockSpec((height, L), lambda b, c: (blk, pos(b, c)))
    in_specs = [cols(MLSTM_QK, 0),
                pl.BlockSpec((L, MLSTM_QK), lambda b, c: (pos(b, c), 0)),
                cols(MLSTM_V, 0),
                pl.BlockSpec((L, N_GATES), lambda b, c: (pos(b, c), 0)),
                cols(N_GATES, 0)]
    args = [qt, k, vot, gc, gr]
    if not reverse:
        in_specs += [cols(MLSTM_V, 0), cols(MLSTM_V, 1), _resident((MLSTM_V, L))]
        args += [hbt, vot, gain_half]
    return pl.pallas_call(
        functools.partial(_mlstm_kernel, reverse=reverse),
        out_shape=jax.ShapeDtypeStruct((MLSTM_V, T), BF16),
        grid=(batch, nc),
        in_specs=in_specs,
        out_specs=cols(MLSTM_V, 0),
        scratch_shapes=[pltpu.VMEM((H, MLSTM_DV, MLSTM_DK), F32),
                        pltpu.VMEM((H, MLSTM_RW, L + MLSTM_DK), BF16),
                        pltpu.VMEM((H, SUBLANES, MLSTM_DK), F32),
                        pltpu.VMEM((SUBLANES, LANES), F32)],
        compiler_params=_params("arbitrary", "arbitrary"),
        name="mlstm_bwd" if reverse else "mlstm_fwd",
    )(*args)


def _mlstm(qt, k, vot, gc, gr, gain_half, batch, seq_len):
    hbt = _mlstm_dir(qt, k, vot, gc, gr, None, None, batch, seq_len, True)
    return _mlstm_dir(qt, k, vot, gc, gr, hbt, gain_half, batch, seq_len, False)


def _rglru_kernel(*refs, nc, reverse):
    if reverse:
        x_ref, xp_ref, xn_ref, cw_ref, cb_ref, wa_ref, ba_ref, wx_ref, bx_ref, lam_ref, o_ref, \
            a_ref, u_ref, state_ref, k_ref = refs
    else:
        x_ref, xp_ref, xn_ref, y_ref, hb_ref, cw_ref, cb_ref, wa_ref, ba_ref, wx_ref, bx_ref, lam_ref, \
            o_ref, a_ref, u_ref, state_ref, k_ref, gy_ref = refs
    tc = x_ref.shape[0]
    c = pl.program_id(1)
    cc = nc - 1 - c if reverse else c

    @pl.when(c == 0)
    def _():
        state_ref[...] = jnp.zeros(state_ref.shape, F32)
        k_ref[...] = (-0.5 * LRU_C * LOG2E) * _softplus(-lam_ref[...])

    prev = jnp.where(cc > 0, xp_ref[...], 0.0)
    nxt = jnp.where(cc < nc - 1, xn_ref[...], 0.0)
    xe = jnp.concatenate([prev, x_ref[...], nxt], axis=0)
    xc = cb_ref[...]
    for j in range(CONV_W):
        xc = xc + xe[j:j + tc] * cw_ref[j:j + 1, :]

    x2 = xc.reshape(tc * LRU_GROUP, D_RNN)
    xb = x2.astype(BF16)
    for n in range(LRU_BLOCKS):
        sl = slice(n * LRU_BW, (n + 1) * LRU_BW)
        t_a = jnp.tanh(_dot(xb[:, sl], wa_ref[n]) + ba_ref[:, sl])
        t_i = jnp.tanh(_dot(xb[:, sl], wx_ref[n]) + bx_ref[:, sl])
        a = jnp.exp2(k_ref[:, sl] + k_ref[:, sl] * t_a)
        y = 1.0 - a * a
        root = jnp.where(y > 0.0, y * lax.rsqrt(y), 0.0)
        hx = 0.5 * x2[:, sl]
        u = root * (hx + hx * t_i)
        a_ref[:, :, sl] = a.reshape(tc, LRU_GROUP, LRU_BW)
        u_ref[:, :, sl] = u.reshape(tc, LRU_GROUP, LRU_BW)

    if reverse:
        def step(k, h):
            t = tc - 1 - k
            h = a_ref[t] * h + u_ref[t]
            o_ref[t] = h
            return h
    else:
        gy_ref[...] = jax.nn.gelu(y_ref[...])

        def step(t, h):
            h = a_ref[t] * h + u_ref[t]
            o_ref[t] = (h + hb_ref[t]) * gy_ref[t]
            return h

    state_ref[...] = lax.fori_loop(0, tc, step, state_ref[...], unroll=LRU_UNROLL)


def _rglru_dir(xy, hb, cw, cb, wa, ba, wx, bx, lam, batch, seq_len, reverse):
    tc, G, C = LRU_TCHUNK, LRU_GROUP, D_RNN
    nc = seq_len // tc
    chunk = (lambda c: nc - 1 - c) if reverse else (lambda c: c)
    blk = lambda col: pl.BlockSpec((tc, G, C), lambda g, c: (chunk(c), g, col))
    in_specs = [blk(0),
                pl.BlockSpec((1, G, C), lambda g, c: (jnp.maximum(chunk(c) * tc - 1, 0), g, 0)),
                pl.BlockSpec((2, G, C),
                             lambda g, c: (jnp.minimum((chunk(c) + 1) * (tc // 2), seq_len // 2 - 1), g, 0))]
    args = [xy, xy, xy]
    scratch = [pltpu.VMEM((tc, G, C), F32), pltpu.VMEM((tc, G, C), F32),
               pltpu.VMEM((G, C), F32), pltpu.VMEM((1, C), F32)]
    if not reverse:
        in_specs += [blk(1), blk(0)]
        args += [xy, hb]
        scratch += [pltpu.VMEM((tc, G, C), F32)]
    in_specs += [_resident((CONV_W, C)), _resident((1, C)),
                 _resident((LRU_BLOCKS, LRU_BW, LRU_BW)), _resident((1, C)),
                 _resident((LRU_BLOCKS, LRU_BW, LRU_BW)), _resident((1, C)), _resident((1, C))]
    args += [cw, cb, wa, ba, wx, bx, lam]
    return pl.pallas_call(
        functools.partial(_rglru_kernel, nc=nc, reverse=reverse),
        out_shape=jax.ShapeDtypeStruct((seq_len, batch, C), F32),
        grid=(batch // G, nc),
        in_specs=in_specs,
        out_specs=blk(0),
        scratch_shapes=scratch,
        compiler_params=_params("arbitrary", "arbitrary"),
        name="rglru_bwd" if reverse else "rglru_fwd",
    )(*args)


def _rglru(xy, w, batch, seq_len):
    common = (w["cw_r"], w["cb_r"])
    hb = _rglru_dir(xy, None, *common, w["lru_wa"][1], w["lru_ba"][1:2], w["lru_wx"][1], w["lru_bx"][1:2],
                    w["lru_lam"][1:2], batch, seq_len, True)
    return _rglru_dir(xy, hb, *common, w["lru_wa"][0], w["lru_ba"][0:1], w["lru_wx"][0], w["lru_bx"][0:1],
                      w["lru_lam"][0:1], batch, seq_len, False)


def _kv_kernel(m_ref, g_ref, b_ref, w_ref, o_ref):
    m = _layer_norm(m_ref[...], g_ref[...], b_ref[...])
    o_ref[...] = _dot(m.astype(BF16), w_ref[...]).astype(o_ref.dtype)


def _mem_kv(mem, g, b, w_kv):
    R = mem.shape[0]
    return pl.pallas_call(
        _kv_kernel,
        out_shape=jax.ShapeDtypeStruct((R, 2 * D_MODEL), BF16),
        grid=(R // N_MEM,),
        in_specs=[pl.BlockSpec((N_MEM, D_MODEL), lambda i: (i, 0)), _resident((1, D_MODEL)),
                  _resident((1, D_MODEL)), _resident((D_MODEL, 2 * D_MODEL))],
        out_specs=pl.BlockSpec((N_MEM, 2 * D_MODEL), lambda i: (i, 0)),
        compiler_params=_params("arbitrary"),
        name="mem_kv",
    )(mem, g, b, w_kv)


def _merge_xattn_kernel(x_ref, hmt_ref, mg_ref, hr_ref, kv_ref,
                        wpm_ref, wpr_ref, wout_ref, wq_ref, wo_ref, lng_ref, lnb_ref, y_ref):
    half = x_ref.shape[0] // 2
    rows = [slice(0, half), slice(half, 2 * half)]
    pm = [_dot_tn(hmt_ref[:, r], wpm_ref[...]) for r in rows]
    pr = [_dot(hr_ref[r, :].astype(BF16), wpr_ref[...]) for r in rows]
    merged = []
    for r, a, b in zip(rows, pm, pr):
        g_m = _sigmoid(mg_ref[r, :D_MODEL].astype(F32))
        g_r = _sigmoid(mg_ref[r, D_MODEL:].astype(F32))
        merged.append((g_m * a + g_r * b).astype(BF16))
    mix = [_dot(m, wout_ref[...]) for m in merged]
    xs = [_layer_norm(DN_ALPHA * x_ref[r, :] + m, lng_ref[0:1, :], lnb_ref[0:1, :]) for r, m in zip(rows, mix)]

    qs = [_dot(x.astype(BF16), wq_ref[...]).astype(BF16) for x in xs]
    heads = [[], []]
    for h in range(XATTN_HEADS):
        sl = slice(h * XATTN_HD, (h + 1) * XATTN_HD)
        ss = [_dot_nt(q[:, sl], kv_ref[:, sl]) * (XATTN_HD ** -0.5) for q in qs]
        for i, s in enumerate(ss):
            e = jnp.exp(s - jnp.max(s, axis=-1, keepdims=True))
            p = e * (1.0 / jnp.sum(e, axis=-1, keepdims=True))
            oh = _dot(p.astype(BF16), kv_ref[:, D_MODEL + h * XATTN_HD:D_MODEL + (h + 1) * XATTN_HD])
            heads[i].append(oh.astype(BF16))
    for r, x, ohs in zip(rows, xs, heads):
        xa = _dot(jnp.concatenate(ohs, axis=1), wo_ref[...])
        y_ref[r, :] = _layer_norm(DN_ALPHA * x + xa, lng_ref[1:2, :], lnb_ref[1:2, :])


def _merge_xattn(x, hmt, mg, hr, kv, wpm, wpr, wout, wq, wo, lng, lnb, seq_len):
    T = x.shape[0]
    tm = MERGE_TILE
    tiles_per_seq = seq_len // tm
    row = lambda i: (i, 0)
    return pl.pallas_call(
        _merge_xattn_kernel,
        out_shape=jax.ShapeDtypeStruct((T, D_MODEL), F32),
        grid=(T // tm,),
        in_specs=[pl.BlockSpec((tm, D_MODEL), row),
                  pl.BlockSpec((MLSTM_V, tm), lambda i: (0, i)),
                  pl.BlockSpec((tm, 2 * D_MODEL), row),
                  pl.BlockSpec((tm, D_RNN), lambda i: (i % tiles_per_seq, i // tiles_per_seq)),
                  pl.BlockSpec((N_MEM, 2 * D_MODEL), lambda i: (i // tiles_per_seq, 0)),
                  _resident((MLSTM_V, D_MODEL)), _resident((D_RNN, D_MODEL)),
                  _resident((D_MODEL, D_MODEL)), _resident((D_MODEL, D_MODEL)),
                  _resident((D_MODEL, D_MODEL)), _resident((2, D_MODEL)), _resident((2, D_MODEL))],
        out_specs=pl.BlockSpec((tm, D_MODEL), row),
        compiler_params=_params("arbitrary"),
        name="merge_xattn",
    )(x, hmt, mg, hr, kv, wpm, wpr, wout, wq, wo, lng, lnb)


def _prep_layer(p, l):
    w_in, b_in = p["w_in"][l], p["b_in"][l]
    cols = lambda lo, hi: w_in[:, lo:hi]
    bias = lambda lo, hi: b_in[lo:hi]
    w_g = cols(OFF_GATE, OFF_XR).astype(BF16)
    b_g = bias(OFF_GATE, OFF_XR)
    return dict(
        ff1_in=p["ff1_in"][l].astype(BF16), ff1_out=p["ff1_out"][l].astype(BF16),
        ff2_in=p["ff2_in"][l].astype(BF16), ff2_out=p["ff2_out"][l].astype(BF16),
        ln_g=p["ln_g"][l], ln_b=p["ln_b"][l],
        w_q=cols(0, MLSTM_QK).astype(BF16), b_q=bias(0, MLSTM_QK)[None],
        cw_q=p["w_conv_qk"][l][:, :MLSTM_QK], cb_q=p["b_conv_qk"][l][None, :MLSTM_QK],
        w_k=cols(MLSTM_QK, OFF_V).astype(BF16), b_k=bias(MLSTM_QK, OFF_V)[None],
        cw_k=p["w_conv_qk"][l][:, MLSTM_QK:], cb_k=p["b_conv_qk"][l][None, MLSTM_QK:],
        w_vot=cols(OFF_V, OFF_GATE).astype(BF16).T, b_vot=bias(OFF_V, OFF_GATE)[:, None],
        w_mg=cols(OFF_MG, D_IN).astype(BF16), b_mg=bias(OFF_MG, D_IN)[None],
        w_xy=cols(OFF_XR, OFF_MG).astype(BF16), b_xy=bias(OFF_XR, OFF_MG)[None],
        w_g=w_g, w_gt=w_g.T, b_g=b_g[None], b_gt=b_g[:, None],
        cw_r=p["w_conv_r"][l], cb_r=p["b_conv_r"][l][None],
        lru_wa=(0.5 * p["lru_wa"][l]).astype(BF16), lru_ba=0.5 * p["lru_ba"][l],
        lru_wx=(0.5 * p["lru_wx"][l]).astype(BF16), lru_bx=0.5 * p["lru_bx"][l], lru_lam=p["lru_lam"][l],
        mh_gain_half=jnp.broadcast_to(0.5 * p["mh_gain"][l][:, None], (MLSTM_V, MLSTM_CHUNK)),
        w_pm=p["w_pm"][l].astype(BF16), w_pr=p["w_pr"][l].astype(BF16), w_out=p["w_out"][l].astype(BF16),
        xa_wq=p["xa_wq"][l].astype(BF16), xa_wkv=p["xa_wkv"][l].astype(BF16), xa_wo=p["xa_wo"][l].astype(BF16),
        mem_ln_g=p["mem_ln_g"][l][None], mem_ln_b=p["mem_ln_b"][l][None],
    )


def _trunk(x, mem, layers):
    B, S, _ = x.shape
    T = B * S
    x = x.reshape(T, D_MODEL)
    mem = mem.reshape(B * N_MEM, D_MODEL)
    for w in layers:
        x, xb = _ffn_ln(x, w["ff1_in"], w["ff1_out"], w["ln_g"][0:1], w["ln_b"][0:1])
        qt = _conv_proj(xb, w["w_q"], w["b_q"], w["cw_q"], w["cb_q"], S, MLSTM_DK ** -0.5, True)
        k = _conv_proj(xb, w["w_k"], w["b_k"], w["cw_k"], w["cb_k"], S, 1.0, False)
        vot = _proj_t(xb, w["w_vot"], w["b_vot"], BF16, 1024)
        mg = _proj(xb, w["w_mg"], w["b_mg"], BF16, 2048)
        xy = _proj(xb, w["w_xy"], w["b_xy"], F32, 1024, seq_len=S)
        gc, gr = _gates(xb, w["w_g"], w["w_gt"], w["b_g"], w["b_gt"])
        hmt = _mlstm(qt, k, vot, gc, gr, w["mh_gain_half"], B, S)
        hr = _rglru(xy.reshape(S, B, 2 * D_RNN), w, B, S)
        kv = _mem_kv(mem, w["mem_ln_g"], w["mem_ln_b"], w["xa_wkv"])
        x = _merge_xattn(x, hmt, mg, hr.reshape(S, B * D_RNN), kv, w["w_pm"], w["w_pr"],
                         w["w_out"], w["xa_wq"], w["xa_wo"], w["ln_g"][1:3], w["ln_b"][1:3], S)
        x, _ = _ffn_ln(x, w["ff2_in"], w["ff2_out"], w["ln_g"][3:4], w["ln_b"][3:4])
    return x.reshape(B, S, D_MODEL)


def kernel(x_prompt, x_sample, mem_prompt, mem_sample, w_in, b_in, w_conv_qk, b_conv_qk, w_conv_r, b_conv_r,
           mh_gain, lru_wa, lru_ba, lru_wx, lru_bx, lru_lam, w_pm, w_pr, w_out, xa_wq, xa_wkv, xa_wo,
           mem_ln_g, mem_ln_b, ff1_in, ff1_out, ff2_in, ff2_out, ln_g, ln_b):
    p = dict(w_in=w_in, b_in=b_in, w_conv_qk=w_conv_qk, b_conv_qk=b_conv_qk, w_conv_r=w_conv_r,
             b_conv_r=b_conv_r, mh_gain=mh_gain, lru_wa=lru_wa, lru_ba=lru_ba, lru_wx=lru_wx,
             lru_bx=lru_bx, lru_lam=lru_lam, w_pm=w_pm, w_pr=w_pr, w_out=w_out, xa_wq=xa_wq,
             xa_wkv=xa_wkv, xa_wo=xa_wo, mem_ln_g=mem_ln_g, mem_ln_b=mem_ln_b, ff1_in=ff1_in,
             ff1_out=ff1_out, ff2_in=ff2_in, ff2_out=ff2_out, ln_g=ln_g, ln_b=ln_b)
    layers = [_prep_layer(p, l) for l in range(DEPTH)]
    return (_trunk(x_prompt, mem_prompt, layers), _trunk(x_sample, mem_sample, layers))
```

```python
import functools
import math

import jax
import jax.numpy as jnp
from jax import lax
from jax.experimental import pallas as pl
from jax.experimental.pallas import tpu as pltpu

F32 = jnp.float32
BF16 = jnp.bfloat16

D_MODEL = 1024
DEPTH = 2
N_MEM = 256
MLSTM_HEADS = 4
MLSTM_DK = D_MODEL // 4
MLSTM_DV = D_MODEL // 2
MLSTM_QK = MLSTM_HEADS * MLSTM_DK
MLSTM_V = MLSTM_HEADS * MLSTM_DV
D_RNN = D_MODEL
LRU_BLOCKS = 8
LRU_BW = D_RNN // LRU_BLOCKS
LRU_C = 8.0
CONV_W = 4
XATTN_HEADS = 4
XATTN_HD = D_MODEL // XATTN_HEADS
D_FF = 2816
DN_ALPHA = (2.0 * DEPTH) ** 0.25
LN_EPS = 1e-5

OFF_V = 2 * MLSTM_QK
OFF_O = OFF_V + MLSTM_V
OFF_GATE = OFF_O + MLSTM_V
OFF_XR = OFF_GATE + 4 * MLSTM_HEADS
OFF_YR = OFF_XR + D_RNN
OFF_MG = OFF_YR + D_RNN
D_IN = OFF_MG + 2 * D_MODEL
N_GATES = 4 * MLSTM_HEADS

VMEM_LIMIT_BYTES = 56 * 1024 * 1024
LANES = 128
SUBLANES = 8
BF16_ROWS = 16

TOKEN_TILE = 512
PROJ_TILE = 1024
MERGE_TILE = 512
MXU_DEPTH = 256
FF_CHUNKS = ((0, 6 * MXU_DEPTH), (6 * MXU_DEPTH, D_FF))
MLSTM_CHUNK = 256
LRU_TCHUNK = 64
LRU_GROUP = SUBLANES
LRU_UNROLL = 8
CONV_HALO = BF16_ROWS
QKVO_ROWS = 1024
MLSTM_RW = MLSTM_DV + LANES
LOG2E = math.log2(math.e)


def _dot(a, b):
    return jnp.dot(a, b, preferred_element_type=F32)


def _dot_nt(a, b):
    return lax.dot_general(a, b, (((1,), (1,)), ((), ())), preferred_element_type=F32)


def _dot_tn(a, b):
    return lax.dot_general(a, b, (((0,), (0,)), ((), ())), preferred_element_type=F32)


def _layer_norm(y, g, b):
    mu = jnp.mean(y, axis=-1, keepdims=True)
    d = y - mu
    var = jnp.mean(d * d, axis=-1, keepdims=True)
    return d * lax.rsqrt(var + LN_EPS) * g + b


def _sigmoid(x):
    return 0.5 + 0.5 * jnp.tanh(0.5 * x)


def _silu(x):
    h = 0.5 * x
    return h + h * jnp.tanh(h)


def _log_sigmoid(x):
    return jnp.minimum(x, 0.0) - jnp.log1p(jnp.exp(-jnp.abs(x)))


def _softplus(x):
    return jnp.maximum(x, 0.0) + jnp.log1p(jnp.exp(-jnp.abs(x)))


def _split3(x):
    hi = x.astype(BF16)
    r = x - hi.astype(F32)
    mid = r.astype(BF16)
    lo = (r - mid.astype(F32)).astype(BF16)
    return hi, mid, lo


def _resident(shape):
    zeros = (0,) * len(shape)
    return pl.BlockSpec(shape, lambda *_: zeros, pipeline_mode=pl.Buffered(1))


def _params(*sem):
    return pltpu.CompilerParams(dimension_semantics=sem, vmem_limit_bytes=VMEM_LIMIT_BYTES)


def _ffn_ln_kernel(x_ref, win_ref, wout_ref, g_ref, b_ref, y_ref, yb_ref):
    x = x_ref[...]
    xb = x.astype(BF16)
    gates = [_dot(xb, win_ref[:, lo:hi]) for lo, hi in FF_CHUNKS]
    ups = [_dot(xb, win_ref[:, D_FF + lo:D_FF + hi]) for lo, hi in FF_CHUNKS]
    acc = jnp.zeros(x.shape, F32)
    for (lo, hi), gate, up in zip(FF_CHUNKS, gates, ups):
        h = (_silu(gate) * up).astype(BF16)
        acc = acc + _dot(h, wout_ref[lo:hi, :])
    y = _layer_norm(DN_ALPHA * x + 0.5 * acc, g_ref[...], b_ref[...])
    y_ref[...] = y
    yb_ref[...] = y.astype(BF16)


def _ffn_ln(x, w_in, w_out, g, b):
    T = x.shape[0]
    tm = TOKEN_TILE
    row = lambda i: (i, 0)
    return pl.pallas_call(
        _ffn_ln_kernel,
        out_shape=(jax.ShapeDtypeStruct((T, D_MODEL), F32), jax.ShapeDtypeStruct((T, D_MODEL), BF16)),
        grid=(T // tm,),
        in_specs=[pl.BlockSpec((tm, D_MODEL), row), _resident((D_MODEL, 2 * D_FF)),
                  _resident((D_FF, D_MODEL)), _resident((1, D_MODEL)), _resident((1, D_MODEL))],
        out_specs=(pl.BlockSpec((tm, D_MODEL), row), pl.BlockSpec((tm, D_MODEL), row)),
        compiler_params=_params("arbitrary"),
        name="ffn_ln",
    )(x, w_in, w_out, g, b)


def _proj_kernel(x_ref, w_ref, b_ref, o_ref):
    o_ref[...] = (_dot(x_ref[...], w_ref[...]) + b_ref[...]).astype(o_ref.dtype)


def _proj(xb, w, b, out_dtype, tn, seq_len=None):
    T, N = xb.shape[0], w.shape[1]
    tm = PROJ_TILE
    if seq_len is None:
        out_shape = (T, N)
        out_map = lambda n, i: (i, n)
    else:
        tiles_per_seq = seq_len // tm
        out_shape = (seq_len, (T // seq_len) * N)
        out_map = lambda n, i: (i % tiles_per_seq, (i // tiles_per_seq) * (N // tn) + n)
    return pl.pallas_call(
        _proj_kernel,
        out_shape=jax.ShapeDtypeStruct(out_shape, out_dtype),
        grid=(N // tn, T // tm),
        in_specs=[pl.BlockSpec((tm, D_MODEL), lambda n, i: (i, 0)),
                  pl.BlockSpec((D_MODEL, tn), lambda n, i: (0, n)),
                  pl.BlockSpec((1, tn), lambda n, i: (0, n))],
        out_specs=pl.BlockSpec((tm, tn), out_map),
        compiler_params=_params("arbitrary", "arbitrary"),
        name="proj",
    )(xb, w, b)


def _qkvo_kernel(x_ref, xp_ref, xn_ref, wq_ref, bq_ref, cwq_ref, cbq_ref, wk_ref, bk_ref, cwk_ref, cbk_ref,
                 wvot_ref, bvot_ref, qt_ref, k_ref, vot_ref, xe_ref, uq_ref, uk_ref, xt_ref, *, tiles_per_seq):
    tm = x_ref.shape[0]
    H = CONV_HALO
    xt_ref[...] = x_ref[...].astype(F32).T.astype(BF16)
    xe_ref[0:H, :] = xp_ref[...]
    xe_ref[H:H + tm, :] = x_ref[...]
    xe_ref[H + tm:, :] = xn_ref[...]
    pos = pl.program_id(0) % tiles_per_seq
    first = pos == 0
    last = pos == tiles_per_seq - 1

    def project(w_ref, b_ref, u_ref):
        u = _dot(xe_ref[...], w_ref[...]) + b_ref[...]
        u_ref[0:H, :] = jnp.where(first, 0.0, u[0:H, :])
        u_ref[H:H + tm, :] = u[H:H + tm, :]
        u_ref[H + tm:, :] = jnp.where(last, 0.0, u[H + tm:, :])

    def conv_silu(u_ref, cw_ref, cb_ref):
        out = cb_ref[...]
        for j in range(CONV_W):
            out = out + u_ref[pl.ds(H - 1 + j, tm), :] * cw_ref[j:j + 1, :]
        return _silu(out)

    project(wq_ref, bq_ref, uq_ref)
    project(wk_ref, bk_ref, uk_ref)
    qt_ref[...] = (conv_silu(uq_ref, cwq_ref, cbq_ref) * (MLSTM_DK ** -0.5)).T.astype(qt_ref.dtype)
    k_ref[...] = conv_silu(uk_ref, cwk_ref, cbk_ref).astype(k_ref.dtype)
    for i in range(vot_ref.shape[0] // QKVO_ROWS):
        rows = slice(i * QKVO_ROWS, (i + 1) * QKVO_ROWS)
        vot_ref[rows, :] = (_dot(wvot_ref[rows, :], xt_ref[...]) + bvot_ref[rows, :]).astype(vot_ref.dtype)


def _qkvo(xb, w, seq_len):
    T = xb.shape[0]
    tm, H = TOKEN_TILE, CONV_HALO
    tiles_per_seq = seq_len // tm
    hpt = tm // H
    last_halo = T // H - 1
    conv_w = lambda: [_resident((D_MODEL, MLSTM_QK)), _resident((1, MLSTM_QK)),
                      _resident((CONV_W, MLSTM_QK)), _resident((1, MLSTM_QK))]
    return pl.pallas_call(
        functools.partial(_qkvo_kernel, tiles_per_seq=tiles_per_seq),
        out_shape=(jax.ShapeDtypeStruct((MLSTM_QK, T), BF16), jax.ShapeDtypeStruct((T, MLSTM_QK), BF16),
                   jax.ShapeDtypeStruct((2 * MLSTM_V, T), BF16)),
        grid=(T // tm,),
        in_specs=[pl.BlockSpec((tm, D_MODEL), lambda i: (i, 0)),
                  pl.BlockSpec((H, D_MODEL), lambda i: (jnp.maximum(i * hpt - 1, 0), 0)),
                  pl.BlockSpec((H, D_MODEL), lambda i: (jnp.minimum((i + 1) * hpt, last_halo), 0))]
        + conv_w() + conv_w() + [_resident((2 * MLSTM_V, D_MODEL)), _resident((2 * MLSTM_V, 1))],
        out_specs=(pl.BlockSpec((MLSTM_QK, tm), lambda i: (0, i)), pl.BlockSpec((tm, MLSTM_QK), lambda i: (i, 0)),
                   pl.BlockSpec((2 * MLSTM_V, tm), lambda i: (0, i))),
        scratch_shapes=[pltpu.VMEM((tm + 2 * H, D_MODEL), BF16), pltpu.VMEM((tm + 2 * H, MLSTM_QK), F32),
                        pltpu.VMEM((tm + 2 * H, MLSTM_QK), F32), pltpu.VMEM((D_MODEL, tm), BF16)],
        compiler_params=_params("arbitrary"),
        name="qkvo",
    )(xb, xb, xb, w["w_q"], w["b_q"], w["cw_q"], w["cb_q"], w["w_k"], w["b_k"], w["cw_k"], w["cb_k"],
      w["w_vot"], w["b_vot"])


def _gates_kernel(x_ref, w_ref, wt_ref, b_ref, bt_ref, gc_ref, gr_ref):
    L = MLSTM_CHUNK
    H = MLSTM_HEADS
    x = x_ref[...]
    g = _dot(x, w_ref[...]) + b_ref[...]
    gt = _dot_nt(wt_ref[...], x) + bt_ref[...]
    ls_c = _log_sigmoid(g)
    ls_r = _log_sigmoid(gt)
    row = lax.broadcasted_iota(jnp.int32, (L, L), 0)
    col = lax.broadcasted_iota(jnp.int32, (L, L), 1)
    tril = (col <= row).astype(BF16)
    triu = (col >= row).astype(BF16)
    c_idx = lax.broadcasted_iota(jnp.int32, (L, N_GATES), 1)
    r_idx = lax.broadcasted_iota(jnp.int32, (N_GATES, L), 0)
    for k in range(x.shape[0] // L):
        sl = slice(k * L, (k + 1) * L)
        lc = ls_c[sl, :]
        pre = sum(_dot(tril, p) for p in _split3(lc))
        suf = pre[L - 1:L, :] - pre + lc
        cum = jnp.where(c_idx >= 2 * H, suf, pre)
        gc_ref[sl, :] = jnp.where(c_idx % (2 * H) >= H, cum, g[sl, :])
        lr = ls_r[:, sl]
        pre = sum(_dot(p, triu) for p in _split3(lr))
        suf = pre[:, L - 1:L] - pre + lr
        cum = jnp.where(r_idx >= 2 * H, suf, pre)
        gr_ref[:, sl] = jnp.where(r_idx % (2 * H) >= H, cum, gt[:, sl])


def _gates(xb, w, wt, b, bt):
    T = xb.shape[0]
    tm = TOKEN_TILE
    return pl.pallas_call(
        _gates_kernel,
        out_shape=(jax.ShapeDtypeStruct((T, N_GATES), F32), jax.ShapeDtypeStruct((N_GATES, T), F32)),
        grid=(T // tm,),
        in_specs=[pl.BlockSpec((tm, D_MODEL), lambda i: (i, 0)),
                  _resident((D_MODEL, N_GATES)), _resident((N_GATES, D_MODEL)),
                  _resident((1, N_GATES)), _resident((N_GATES, 1))],
        out_specs=(pl.BlockSpec((tm, N_GATES), lambda i: (i, 0)),
                   pl.BlockSpec((N_GATES, tm), lambda i: (0, i))),
        compiler_params=_params("arbitrary"),
        name="mlstm_gates",
    )(xb, w, wt, b, bt)


def _mlstm_kernel(*refs, reverse):
    if reverse:
        qt_ref, k_ref, vt_ref, gc_ref, gr_ref, out_ref, ct_ref, rt_ref, n_ref, m_ref = refs
    else:
        qt_ref, k_ref, vt_ref, gc_ref, gr_ref, hbt_ref, ot_ref, gain_ref, out_ref, \
            ct_ref, rt_ref, n_ref, m_ref = refs
    L = k_ref.shape[0]
    H, DK, DV = MLSTM_HEADS, MLSTM_DK, MLSTM_DV
    base = 2 * H if reverse else 0

    @pl.when(pl.program_id(1) == 0)
    def _():
        ct_ref[...] = jnp.zeros(ct_ref.shape, F32)
        n_ref[...] = jnp.zeros(n_ref.shape, F32)
        m_ref[...] = jnp.zeros(m_ref.shape, F32)

    key = lax.broadcasted_iota(jnp.int32, (L, L), 0)
    qry = lax.broadcasted_iota(jnp.int32, (L, L), 1)
    mask = (key >= qry) if reverse else (key <= qry)
    gc = gc_ref[:, base:base + 2 * H] * LOG2E
    gr = gr_ref[base:base + 2 * H, :] * LOG2E
    last = 0 if reverse else L - 1

    g_rows = [gr[H + h:H + h + 1, :] for h in range(H)]
    b_rows = [gr[h:h + 1, :] - gr[H + h:H + h + 1, :] for h in range(H)]
    b_cols = [gc[:, h:h + 1] - gc[:, H + h:H + h + 1] for h in range(H)]
    ms = [m_ref[h:h + 1, 0:1] for h in range(H)]
    qts = [qt_ref[h * DK:(h + 1) * DK, :] for h in range(H)]
    ks = [k_ref[:, h * DK:(h + 1) * DK] for h in range(H)]
    vts = [vt_ref[h * DV:(h + 1) * DV, :] for h in range(H)]

    sts = [_dot(ks[h], qts[h]) for h in range(H)]

    for h in range(H):
        sl = slice(h * DV, (h + 1) * DV)
        log_d = jnp.where(mask, g_rows[h] + b_cols[h], -jnp.inf)
        log_inter = g_rows[h] + ms[h]
        m_t = jnp.maximum(log_inter, jnp.max(log_d, axis=0, keepdims=True))
        s = sts[h] * jnp.exp2(log_d - m_t)
        w_inter = jnp.exp2(log_inter - m_t)
        lhs = jnp.concatenate([s.astype(BF16), (qts[h].astype(F32) * w_inter).astype(BF16)], axis=0)
        rt_ref[h, :DV, :L] = vts[h]
        rt_ref[h, DV:, :L] = jnp.ones((LANES, L), BF16)
        rt_ref[h, :DV, L:] = ct_ref[h].astype(BF16)
        rt_ref[h, DV:, L:] = jnp.broadcast_to(n_ref[h][0:1, :].astype(BF16), (LANES, DK))
        res = _dot(rt_ref[h], lhs)
        den = res[DV:DV + 1, :]
        hh = res[:DV, :] * (1.0 / jnp.maximum(jnp.abs(den), jnp.exp2(-m_t)))
        if reverse:
            out_ref[sl, :] = hh.astype(out_ref.dtype)
        else:
            hh = hh + hbt_ref[sl, :].astype(F32)
            mu = jnp.mean(hh, axis=0, keepdims=True)
            d = hh - mu
            var = jnp.mean(d * d, axis=0, keepdims=True)
            hn = d * lax.rsqrt(var + LN_EPS) * gain_ref[sl, :]
            out_ref[sl, :] = (hn + hn * jnp.tanh(0.5 * ot_ref[sl, :].astype(F32))).astype(out_ref.dtype)

    for h in range(H):
        g_tot = g_rows[h][:, last:last + 1]
        m_new = jnp.maximum(g_tot + ms[h], jnp.max(g_tot + b_rows[h], axis=-1, keepdims=True))
        wk = ks[h].astype(F32) * jnp.exp2(g_tot + b_cols[h] - m_new)
        decay = jnp.exp2(g_tot + ms[h] - m_new)
        ct_new = decay * ct_ref[h] + _dot(vts[h], wk.astype(BF16))
        n_new = decay * n_ref[h] + jnp.sum(wk, axis=0, keepdims=True)
        ct_ref[h] = ct_new
        n_ref[h] = n_new
        m_ref[h:h + 1, :] = jnp.broadcast_to(m_new, (1, m_ref.shape[1]))


def _mlstm_dir(qt, k, vot, gc, gr, hbt, gain_half, batch, seq_len, reverse):
    T = k.shape[0]
    L = MLSTM_CHUNK
    nc = seq_len // L
    H = MLSTM_HEADS
    pos = (lambda b, c: b * nc + nc - 1 - c) if reverse else (lambda b, c: b * nc + c)
    cols = lambda height, blk: pl.BlockSpec((height, L), lambda b, c: (blk, pos(b, c)))
    in_specs = [cols(MLSTM_QK, 0),
                pl.BlockSpec((L, MLSTM_QK), lambda b, c: (pos(b, c), 0)),
                cols(MLSTM_V, 0),
                pl.BlockSpec((L, N_GATES), lambda b, c: (pos(b, c), 0)),
                cols(N_GATES, 0)]
    args = [qt, k, vot, gc, gr]
    if not reverse:
        in_specs += [cols(MLSTM_V, 0), cols(MLSTM_V, 1), _resident((MLSTM_V, L))]
        args += [hbt, vot, gain_half]
    return pl.pallas_call(
        functools.partial(_mlstm_kernel, reverse=reverse),
        out_shape=jax.ShapeDtypeStruct((MLSTM_V, T), BF16),
        grid=(batch, nc),
        in_specs=in_specs,
        out_specs=cols(MLSTM_V, 0),
        scratch_shapes=[pltpu.VMEM((H, MLSTM_DV, MLSTM_DK), F32),
                        pltpu.VMEM((H, MLSTM_RW, L + MLSTM_DK), BF16),
                        pltpu.VMEM((H, SUBLANES, MLSTM_DK), F32),
                        pltpu.VMEM((SUBLANES, LANES), F32)],
        compiler_params=_params("arbitrary", "arbitrary"),
        name="mlstm_bwd" if reverse else "mlstm_fwd",
    )(*args)


def _mlstm(qt, k, vot, gc, gr, gain_half, batch, seq_len):
    hbt = _mlstm_dir(qt, k, vot, gc, gr, None, None, batch, seq_len, True)
    return _mlstm_dir(qt, k, vot, gc, gr, hbt, gain_half, batch, seq_len, False)


def _rglru_kernel(*refs, nc, reverse):
    if reverse:
        x_ref, xp_ref, xn_ref, cw_ref, cb_ref, wa_ref, ba_ref, wx_ref, bx_ref, lam_ref, o_ref, \
            a_ref, u_ref, state_ref, k_ref = refs
    else:
        x_ref, xp_ref, xn_ref, y_ref, hb_ref, cw_ref, cb_ref, wa_ref, ba_ref, wx_ref, bx_ref, lam_ref, \
            o_ref, a_ref, u_ref, state_ref, k_ref, gy_ref = refs
    tc = x_ref.shape[0]
    c = pl.program_id(1)
    cc = nc - 1 - c if reverse else c

    @pl.when(c == 0)
    def _():
        state_ref[...] = jnp.zeros(state_ref.shape, F32)
        k_ref[...] = (-0.5 * LRU_C * LOG2E) * _softplus(-lam_ref[...])

    prev = jnp.where(cc > 0, xp_ref[...], 0.0)
    nxt = jnp.where(cc < nc - 1, xn_ref[...], 0.0)
    xe = jnp.concatenate([prev, x_ref[...], nxt], axis=0)
    xc = cb_ref[...]
    for j in range(CONV_W):
        xc = xc + xe[j:j + tc] * cw_ref[j:j + 1, :]

    x2 = xc.reshape(tc * LRU_GROUP, D_RNN)
    xb = x2.astype(BF16)
    for n in range(LRU_BLOCKS):
        sl = slice(n * LRU_BW, (n + 1) * LRU_BW)
        t_a = jnp.tanh(_dot(xb[:, sl], wa_ref[n]) + ba_ref[:, sl])
        t_i = jnp.tanh(_dot(xb[:, sl], wx_ref[n]) + bx_ref[:, sl])
        a = jnp.exp2(k_ref[:, sl] + k_ref[:, sl] * t_a)
        y = 1.0 - a * a
        root = jnp.where(y > 0.0, y * lax.rsqrt(y), 0.0)
        hx = 0.5 * x2[:, sl]
        u = root * (hx + hx * t_i)
        a_ref[:, :, sl] = a.reshape(tc, LRU_GROUP, LRU_BW)
        u_ref[:, :, sl] = u.reshape(tc, LRU_GROUP, LRU_BW)

    if reverse:
        def step(k, h):
            t = tc - 1 - k
            h = a_ref[t] * h + u_ref[t]
            o_ref[t] = h
            return h
    else:
        gy_ref[...] = jax.nn.gelu(y_ref[...])

        def step(t, h):
            h = a_ref[t] * h + u_ref[t]
            o_ref[t] = (h + hb_ref[t]) * gy_ref[t]
            return h

    state_ref[...] = lax.fori_loop(0, tc, step, state_ref[...], unroll=LRU_UNROLL)


def _rglru_dir(xy, hb, cw, cb, wa, ba, wx, bx, lam, batch, seq_len, reverse):
    tc, G, C = LRU_TCHUNK, LRU_GROUP, D_RNN
    nc = seq_len // tc
    chunk = (lambda c: nc - 1 - c) if reverse else (lambda c: c)
    blk = lambda col: pl.BlockSpec((tc, G, C), lambda g, c: (chunk(c), g, col))
    in_specs = [blk(0),
                pl.BlockSpec((1, G, C), lambda g, c: (jnp.maximum(chunk(c) * tc - 1, 0), g, 0)),
                pl.BlockSpec((2, G, C),
                             lambda g, c: (jnp.minimum((chunk(c) + 1) * (tc // 2), seq_len // 2 - 1), g, 0))]
    args = [xy, xy, xy]
    scratch = [pltpu.VMEM((tc, G, C), F32), pltpu.VMEM((tc, G, C), F32),
               pltpu.VMEM((G, C), F32), pltpu.VMEM((1, C), F32)]
    if not reverse:
        in_specs += [blk(1), blk(0)]
        args += [xy, hb]
        scratch += [pltpu.VMEM((tc, G, C), F32)]
    in_specs += [_resident((CONV_W, C)), _resident((1, C)),
                 _resident((LRU_BLOCKS, LRU_BW, LRU_BW)), _resident((1, C)),
                 _resident((LRU_BLOCKS, LRU_BW, LRU_BW)), _resident((1, C)), _resident((1, C))]
    args += [cw, cb, wa, ba, wx, bx, lam]
    return pl.pallas_call(
        functools.partial(_rglru_kernel, nc=nc, reverse=reverse),
        out_shape=jax.ShapeDtypeStruct((seq_len, batch, C), F32),
        grid=(batch // G, nc),
        in_specs=in_specs,
        out_specs=blk(0),
        scratch_shapes=scratch,
        compiler_params=_params("arbitrary", "arbitrary"),
        name="rglru_bwd" if reverse else "rglru_fwd",
    )(*args)


def _rglru(xy, w, batch, seq_len):
    common = (w["cw_r"], w["cb_r"])
    hb = _rglru_dir(xy, None, *common, w["lru_wa"][1], w["lru_ba"][1:2], w["lru_wx"][1], w["lru_bx"][1:2],
                    w["lru_lam"][1:2], batch, seq_len, True)
    return _rglru_dir(xy, hb, *common, w["lru_wa"][0], w["lru_ba"][0:1], w["lru_wx"][0], w["lru_bx"][0:1],
                      w["lru_lam"][0:1], batch, seq_len, False)


def _kv_kernel(m_ref, g_ref, b_ref, w_ref, o_ref):
    m = _layer_norm(m_ref[...], g_ref[...], b_ref[...])
    o_ref[...] = _dot(m.astype(BF16), w_ref[...]).astype(o_ref.dtype)


def _mem_kv(mem, g, b, w_kv):
    R = mem.shape[0]
    return pl.pallas_call(
        _kv_kernel,
        out_shape=jax.ShapeDtypeStruct((R, 2 * D_MODEL), BF16),
        grid=(R // N_MEM,),
        in_specs=[pl.BlockSpec((N_MEM, D_MODEL), lambda i: (i, 0)), _resident((1, D_MODEL)),
                  _resident((1, D_MODEL)), _resident((D_MODEL, 2 * D_MODEL))],
        out_specs=pl.BlockSpec((N_MEM, 2 * D_MODEL), lambda i: (i, 0)),
        compiler_params=_params("arbitrary"),
        name="mem_kv",
    )(mem, g, b, w_kv)


def _merge_xattn_kernel(x_ref, hmt_ref, mg_ref, hr_ref, kv_ref,
                        wpm_ref, wpr_ref, wout_ref, wq_ref, wo_ref, lng_ref, lnb_ref, y_ref):
    half = x_ref.shape[0] // 2
    rows = [slice(0, half), slice(half, 2 * half)]
    pm = [_dot_tn(hmt_ref[:, r], wpm_ref[...]) for r in rows]
    pr = [_dot(hr_ref[r, :].astype(BF16), wpr_ref[...]) for r in rows]
    merged = []
    for r, a, b in zip(rows, pm, pr):
        g_m = _sigmoid(mg_ref[r, :D_MODEL].astype(F32))
        g_r = _sigmoid(mg_ref[r, D_MODEL:].astype(F32))
        merged.append((g_m * a + g_r * b).astype(BF16))
    mix = [_dot(m, wout_ref[...]) for m in merged]
    xs = [_layer_norm(DN_ALPHA * x_ref[r, :] + m, lng_ref[0:1, :], lnb_ref[0:1, :]) for r, m in zip(rows, mix)]

    qs = [_dot(x.astype(BF16), wq_ref[...]).astype(BF16) for x in xs]
    heads = [[], []]
    for h in range(XATTN_HEADS):
        sl = slice(h * XATTN_HD, (h + 1) * XATTN_HD)
        ss = [_dot_nt(q[:, sl], kv_ref[:, sl]) * (XATTN_HD ** -0.5) for q in qs]
        for i, s in enumerate(ss):
            e = jnp.exp(s - jnp.max(s, axis=-1, keepdims=True))
            p = e * (1.0 / jnp.sum(e, axis=-1, keepdims=True))
            oh = _dot(p.astype(BF16), kv_ref[:, D_MODEL + h * XATTN_HD:D_MODEL + (h + 1) * XATTN_HD])
            heads[i].append(oh.astype(BF16))
    for r, x, ohs in zip(rows, xs, heads):
        xa = _dot(jnp.concatenate(ohs, axis=1), wo_ref[...])
        y_ref[r, :] = _layer_norm(DN_ALPHA * x + xa, lng_ref[1:2, :], lnb_ref[1:2, :])


def _merge_xattn(x, hmt, mg, hr, kv, wpm, wpr, wout, wq, wo, lng, lnb, seq_len):
    T = x.shape[0]
    tm = MERGE_TILE
    tiles_per_seq = seq_len // tm
    row = lambda i: (i, 0)
    return pl.pallas_call(
        _merge_xattn_kernel,
        out_shape=jax.ShapeDtypeStruct((T, D_MODEL), F32),
        grid=(T // tm,),
        in_specs=[pl.BlockSpec((tm, D_MODEL), row),
                  pl.BlockSpec((MLSTM_V, tm), lambda i: (0, i)),
                  pl.BlockSpec((tm, 2 * D_MODEL), row),
                  pl.BlockSpec((tm, D_RNN), lambda i: (i % tiles_per_seq, i // tiles_per_seq)),
                  pl.BlockSpec((N_MEM, 2 * D_MODEL), lambda i: (i // tiles_per_seq, 0)),
                  _resident((MLSTM_V, D_MODEL)), _resident((D_RNN, D_MODEL)),
                  _resident((D_MODEL, D_MODEL)), _resident((D_MODEL, D_MODEL)),
                  _resident((D_MODEL, D_MODEL)), _resident((2, D_MODEL)), _resident((2, D_MODEL))],
        out_specs=pl.BlockSpec((tm, D_MODEL), row),
        compiler_params=_params("arbitrary"),
        name="merge_xattn",
    )(x, hmt, mg, hr, kv, wpm, wpr, wout, wq, wo, lng, lnb)


def _prep_layer(p, l):
    w_in, b_in = p["w_in"][l], p["b_in"][l]
    cols = lambda lo, hi: w_in[:, lo:hi]
    bias = lambda lo, hi: b_in[lo:hi]
    w_g = cols(OFF_GATE, OFF_XR).astype(BF16)
    b_g = bias(OFF_GATE, OFF_XR)
    return dict(
        ff1_in=p["ff1_in"][l].astype(BF16), ff1_out=p["ff1_out"][l].astype(BF16),
        ff2_in=p["ff2_in"][l].astype(BF16), ff2_out=p["ff2_out"][l].astype(BF16),
        ln_g=p["ln_g"][l], ln_b=p["ln_b"][l],
        w_q=cols(0, MLSTM_QK).astype(BF16), b_q=bias(0, MLSTM_QK)[None],
        cw_q=p["w_conv_qk"][l][:, :MLSTM_QK], cb_q=p["b_conv_qk"][l][None, :MLSTM_QK],
        w_k=cols(MLSTM_QK, OFF_V).astype(BF16), b_k=bias(MLSTM_QK, OFF_V)[None],
        cw_k=p["w_conv_qk"][l][:, MLSTM_QK:], cb_k=p["b_conv_qk"][l][None, MLSTM_QK:],
        w_vot=cols(OFF_V, OFF_GATE).astype(BF16).T, b_vot=bias(OFF_V, OFF_GATE)[:, None],
        w_mg=cols(OFF_MG, D_IN).astype(BF16), b_mg=bias(OFF_MG, D_IN)[None],
        w_xy=cols(OFF_XR, OFF_MG).astype(BF16), b_xy=bias(OFF_XR, OFF_MG)[None],
        w_g=w_g, w_gt=w_g.T, b_g=b_g[None], b_gt=b_g[:, None],
        cw_r=p["w_conv_r"][l], cb_r=p["b_conv_r"][l][None],
        lru_wa=(0.5 * p["lru_wa"][l]).astype(BF16), lru_ba=0.5 * p["lru_ba"][l],
        lru_wx=(0.5 * p["lru_wx"][l]).astype(BF16), lru_bx=0.5 * p["lru_bx"][l], lru_lam=p["lru_lam"][l],
        mh_gain_half=jnp.broadcast_to(0.5 * p["mh_gain"][l][:, None], (MLSTM_V, MLSTM_CHUNK)),
        w_pm=p["w_pm"][l].astype(BF16), w_pr=p["w_pr"][l].astype(BF16), w_out=p["w_out"][l].astype(BF16),
        xa_wq=p["xa_wq"][l].astype(BF16), xa_wkv=p["xa_wkv"][l].astype(BF16), xa_wo=p["xa_wo"][l].astype(BF16),
        mem_ln_g=p["mem_ln_g"][l][None], mem_ln_b=p["mem_ln_b"][l][None],
    )


def _trunk(x, mem, layers):
    B, S, _ = x.shape
    T = B * S
    x = x.reshape(T, D_MODEL)
    mem = mem.reshape(B * N_MEM, D_MODEL)
    for w in layers:
        x, xb = _ffn_ln(x, w["ff1_in"], w["ff1_out"], w["ln_g"][0:1], w["ln_b"][0:1])
        qt, k, vot = _qkvo(xb, w, S)
        mg = _proj(xb, w["w_mg"], w["b_mg"], BF16, 2048)
        xy = _proj(xb, w["w_xy"], w["b_xy"], F32, 1024, seq_len=S)
        gc, gr = _gates(xb, w["w_g"], w["w_gt"], w["b_g"], w["b_gt"])
        hmt = _mlstm(qt, k, vot, gc, gr, w["mh_gain_half"], B, S)
        hr = _rglru(xy.reshape(S, B, 2 * D_RNN), w, B, S)
        kv = _mem_kv(mem, w["mem_ln_g"], w["mem_ln_b"], w["xa_wkv"])
        x = _merge_xattn(x, hmt, mg, hr.reshape(S, B * D_RNN), kv, w["w_pm"], w["w_pr"],
                         w["w_out"], w["xa_wq"], w["xa_wo"], w["ln_g"][1:3], w["ln_b"][1:3], S)
        x, _ = _ffn_ln(x, w["ff2_in"], w["ff2_out"], w["ln_g"][3:4], w["ln_b"][3:4])
    return x.reshape(B, S, D_MODEL)


def kernel(x_prompt, x_sample, mem_prompt, mem_sample, w_in, b_in, w_conv_qk, b_conv_qk, w_conv_r, b_conv_r,
           mh_gain, lru_wa, lru_ba, lru_wx, lru_bx, lru_lam, w_pm, w_pr, w_out, xa_wq, xa_wkv, xa_wo,
           mem_ln_g, mem_ln_b, ff1_in, ff1_out, ff2_in, ff2_out, ln_g, ln_b):
    p = dict(w_in=w_in, b_in=b_in, w_conv_qk=w_conv_qk, b_conv_qk=b_conv_qk, w_conv_r=w_conv_r,
             b_conv_r=b_conv_r, mh_gain=mh_gain, lru_wa=lru_wa, lru_ba=lru_ba, lru_wx=lru_wx,
             lru_bx=lru_bx, lru_lam=lru_lam, w_pm=w_pm, w_pr=w_pr, w_out=w_out, xa_wq=xa_wq,
             xa_wkv=xa_wkv, xa_wo=xa_wo, mem_ln_g=mem_ln_g, mem_ln_b=mem_ln_b, ff1_in=ff1_in,
             ff1_out=ff1_out, ff2_in=ff2_in, ff2_out=ff2_out, ln_g=ln_g, ln_b=ln_b)
    layers = [_prep_layer(p, l) for l in range(DEPTH)]
    return (_trunk(x_prompt, mem_prompt, layers), _trunk(x_sample, mem_sample, layers))
```

```python
import functools
import math

import jax
import jax.numpy as jnp
from jax import lax
from jax.experimental import pallas as pl
from jax.experimental.pallas import tpu as pltpu

F32 = jnp.float32
BF16 = jnp.bfloat16

D_MODEL = 1024
DEPTH = 2
N_MEM = 256
MLSTM_HEADS = 4
MLSTM_DK = D_MODEL // 4
MLSTM_DV = D_MODEL // 2
MLSTM_QK = MLSTM_HEADS * MLSTM_DK
MLSTM_V = MLSTM_HEADS * MLSTM_DV
D_RNN = D_MODEL
LRU_BLOCKS = 8
LRU_BW = D_RNN // LRU_BLOCKS
LRU_C = 8.0
CONV_W = 4
XATTN_HEADS = 4
XATTN_HD = D_MODEL // XATTN_HEADS
D_FF = 2816
DN_ALPHA = (2.0 * DEPTH) ** 0.25
LN_EPS = 1e-5

OFF_V = 2 * MLSTM_QK
OFF_O = OFF_V + MLSTM_V
OFF_GATE = OFF_O + MLSTM_V
OFF_XR = OFF_GATE + 4 * MLSTM_HEADS
OFF_YR = OFF_XR + D_RNN
OFF_MG = OFF_YR + D_RNN
D_IN = OFF_MG + 2 * D_MODEL
N_GATES = 4 * MLSTM_HEADS

VMEM_LIMIT_BYTES = 56 * 1024 * 1024
LANES = 128
SUBLANES = 8
BF16_ROWS = 16

TOKEN_TILE = 512
MERGE_TILE = 512
MXU_DEPTH = 256
FF_CHUNKS = ((0, 6 * MXU_DEPTH), (6 * MXU_DEPTH, D_FF))
MLSTM_CHUNK = 256
LRU_TCHUNK = 128
LRU_GROUP = SUBLANES
LRU_UNROLL = 8
CONV_HALO = BF16_ROWS
QKVO_ROWS = 1024
MLSTM_RW = MLSTM_DV + LANES
LOG2E = math.log2(math.e)


def _dot(a, b):
    return jnp.dot(a, b, preferred_element_type=F32)


def _dot_nt(a, b):
    return lax.dot_general(a, b, (((1,), (1,)), ((), ())), preferred_element_type=F32)


def _dot_tn(a, b):
    return lax.dot_general(a, b, (((0,), (0,)), ((), ())), preferred_element_type=F32)


def _layer_norm(y, g, b):
    mu = jnp.mean(y, axis=-1, keepdims=True)
    d = y - mu
    var = jnp.mean(d * d, axis=-1, keepdims=True)
    return d * lax.rsqrt(var + LN_EPS) * g + b


def _sigmoid(x):
    return 0.5 + 0.5 * jnp.tanh(0.5 * x)


def _silu(x):
    h = 0.5 * x
    return h + h * jnp.tanh(h)


def _log_sigmoid(x):
    return jnp.minimum(x, 0.0) - jnp.log1p(jnp.exp(-jnp.abs(x)))


def _softplus(x):
    return jnp.maximum(x, 0.0) + jnp.log1p(jnp.exp(-jnp.abs(x)))


def _split3(x):
    hi = x.astype(BF16)
    r = x - hi.astype(F32)
    mid = r.astype(BF16)
    lo = (r - mid.astype(F32)).astype(BF16)
    return hi, mid, lo


def _resident(shape):
    zeros = (0,) * len(shape)
    return pl.BlockSpec(shape, lambda *_: zeros, pipeline_mode=pl.Buffered(1))


def _params(*sem):
    return pltpu.CompilerParams(dimension_semantics=sem, vmem_limit_bytes=VMEM_LIMIT_BYTES)


def _ffn_ln_kernel(x_ref, win_ref, wout_ref, g_ref, b_ref, y_ref, yb_ref):
    x = x_ref[...]
    xb = x.astype(BF16)
    gates = [_dot(xb, win_ref[:, lo:hi]) for lo, hi in FF_CHUNKS]
    ups = [_dot(xb, win_ref[:, D_FF + lo:D_FF + hi]) for lo, hi in FF_CHUNKS]
    acc = jnp.zeros(x.shape, F32)
    for (lo, hi), gate, up in zip(FF_CHUNKS, gates, ups):
        h = (_silu(gate) * up).astype(BF16)
        acc = acc + _dot(h, wout_ref[lo:hi, :])
    y = _layer_norm(DN_ALPHA * x + 0.5 * acc, g_ref[...], b_ref[...])
    y_ref[...] = y
    yb_ref[...] = y.astype(BF16)


def _ffn_ln(x, w_in, w_out, g, b):
    T = x.shape[0]
    tm = TOKEN_TILE
    row = lambda i: (i, 0)
    return pl.pallas_call(
        _ffn_ln_kernel,
        out_shape=(jax.ShapeDtypeStruct((T, D_MODEL), F32), jax.ShapeDtypeStruct((T, D_MODEL), BF16)),
        grid=(T // tm,),
        in_specs=[pl.BlockSpec((tm, D_MODEL), row), _resident((D_MODEL, 2 * D_FF)),
                  _resident((D_FF, D_MODEL)), _resident((1, D_MODEL)), _resident((1, D_MODEL))],
        out_specs=(pl.BlockSpec((tm, D_MODEL), row), pl.BlockSpec((tm, D_MODEL), row)),
        compiler_params=_params("arbitrary"),
        name="ffn_ln",
    )(x, w_in, w_out, g, b)


def _qkvo_kernel(x_ref, xp_ref, xn_ref, wq_ref, bq_ref, cwq_ref, cbq_ref, wk_ref, bk_ref, cwk_ref, cbk_ref,
                 wvot_ref, bvot_ref, qt_ref, k_ref, vot_ref, xe_ref, uq_ref, uk_ref, xt_ref, *, tiles_per_seq):
    tm = x_ref.shape[0]
    H = CONV_HALO
    xt_ref[...] = x_ref[...].astype(F32).T.astype(BF16)
    xe_ref[0:H, :] = xp_ref[...]
    xe_ref[H:H + tm, :] = x_ref[...]
    xe_ref[H + tm:, :] = xn_ref[...]
    pos = pl.program_id(0) % tiles_per_seq
    first = pos == 0
    last = pos == tiles_per_seq - 1

    def project(w_ref, b_ref, u_ref):
        u = _dot(xe_ref[...], w_ref[...]) + b_ref[...]
        u_ref[0:H, :] = jnp.where(first, 0.0, u[0:H, :])
        u_ref[H:H + tm, :] = u[H:H + tm, :]
        u_ref[H + tm:, :] = jnp.where(last, 0.0, u[H + tm:, :])

    def conv_silu(u_ref, cw_ref, cb_ref):
        out = cb_ref[...]
        for j in range(CONV_W):
            out = out + u_ref[pl.ds(H - 1 + j, tm), :] * cw_ref[j:j + 1, :]
        return _silu(out)

    project(wq_ref, bq_ref, uq_ref)
    project(wk_ref, bk_ref, uk_ref)
    qt_ref[...] = (conv_silu(uq_ref, cwq_ref, cbq_ref) * (MLSTM_DK ** -0.5)).T.astype(qt_ref.dtype)
    k_ref[...] = conv_silu(uk_ref, cwk_ref, cbk_ref).astype(k_ref.dtype)
    for i in range(vot_ref.shape[0] // QKVO_ROWS):
        rows = slice(i * QKVO_ROWS, (i + 1) * QKVO_ROWS)
        vot_ref[rows, :] = (_dot(wvot_ref[rows, :], xt_ref[...]) + bvot_ref[rows, :]).astype(vot_ref.dtype)


def _qkvo(xb, w, seq_len):
    T = xb.shape[0]
    tm, H = TOKEN_TILE, CONV_HALO
    tiles_per_seq = seq_len // tm
    hpt = tm // H
    last_halo = T // H - 1
    conv_w = lambda: [_resident((D_MODEL, MLSTM_QK)), _resident((1, MLSTM_QK)),
                      _resident((CONV_W, MLSTM_QK)), _resident((1, MLSTM_QK))]
    return pl.pallas_call(
        functools.partial(_qkvo_kernel, tiles_per_seq=tiles_per_seq),
        out_shape=(jax.ShapeDtypeStruct((MLSTM_QK, T), BF16), jax.ShapeDtypeStruct((T, MLSTM_QK), BF16),
                   jax.ShapeDtypeStruct((2 * MLSTM_V, T), BF16)),
        grid=(T // tm,),
        in_specs=[pl.BlockSpec((tm, D_MODEL), lambda i: (i, 0)),
                  pl.BlockSpec((H, D_MODEL), lambda i: (jnp.maximum(i * hpt - 1, 0), 0)),
                  pl.BlockSpec((H, D_MODEL), lambda i: (jnp.minimum((i + 1) * hpt, last_halo), 0))]
        + conv_w() + conv_w() + [_resident((2 * MLSTM_V, D_MODEL)), _resident((2 * MLSTM_V, 1))],
        out_specs=(pl.BlockSpec((MLSTM_QK, tm), lambda i: (0, i)), pl.BlockSpec((tm, MLSTM_QK), lambda i: (i, 0)),
                   pl.BlockSpec((2 * MLSTM_V, tm), lambda i: (0, i))),
        scratch_shapes=[pltpu.VMEM((tm + 2 * H, D_MODEL), BF16), pltpu.VMEM((tm + 2 * H, MLSTM_QK), F32),
                        pltpu.VMEM((tm + 2 * H, MLSTM_QK), F32), pltpu.VMEM((D_MODEL, tm), BF16)],
        compiler_params=_params("arbitrary"),
        name="qkvo",
    )(xb, xb, xb, w["w_q"], w["b_q"], w["cw_q"], w["cb_q"], w["w_k"], w["b_k"], w["cw_k"], w["cb_k"],
      w["w_vot"], w["b_vot"])


def _mgxy_gates_kernel(x_ref, wmg_ref, bmg_ref, wxy_ref, bxy_ref, w_ref, wt_ref, b_ref, bt_ref,
                       mg_ref, xy_ref, gc_ref, gr_ref):
    L = MLSTM_CHUNK
    H = MLSTM_HEADS
    x = x_ref[...]
    mg_ref[...] = (_dot(x, wmg_ref[...]) + bmg_ref[...]).astype(mg_ref.dtype)
    xy_ref[...] = _dot(x, wxy_ref[...]) + bxy_ref[...]
    g = _dot(x, w_ref[...]) + b_ref[...]
    gt = _dot_nt(wt_ref[...], x) + bt_ref[...]
    ls_c = _log_sigmoid(g)
    ls_r = _log_sigmoid(gt)
    row = lax.broadcasted_iota(jnp.int32, (L, L), 0)
    col = lax.broadcasted_iota(jnp.int32, (L, L), 1)
    tril = (col <= row).astype(BF16)
    triu = (col >= row).astype(BF16)
    c_idx = lax.broadcasted_iota(jnp.int32, (L, N_GATES), 1)
    r_idx = lax.broadcasted_iota(jnp.int32, (N_GATES, L), 0)
    for k in range(x.shape[0] // L):
        sl = slice(k * L, (k + 1) * L)
        lc = ls_c[sl, :]
        pre = sum(_dot(tril, p) for p in _split3(lc))
        suf = pre[L - 1:L, :] - pre + lc
        cum = jnp.where(c_idx >= 2 * H, suf, pre)
        gc_ref[sl, :] = jnp.where(c_idx % (2 * H) >= H, cum, g[sl, :])
        lr = ls_r[:, sl]
        pre = sum(_dot(p, triu) for p in _split3(lr))
        suf = pre[:, L - 1:L] - pre + lr
        cum = jnp.where(r_idx >= 2 * H, suf, pre)
        gr_ref[:, sl] = jnp.where(r_idx % (2 * H) >= H, cum, gt[:, sl])


def _mgxy_gates(xb, w, seq_len):
    T = xb.shape[0]
    tm = TOKEN_TILE
    tiles_per_seq = seq_len // tm
    row = lambda i: (i, 0)
    return pl.pallas_call(
        _mgxy_gates_kernel,
        out_shape=(jax.ShapeDtypeStruct((T, 2 * D_MODEL), BF16),
                   jax.ShapeDtypeStruct((seq_len, (T // seq_len) * 2 * D_RNN), F32),
                   jax.ShapeDtypeStruct((T, N_GATES), F32), jax.ShapeDtypeStruct((N_GATES, T), F32)),
        grid=(T // tm,),
        in_specs=[pl.BlockSpec((tm, D_MODEL), row),
                  _resident((D_MODEL, 2 * D_MODEL)), _resident((1, 2 * D_MODEL)),
                  _resident((D_MODEL, 2 * D_RNN)), _resident((1, 2 * D_RNN)),
                  _resident((D_MODEL, N_GATES)), _resident((N_GATES, D_MODEL)),
                  _resident((1, N_GATES)), _resident((N_GATES, 1))],
        out_specs=(pl.BlockSpec((tm, 2 * D_MODEL), row),
                   pl.BlockSpec((tm, 2 * D_RNN), lambda i: (i % tiles_per_seq, i // tiles_per_seq)),
                   pl.BlockSpec((tm, N_GATES), row), pl.BlockSpec((N_GATES, tm), lambda i: (0, i))),
        compiler_params=_params("arbitrary"),
        name="mgxy_gates",
    )(xb, w["w_mg"], w["b_mg"], w["w_xy"], w["b_xy"], w["w_g"], w["w_gt"], w["b_g"], w["b_gt"])


def _mlstm_kernel(*refs, reverse):
    if reverse:
        qt_ref, k_ref, vt_ref, gc_ref, gr_ref, out_ref, ct_ref, rt_ref, n_ref, m_ref = refs
    else:
        qt_ref, k_ref, vt_ref, gc_ref, gr_ref, hbt_ref, ot_ref, gain_ref, out_ref, \
            ct_ref, rt_ref, n_ref, m_ref = refs
    L = k_ref.shape[0]
    H, DK, DV = MLSTM_HEADS, MLSTM_DK, MLSTM_DV
    base = 2 * H if reverse else 0

    @pl.when(pl.program_id(1) == 0)
    def _():
        ct_ref[...] = jnp.zeros(ct_ref.shape, F32)
        n_ref[...] = jnp.zeros(n_ref.shape, F32)
        m_ref[...] = jnp.zeros(m_ref.shape, F32)

    key = lax.broadcasted_iota(jnp.int32, (L, L), 0)
    qry = lax.broadcasted_iota(jnp.int32, (L, L), 1)
    mask = (key >= qry) if reverse else (key <= qry)
    gc = gc_ref[:, base:base + 2 * H] * LOG2E
    gr = gr_ref[base:base + 2 * H, :] * LOG2E
    last = 0 if reverse else L - 1

    g_rows = [gr[H + h:H + h + 1, :] for h in range(H)]
    b_rows = [gr[h:h + 1, :] - gr[H + h:H + h + 1, :] for h in range(H)]
    b_cols = [gc[:, h:h + 1] - gc[:, H + h:H + h + 1] for h in range(H)]
    ms = [m_ref[h:h + 1, 0:1] for h in range(H)]
    qts = [qt_ref[h * DK:(h + 1) * DK, :] for h in range(H)]
    ks = [k_ref[:, h * DK:(h + 1) * DK] for h in range(H)]
    vts = [vt_ref[h * DV:(h + 1) * DV, :] for h in range(H)]

    sts = [_dot(ks[h], qts[h]) for h in range(H)]

    for h in range(H):
        sl = slice(h * DV, (h + 1) * DV)
        log_d = jnp.where(mask, g_rows[h] + b_cols[h], -jnp.inf)
        log_inter = g_rows[h] + ms[h]
        m_t = jnp.maximum(log_inter, jnp.max(log_d, axis=0, keepdims=True))
        s = sts[h] * jnp.exp2(log_d - m_t)
        w_inter = jnp.exp2(log_inter - m_t)
        lhs = jnp.concatenate([s.astype(BF16), qts[h] * w_inter.astype(BF16)], axis=0)
        rt_ref[h, :DV, :L] = vts[h]
        rt_ref[h, DV:, :L] = jnp.ones((LANES, L), BF16)
        rt_ref[h, :DV, L:] = ct_ref[h].astype(BF16)
        rt_ref[h, DV:, L:] = jnp.broadcast_to(n_ref[h][0:1, :].astype(BF16), (LANES, DK))
        res = _dot(rt_ref[h], lhs)
        den = res[DV:DV + 1, :]
        hh = res[:DV, :] * (1.0 / jnp.maximum(jnp.abs(den), jnp.exp2(-m_t)))
        if reverse:
            out_ref[sl, :] = hh.astype(out_ref.dtype)
        else:
            hh = hh + hbt_ref[sl, :].astype(F32)
            mu = jnp.mean(hh, axis=0, keepdims=True)
            d = hh - mu
            var = jnp.mean(d * d, axis=0, keepdims=True)
            hn = d * lax.rsqrt(var + LN_EPS) * gain_ref[sl, :]
            out_ref[sl, :] = (hn + hn * jnp.tanh(0.5 * ot_ref[sl, :].astype(F32))).astype(out_ref.dtype)

    for h in range(H):
        g_tot = g_rows[h][:, last:last + 1]
        m_new = jnp.maximum(g_tot + ms[h], jnp.max(g_tot + b_rows[h], axis=-1, keepdims=True))
        wk = ks[h].astype(F32) * jnp.exp2(g_tot + b_cols[h] - m_new)
        decay = jnp.exp2(g_tot + ms[h] - m_new)
        ct_new = decay * ct_ref[h] + _dot(vts[h], wk.astype(BF16))
        n_new = decay * n_ref[h] + jnp.sum(wk, axis=0, keepdims=True)
        ct_ref[h] = ct_new
        n_ref[h] = n_new
        m_ref[h:h + 1, :] = jnp.broadcast_to(m_new, (1, m_ref.shape[1]))


def _mlstm_dir(qt, k, vot, gc, gr, hbt, gain_half, batch, seq_len, reverse):
    T = k.shape[0]
    L = MLSTM_CHUNK
    nc = seq_len // L
    H = MLSTM_HEADS
    pos = (lambda b, c: b * nc + nc - 1 - c) if reverse else (lambda b, c: b * nc + c)
    cols = lambda height, blk: pl.BlockSpec((height, L), lambda b, c: (blk, pos(b, c)))
    in_specs = [cols(MLSTM_QK, 0),
                pl.BlockSpec((L, MLSTM_QK), lambda b, c: (pos(b, c), 0)),
                cols(MLSTM_V, 0),
                pl.BlockSpec((L, N_GATES), lambda b, c: (pos(b, c), 0)),
                cols(N_GATES, 0)]
    args = [qt, k, vot, gc, gr]
    if not reverse:
        in_specs += [cols(MLSTM_V, 0), cols(MLSTM_V, 1), _resident((MLSTM_V, L))]
        args += [hbt, vot, gain_half]
    return pl.pallas_call(
        functools.partial(_mlstm_kernel, reverse=reverse),
        out_shape=jax.ShapeDtypeStruct((MLSTM_V, T), BF16),
        grid=(batch, nc),
        in_specs=in_specs,
        out_specs=cols(MLSTM_V, 0),
        scratch_shapes=[pltpu.VMEM((H, MLSTM_DV, MLSTM_DK), F32),
                        pltpu.VMEM((H, MLSTM_RW, L + MLSTM_DK), BF16),
                        pltpu.VMEM((H, SUBLANES, MLSTM_DK), F32),
                        pltpu.VMEM((SUBLANES, LANES), F32)],
        compiler_params=_params("arbitrary", "arbitrary"),
        name="mlstm_bwd" if reverse else "mlstm_fwd",
    )(*args)


def _mlstm(qt, k, vot, gc, gr, gain_half, batch, seq_len):
    hbt = _mlstm_dir(qt, k, vot, gc, gr, None, None, batch, seq_len, True)
    return _mlstm_dir(qt, k, vot, gc, gr, hbt, gain_half, batch, seq_len, False)


def _rglru_kernel(*refs, nc, reverse):
    if reverse:
        x_ref, xp_ref, xn_ref, cw_ref, cb_ref, wa_ref, ba_ref, wx_ref, bx_ref, lam_ref, o_ref, \
            a_ref, u_ref, state_ref, k_ref = refs
    else:
        x_ref, xp_ref, xn_ref, y_ref, hb_ref, cw_ref, cb_ref, wa_ref, ba_ref, wx_ref, bx_ref, lam_ref, \
            o_ref, a_ref, u_ref, state_ref, k_ref, gy_ref = refs
    tc = x_ref.shape[0]
    c = pl.program_id(1)
    cc = nc - 1 - c if reverse else c

    @pl.when(c == 0)
    def _():
        state_ref[...] = jnp.zeros(state_ref.shape, F32)
        k_ref[...] = (-0.5 * LRU_C * LOG2E) * _softplus(-lam_ref[...])

    prev = jnp.where(cc > 0, xp_ref[...], 0.0)
    nxt = jnp.where(cc < nc - 1, xn_ref[...], 0.0)
    xe = jnp.concatenate([prev, x_ref[...], nxt], axis=0)
    xc = cb_ref[...]
    for j in range(CONV_W):
        xc = xc + xe[j:j + tc] * cw_ref[j:j + 1, :]

    x2 = xc.reshape(tc * LRU_GROUP, D_RNN)
    xb = x2.astype(BF16)
    for n in range(LRU_BLOCKS):
        sl = slice(n * LRU_BW, (n + 1) * LRU_BW)
        t_a = jnp.tanh(_dot(xb[:, sl], wa_ref[n]) + ba_ref[:, sl])
        t_i = jnp.tanh(_dot(xb[:, sl], wx_ref[n]) + bx_ref[:, sl])
        a = jnp.exp2(k_ref[:, sl] + k_ref[:, sl] * t_a)
        y = 1.0 - a * a
        root = jnp.where(y > 0.0, y * lax.rsqrt(y), 0.0)
        hx = 0.5 * x2[:, sl]
        u = root * (hx + hx * t_i)
        a_ref[:, :, sl] = a.reshape(tc, LRU_GROUP, LRU_BW)
        u_ref[:, :, sl] = u.reshape(tc, LRU_GROUP, LRU_BW)

    if reverse:
        def step(k, h):
            t = tc - 1 - k
            h = a_ref[t] * h + u_ref[t]
            o_ref[t] = h
            return h
    else:
        gy_ref[...] = jax.nn.gelu(y_ref[...])

        def step(t, h):
            h = a_ref[t] * h + u_ref[t]
            o_ref[t] = (h + hb_ref[t]) * gy_ref[t]
            return h

    state_ref[...] = lax.fori_loop(0, tc, step, state_ref[...], unroll=LRU_UNROLL)


def _rglru_dir(xy, hb, cw, cb, wa, ba, wx, bx, lam, batch, seq_len, reverse):
    tc, G, C = LRU_TCHUNK, LRU_GROUP, D_RNN
    nc = seq_len // tc
    chunk = (lambda c: nc - 1 - c) if reverse else (lambda c: c)
    blk = lambda col: pl.BlockSpec((tc, G, C), lambda g, c: (chunk(c), g, col))
    in_specs = [blk(0),
                pl.BlockSpec((1, G, C), lambda g, c: (jnp.maximum(chunk(c) * tc - 1, 0), g, 0)),
                pl.BlockSpec((2, G, C),
                             lambda g, c: (jnp.minimum((chunk(c) + 1) * (tc // 2), seq_len // 2 - 1), g, 0))]
    args = [xy, xy, xy]
    scratch = [pltpu.VMEM((tc, G, C), F32), pltpu.VMEM((tc, G, C), F32),
               pltpu.VMEM((G, C), F32), pltpu.VMEM((1, C), F32)]
    if not reverse:
        in_specs += [blk(1), blk(0)]
        args += [xy, hb]
        scratch += [pltpu.VMEM((tc, G, C), F32)]
    in_specs += [_resident((CONV_W, C)), _resident((1, C)),
                 _resident((LRU_BLOCKS, LRU_BW, LRU_BW)), _resident((1, C)),
                 _resident((LRU_BLOCKS, LRU_BW, LRU_BW)), _resident((1, C)), _resident((1, C))]
    args += [cw, cb, wa, ba, wx, bx, lam]
    return pl.pallas_call(
        functools.partial(_rglru_kernel, nc=nc, reverse=reverse),
        out_shape=jax.ShapeDtypeStruct((seq_len, batch, C), F32),
        grid=(batch // G, nc),
        in_specs=in_specs,
        out_specs=blk(0),
        scratch_shapes=scratch,
        compiler_params=_params("arbitrary", "arbitrary"),
        name="rglru_bwd" if reverse else "rglru_fwd",
    )(*args)


def _rglru(xy, w, batch, seq_len):
    common = (w["cw_r"], w["cb_r"])
    hb = _rglru_dir(xy, None, *common, w["lru_wa"][1], w["lru_ba"][1:2], w["lru_wx"][1], w["lru_bx"][1:2],
                    w["lru_lam"][1:2], batch, seq_len, True)
    return _rglru_dir(xy, hb, *common, w["lru_wa"][0], w["lru_ba"][0:1], w["lru_wx"][0], w["lru_bx"][0:1],
                      w["lru_lam"][0:1], batch, seq_len, False)


def _kv_kernel(m_ref, g_ref, b_ref, w_ref, o_ref):
    m = _layer_norm(m_ref[...], g_ref[...], b_ref[...])
    o_ref[...] = _dot(m.astype(BF16), w_ref[...]).astype(o_ref.dtype)


def _mem_kv(mem, g, b, w_kv):
    R = mem.shape[0]
    return pl.pallas_call(
        _kv_kernel,
        out_shape=jax.ShapeDtypeStruct((R, 2 * D_MODEL), BF16),
        grid=(R // N_MEM,),
        in_specs=[pl.BlockSpec((N_MEM, D_MODEL), lambda i: (i, 0)), _resident((1, D_MODEL)),
                  _resident((1, D_MODEL)), _resident((D_MODEL, 2 * D_MODEL))],
        out_specs=pl.BlockSpec((N_MEM, 2 * D_MODEL), lambda i: (i, 0)),
        compiler_params=_params("arbitrary"),
        name="mem_kv",
    )(mem, g, b, w_kv)


def _merge_xattn_kernel(x_ref, hmt_ref, mg_ref, hr_ref, kv_ref,
                        wpm_ref, wpr_ref, wout_ref, wq_ref, wo_ref, lng_ref, lnb_ref, y_ref):
    half = x_ref.shape[0] // 2
    rows = [slice(0, half), slice(half, 2 * half)]
    pm = [_dot_tn(hmt_ref[:, r], wpm_ref[...]) for r in rows]
    pr = [_dot(hr_ref[r, :].astype(BF16), wpr_ref[...]) for r in rows]
    merged = []
    for r, a, b in zip(rows, pm, pr):
        g_m = _sigmoid(mg_ref[r, :D_MODEL].astype(F32))
        g_r = _sigmoid(mg_ref[r, D_MODEL:].astype(F32))
        merged.append((g_m * a + g_r * b).astype(BF16))
    mix = [_dot(m, wout_ref[...]) for m in merged]
    xs = [_layer_norm(DN_ALPHA * x_ref[r, :] + m, lng_ref[0:1, :], lnb_ref[0:1, :]) for r, m in zip(rows, mix)]

    qs = [_dot(x.astype(BF16), wq_ref[...]).astype(BF16) for x in xs]
    heads = [[], []]
    for h in range(XATTN_HEADS):
        sl = slice(h * XATTN_HD, (h + 1) * XATTN_HD)
        ss = [_dot_nt(q[:, sl], kv_ref[:, sl]) * (XATTN_HD ** -0.5) for q in qs]
        for i, s in enumerate(ss):
            e = jnp.exp(s - jnp.max(s, axis=-1, keepdims=True))
            p = e * (1.0 / jnp.sum(e, axis=-1, keepdims=True))
            oh = _dot(p.astype(BF16), kv_ref[:, D_MODEL + h * XATTN_HD:D_MODEL + (h + 1) * XATTN_HD])
            heads[i].append(oh.astype(BF16))
    for r, x, ohs in zip(rows, xs, heads):
        xa = _dot(jnp.concatenate(ohs, axis=1), wo_ref[...])
        y_ref[r, :] = _layer_norm(DN_ALPHA * x + xa, lng_ref[1:2, :], lnb_ref[1:2, :])


def _merge_xattn(x, hmt, mg, hr, kv, wpm, wpr, wout, wq, wo, lng, lnb, seq_len):
    T = x.shape[0]
    tm = MERGE_TILE
    tiles_per_seq = seq_len // tm
    row = lambda i: (i, 0)
    return pl.pallas_call(
        _merge_xattn_kernel,
        out_shape=jax.ShapeDtypeStruct((T, D_MODEL), F32),
        grid=(T // tm,),
        in_specs=[pl.BlockSpec((tm, D_MODEL), row),
                  pl.BlockSpec((MLSTM_V, tm), lambda i: (0, i)),
                  pl.BlockSpec((tm, 2 * D_MODEL), row),
                  pl.BlockSpec((tm, D_RNN), lambda i: (i % tiles_per_seq, i // tiles_per_seq)),
                  pl.BlockSpec((N_MEM, 2 * D_MODEL), lambda i: (i // tiles_per_seq, 0)),
                  _resident((MLSTM_V, D_MODEL)), _resident((D_RNN, D_MODEL)),
                  _resident((D_MODEL, D_MODEL)), _resident((D_MODEL, D_MODEL)),
                  _resident((D_MODEL, D_MODEL)), _resident((2, D_MODEL)), _resident((2, D_MODEL))],
        out_specs=pl.BlockSpec((tm, D_MODEL), row),
        compiler_params=_params("arbitrary"),
        name="merge_xattn",
    )(x, hmt, mg, hr, kv, wpm, wpr, wout, wq, wo, lng, lnb)


def _prep_layer(p, l):
    w_in, b_in = p["w_in"][l], p["b_in"][l]
    cols = lambda lo, hi: w_in[:, lo:hi]
    bias = lambda lo, hi: b_in[lo:hi]
    w_g = cols(OFF_GATE, OFF_XR).astype(BF16)
    b_g = bias(OFF_GATE, OFF_XR)
    return dict(
        ff1_in=p["ff1_in"][l].astype(BF16), ff1_out=p["ff1_out"][l].astype(BF16),
        ff2_in=p["ff2_in"][l].astype(BF16), ff2_out=p["ff2_out"][l].astype(BF16),
        ln_g=p["ln_g"][l], ln_b=p["ln_b"][l],
        w_q=cols(0, MLSTM_QK).astype(BF16), b_q=bias(0, MLSTM_QK)[None],
        cw_q=p["w_conv_qk"][l][:, :MLSTM_QK], cb_q=p["b_conv_qk"][l][None, :MLSTM_QK],
        w_k=cols(MLSTM_QK, OFF_V).astype(BF16), b_k=bias(MLSTM_QK, OFF_V)[None],
        cw_k=p["w_conv_qk"][l][:, MLSTM_QK:], cb_k=p["b_conv_qk"][l][None, MLSTM_QK:],
        w_vot=cols(OFF_V, OFF_GATE).astype(BF16).T, b_vot=bias(OFF_V, OFF_GATE)[:, None],
        w_mg=cols(OFF_MG, D_IN).astype(BF16), b_mg=bias(OFF_MG, D_IN)[None],
        w_xy=cols(OFF_XR, OFF_MG).astype(BF16), b_xy=bias(OFF_XR, OFF_MG)[None],
        w_g=w_g, w_gt=w_g.T, b_g=b_g[None], b_gt=b_g[:, None],
        cw_r=p["w_conv_r"][l], cb_r=p["b_conv_r"][l][None],
        lru_wa=(0.5 * p["lru_wa"][l]).astype(BF16), lru_ba=0.5 * p["lru_ba"][l],
        lru_wx=(0.5 * p["lru_wx"][l]).astype(BF16), lru_bx=0.5 * p["lru_bx"][l], lru_lam=p["lru_lam"][l],
        mh_gain_half=jnp.broadcast_to(0.5 * p["mh_gain"][l][:, None], (MLSTM_V, MLSTM_CHUNK)),
        w_pm=p["w_pm"][l].astype(BF16), w_pr=p["w_pr"][l].astype(BF16), w_out=p["w_out"][l].astype(BF16),
        xa_wq=p["xa_wq"][l].astype(BF16), xa_wkv=p["xa_wkv"][l].astype(BF16), xa_wo=p["xa_wo"][l].astype(BF16),
        mem_ln_g=p["mem_ln_g"][l][None], mem_ln_b=p["mem_ln_b"][l][None],
    )


def _trunk(x, mem, layers):
    B, S, _ = x.shape
    T = B * S
    x = x.reshape(T, D_MODEL)
    mem = mem.reshape(B * N_MEM, D_MODEL)
    for w in layers:
        x, xb = _ffn_ln(x, w["ff1_in"], w["ff1_out"], w["ln_g"][0:1], w["ln_b"][0:1])
        qt, k, vot = _qkvo(xb, w, S)
        mg, xy, gc, gr = _mgxy_gates(xb, w, S)
        hmt = _mlstm(qt, k, vot, gc, gr, w["mh_gain_half"], B, S)
        hr = _rglru(xy.reshape(S, B, 2 * D_RNN), w, B, S)
        kv = _mem_kv(mem, w["mem_ln_g"], w["mem_ln_b"], w["xa_wkv"])
        x = _merge_xattn(x, hmt, mg, hr.reshape(S, B * D_RNN), kv, w["w_pm"], w["w_pr"],
                         w["w_out"], w["xa_wq"], w["xa_wo"], w["ln_g"][1:3], w["ln_b"][1:3], S)
        x, _ = _ffn_ln(x, w["ff2_in"], w["ff2_out"], w["ln_g"][3:4], w["ln_b"][3:4])
    return x.reshape(B, S, D_MODEL)


def kernel(x_prompt, x_sample, mem_prompt, mem_sample, w_in, b_in, w_conv_qk, b_conv_qk, w_conv_r, b_conv_r,
           mh_gain, lru_wa, lru_ba, lru_wx, lru_bx, lru_lam, w_pm, w_pr, w_out, xa_wq, xa_wkv, xa_wo,
           mem_ln_g, mem_ln_b, ff1_in, ff1_out, ff2_in, ff2_out, ln_g, ln_b):
    p = dict(w_in=w_in, b_in=b_in, w_conv_qk=w_conv_qk, b_conv_qk=b_conv_qk, w_conv_r=w_conv_r,
             b_conv_r=b_conv_r, mh_gain=mh_gain, lru_wa=lru_wa, lru_ba=lru_ba, lru_wx=lru_wx,
             lru_bx=lru_bx, lru_lam=lru_lam, w_pm=w_pm, w_pr=w_pr, w_out=w_out, xa_wq=xa_wq,
             xa_wkv=xa_wkv, xa_wo=xa_wo, mem_ln_g=mem_ln_g, mem_ln_b=mem_ln_b, ff1_in=ff1_in,
             ff1_out=ff1_out, ff2_in=ff2_in, ff2_out=ff2_out, ln_g=ln_g, ln_b=ln_b)
    layers = [_prep_layer(p, l) for l in range(DEPTH)]
    return (_trunk(x_prompt, mem_prompt, layers), _trunk(x_sample, mem_sample, layers))
```

```python
import functools
import math

import jax
import jax.numpy as jnp
from jax import lax
from jax.experimental import pallas as pl
from jax.experimental.pallas import tpu as pltpu

F32 = jnp.float32
BF16 = jnp.bfloat16

D_MODEL = 1024
DEPTH = 2
N_MEM = 256
MLSTM_HEADS = 4
MLSTM_DK = D_MODEL // 4
MLSTM_DV = D_MODEL // 2
MLSTM_QK = MLSTM_HEADS * MLSTM_DK
MLSTM_V = MLSTM_HEADS * MLSTM_DV
D_RNN = D_MODEL
LRU_BLOCKS = 8
LRU_BW = D_RNN // LRU_BLOCKS
LRU_C = 8.0
CONV_W = 4
XATTN_HEADS = 4
XATTN_HD = D_MODEL // XATTN_HEADS
D_FF = 2816
DN_ALPHA = (2.0 * DEPTH) ** 0.25
LN_EPS = 1e-5

OFF_V = 2 * MLSTM_QK
OFF_O = OFF_V + MLSTM_V
OFF_GATE = OFF_O + MLSTM_V
OFF_XR = OFF_GATE + 4 * MLSTM_HEADS
OFF_YR = OFF_XR + D_RNN
OFF_MG = OFF_YR + D_RNN
D_IN = OFF_MG + 2 * D_MODEL
N_GATES = 4 * MLSTM_HEADS

VMEM_LIMIT_BYTES = 56 * 1024 * 1024
LANES = 128
SUBLANES = 8
BF16_ROWS = 16

TOKEN_TILE = 512
MERGE_TILE = 512
MXU_DEPTH = 256
FF_CHUNKS = ((0, 6 * MXU_DEPTH), (6 * MXU_DEPTH, D_FF))
MLSTM_CHUNK = 256
LRU_TCHUNK = 128
LRU_GROUP = SUBLANES
LRU_UNROLL = 8
CONV_HALO = BF16_ROWS
QKVO_ROWS = 1024
MLSTM_RW = MLSTM_DV + LANES
LOG2E = math.log2(math.e)


def _dot(a, b):
    return jnp.dot(a, b, preferred_element_type=F32)


def _dot_nt(a, b):
    return lax.dot_general(a, b, (((1,), (1,)), ((), ())), preferred_element_type=F32)


def _dot_tn(a, b):
    return lax.dot_general(a, b, (((0,), (0,)), ((), ())), preferred_element_type=F32)


def _layer_norm(y, g, b):
    mu = jnp.mean(y, axis=-1, keepdims=True)
    d = y - mu
    var = jnp.mean(d * d, axis=-1, keepdims=True)
    return d * lax.rsqrt(var + LN_EPS) * g + b


def _sigmoid(x):
    return 0.5 + 0.5 * jnp.tanh(0.5 * x)


def _silu(x):
    h = 0.5 * x
    return h + h * jnp.tanh(h)


def _log_sigmoid(x):
    return jnp.minimum(x, 0.0) - jnp.log1p(jnp.exp(-jnp.abs(x)))


def _softplus(x):
    return jnp.maximum(x, 0.0) + jnp.log1p(jnp.exp(-jnp.abs(x)))


def _pack_bf16_pair(hi, lo):
    hi_bits = lax.bitcast_convert_type(hi.astype(BF16).astype(F32), jnp.uint32)
    lo_bits = lax.bitcast_convert_type(lo.astype(BF16).astype(F32), jnp.uint32)
    return hi_bits | (lo_bits >> 16)


def _unpack_bf16_pair(w):
    hi = lax.bitcast_convert_type(w & jnp.uint32(0xFFFF0000), F32)
    lo = lax.bitcast_convert_type(w << 16, F32)
    return hi, lo


def _split3(x):
    hi = x.astype(BF16)
    r = x - hi.astype(F32)
    mid = r.astype(BF16)
    lo = (r - mid.astype(F32)).astype(BF16)
    return hi, mid, lo


def _resident(shape):
    zeros = (0,) * len(shape)
    return pl.BlockSpec(shape, lambda *_: zeros, pipeline_mode=pl.Buffered(1))


def _params(*sem):
    return pltpu.CompilerParams(dimension_semantics=sem, vmem_limit_bytes=VMEM_LIMIT_BYTES)


def _ffn_ln_kernel(x_ref, win_ref, wout_ref, g_ref, b_ref, y_ref, yb_ref):
    x = x_ref[...]
    xb = x.astype(BF16)
    gates = [_dot(xb, win_ref[:, lo:hi]) for lo, hi in FF_CHUNKS]
    ups = [_dot(xb, win_ref[:, D_FF + lo:D_FF + hi]) for lo, hi in FF_CHUNKS]
    acc = jnp.zeros(x.shape, F32)
    for (lo, hi), gate, up in zip(FF_CHUNKS, gates, ups):
        h = (_silu(gate) * up).astype(BF16)
        acc = acc + _dot(h, wout_ref[lo:hi, :])
    y = _layer_norm(DN_ALPHA * x + 0.5 * acc, g_ref[...], b_ref[...])
    y_ref[...] = y
    yb_ref[...] = y.astype(BF16)


def _ffn_ln(x, w_in, w_out, g, b):
    T = x.shape[0]
    tm = TOKEN_TILE
    row = lambda i: (i, 0)
    return pl.pallas_call(
        _ffn_ln_kernel,
        out_shape=(jax.ShapeDtypeStruct((T, D_MODEL), F32), jax.ShapeDtypeStruct((T, D_MODEL), BF16)),
        grid=(T // tm,),
        in_specs=[pl.BlockSpec((tm, D_MODEL), row), _resident((D_MODEL, 2 * D_FF)),
                  _resident((D_FF, D_MODEL)), _resident((1, D_MODEL)), _resident((1, D_MODEL))],
        out_specs=(pl.BlockSpec((tm, D_MODEL), row), pl.BlockSpec((tm, D_MODEL), row)),
        compiler_params=_params("arbitrary"),
        name="ffn_ln",
    )(x, w_in, w_out, g, b)


def _qkvo_kernel(x_ref, xp_ref, xn_ref, wq_ref, bq_ref, cwq_ref, cbq_ref, wk_ref, bk_ref, cwk_ref, cbk_ref,
                 wvot_ref, bvot_ref, qt_ref, k_ref, vot_ref, xe_ref, uq_ref, uk_ref, xt_ref, *, tiles_per_seq):
    tm = x_ref.shape[0]
    H = CONV_HALO
    xt_ref[...] = x_ref[...].astype(F32).T.astype(BF16)
    xe_ref[0:H, :] = xp_ref[...]
    xe_ref[H:H + tm, :] = x_ref[...]
    xe_ref[H + tm:, :] = xn_ref[...]
    pos = pl.program_id(0) % tiles_per_seq
    first = pos == 0
    last = pos == tiles_per_seq - 1

    def project(w_ref, b_ref, u_ref):
        u = _dot(xe_ref[...], w_ref[...]) + b_ref[...]
        u_ref[0:H, :] = jnp.where(first, 0.0, u[0:H, :])
        u_ref[H:H + tm, :] = u[H:H + tm, :]
        u_ref[H + tm:, :] = jnp.where(last, 0.0, u[H + tm:, :])

    def conv_silu(u_ref, cw_ref, cb_ref):
        out = cb_ref[...]
        for j in range(CONV_W):
            out = out + u_ref[pl.ds(H - 1 + j, tm), :] * cw_ref[j:j + 1, :]
        return _silu(out)

    project(wq_ref, bq_ref, uq_ref)
    project(wk_ref, bk_ref, uk_ref)
    qt_ref[...] = (conv_silu(uq_ref, cwq_ref, cbq_ref) * (MLSTM_DK ** -0.5)).T.astype(qt_ref.dtype)
    k_ref[...] = conv_silu(uk_ref, cwk_ref, cbk_ref).astype(k_ref.dtype)
    for i in range(vot_ref.shape[0] // QKVO_ROWS):
        rows = slice(i * QKVO_ROWS, (i + 1) * QKVO_ROWS)
        vot_ref[rows, :] = (_dot(wvot_ref[rows, :], xt_ref[...]) + bvot_ref[rows, :]).astype(vot_ref.dtype)


def _qkvo(xb, w, seq_len):
    T = xb.shape[0]
    tm, H = TOKEN_TILE, CONV_HALO
    tiles_per_seq = seq_len // tm
    hpt = tm // H
    last_halo = T // H - 1
    conv_w = lambda: [_resident((D_MODEL, MLSTM_QK)), _resident((1, MLSTM_QK)),
                      _resident((CONV_W, MLSTM_QK)), _resident((1, MLSTM_QK))]
    return pl.pallas_call(
        functools.partial(_qkvo_kernel, tiles_per_seq=tiles_per_seq),
        out_shape=(jax.ShapeDtypeStruct((MLSTM_QK, T), BF16), jax.ShapeDtypeStruct((T, MLSTM_QK), BF16),
                   jax.ShapeDtypeStruct((2 * MLSTM_V, T), BF16)),
        grid=(T // tm,),
        in_specs=[pl.BlockSpec((tm, D_MODEL), lambda i: (i, 0)),
                  pl.BlockSpec((H, D_MODEL), lambda i: (jnp.maximum(i * hpt - 1, 0), 0)),
                  pl.BlockSpec((H, D_MODEL), lambda i: (jnp.minimum((i + 1) * hpt, last_halo), 0))]
        + conv_w() + conv_w() + [_resident((2 * MLSTM_V, D_MODEL)), _resident((2 * MLSTM_V, 1))],
        out_specs=(pl.BlockSpec((MLSTM_QK, tm), lambda i: (0, i)), pl.BlockSpec((tm, MLSTM_QK), lambda i: (i, 0)),
                   pl.BlockSpec((2 * MLSTM_V, tm), lambda i: (0, i))),
        scratch_shapes=[pltpu.VMEM((tm + 2 * H, D_MODEL), BF16), pltpu.VMEM((tm + 2 * H, MLSTM_QK), F32),
                        pltpu.VMEM((tm + 2 * H, MLSTM_QK), F32), pltpu.VMEM((D_MODEL, tm), BF16)],
        compiler_params=_params("arbitrary"),
        name="qkvo",
    )(xb, xb, xb, w["w_q"], w["b_q"], w["cw_q"], w["cb_q"], w["w_k"], w["b_k"], w["cw_k"], w["cb_k"],
      w["w_vot"], w["b_vot"])


def _mgxy_gates_kernel(x_ref, wmg_ref, bmg_ref, wxy_ref, bxy_ref, w_ref, wt_ref, b_ref, bt_ref,
                       mg_ref, xy_ref, gc_ref, gr_ref):
    L = MLSTM_CHUNK
    H = MLSTM_HEADS
    x = x_ref[...]
    mg_ref[...] = (_dot(x, wmg_ref[...]) + bmg_ref[...]).astype(mg_ref.dtype)
    xy = _dot(x, wxy_ref[...]) + bxy_ref[...]
    xy_ref[...] = _pack_bf16_pair(xy[:, :D_RNN], xy[:, D_RNN:])
    g = _dot(x, w_ref[...]) + b_ref[...]
    gt = _dot_nt(wt_ref[...], x) + bt_ref[...]
    ls_c = _log_sigmoid(g)
    ls_r = _log_sigmoid(gt)
    row = lax.broadcasted_iota(jnp.int32, (L, L), 0)
    col = lax.broadcasted_iota(jnp.int32, (L, L), 1)
    tril = (col <= row).astype(BF16)
    triu = (col >= row).astype(BF16)
    c_idx = lax.broadcasted_iota(jnp.int32, (L, N_GATES), 1)
    r_idx = lax.broadcasted_iota(jnp.int32, (N_GATES, L), 0)
    for k in range(x.shape[0] // L):
        sl = slice(k * L, (k + 1) * L)
        lc = ls_c[sl, :]
        pre = sum(_dot(tril, p) for p in _split3(lc))
        suf = pre[L - 1:L, :] - pre + lc
        cum = jnp.where(c_idx >= 2 * H, suf, pre)
        gc_ref[sl, :] = jnp.where(c_idx % (2 * H) >= H, cum, g[sl, :])
        lr = ls_r[:, sl]
        pre = sum(_dot(p, triu) for p in _split3(lr))
        suf = pre[:, L - 1:L] - pre + lr
        cum = jnp.where(r_idx >= 2 * H, suf, pre)
        gr_ref[:, sl] = jnp.where(r_idx % (2 * H) >= H, cum, gt[:, sl])


def _mgxy_gates(xb, w, seq_len):
    T = xb.shape[0]
    tm = TOKEN_TILE
    tiles_per_seq = seq_len // tm
    row = lambda i: (i, 0)
    return pl.pallas_call(
        _mgxy_gates_kernel,
        out_shape=(jax.ShapeDtypeStruct((T, 2 * D_MODEL), BF16),
                   jax.ShapeDtypeStruct((seq_len, (T // seq_len) * D_RNN), jnp.uint32),
                   jax.ShapeDtypeStruct((T, N_GATES), F32), jax.ShapeDtypeStruct((N_GATES, T), F32)),
        grid=(T // tm,),
        in_specs=[pl.BlockSpec((tm, D_MODEL), row),
                  _resident((D_MODEL, 2 * D_MODEL)), _resident((1, 2 * D_MODEL)),
                  _resident((D_MODEL, 2 * D_RNN)), _resident((1, 2 * D_RNN)),
                  _resident((D_MODEL, N_GATES)), _resident((N_GATES, D_MODEL)),
                  _resident((1, N_GATES)), _resident((N_GATES, 1))],
        out_specs=(pl.BlockSpec((tm, 2 * D_MODEL), row),
                   pl.BlockSpec((tm, D_RNN), lambda i: (i % tiles_per_seq, i // tiles_per_seq)),
                   pl.BlockSpec((tm, N_GATES), row), pl.BlockSpec((N_GATES, tm), lambda i: (0, i))),
        compiler_params=_params("arbitrary"),
        name="mgxy_gates",
    )(xb, w["w_mg"], w["b_mg"], w["w_xy"], w["b_xy"], w["w_g"], w["w_gt"], w["b_g"], w["b_gt"])


def _mlstm_kernel(*refs, reverse):
    if reverse:
        qt_ref, k_ref, vt_ref, gc_ref, gr_ref, out_ref, ct_ref, rt_ref, n_ref, m_ref = refs
    else:
        qt_ref, k_ref, vt_ref, gc_ref, gr_ref, hbt_ref, ot_ref, gain_ref, out_ref, \
            ct_ref, rt_ref, n_ref, m_ref = refs
    L = k_ref.shape[0]
    H, DK, DV = MLSTM_HEADS, MLSTM_DK, MLSTM_DV
    base = 2 * H if reverse else 0

    @pl.when(pl.program_id(1) == 0)
    def _():
        ct_ref[...] = jnp.zeros(ct_ref.shape, F32)
        n_ref[...] = jnp.zeros(n_ref.shape, F32)
        m_ref[...] = jnp.zeros(m_ref.shape, F32)

    key = lax.broadcasted_iota(jnp.int32, (L, L), 0)
    qry = lax.broadcasted_iota(jnp.int32, (L, L), 1)
    mask = (key >= qry) if reverse else (key <= qry)
    gc = gc_ref[:, base:base + 2 * H] * LOG2E
    gr = gr_ref[base:base + 2 * H, :] * LOG2E
    last = 0 if reverse else L - 1

    g_rows = [gr[H + h:H + h + 1, :] for h in range(H)]
    b_rows = [gr[h:h + 1, :] - gr[H + h:H + h + 1, :] for h in range(H)]
    b_cols = [gc[:, h:h + 1] - gc[:, H + h:H + h + 1] for h in range(H)]
    ms = [m_ref[h:h + 1, 0:1] for h in range(H)]
    qts = [qt_ref[h * DK:(h + 1) * DK, :] for h in range(H)]
    ks = [k_ref[:, h * DK:(h + 1) * DK] for h in range(H)]
    vts = [vt_ref[h * DV:(h + 1) * DV, :] for h in range(H)]

    sts = [_dot(ks[h], qts[h]) for h in range(H)]

    for h in range(H):
        sl = slice(h * DV, (h + 1) * DV)
        log_d = jnp.where(mask, g_rows[h] + b_cols[h], -jnp.inf)
        log_inter = g_rows[h] + ms[h]
        m_t = jnp.maximum(log_inter, jnp.max(log_d, axis=0, keepdims=True))
        s = sts[h] * jnp.exp2(log_d - m_t)
        w_inter = jnp.exp2(log_inter - m_t)
        lhs = jnp.concatenate([s.astype(BF16), qts[h] * w_inter.astype(BF16)], axis=0)
        rt_ref[h, :DV, :L] = vts[h]
        rt_ref[h, DV:, :L] = jnp.ones((LANES, L), BF16)
        rt_ref[h, :DV, L:] = ct_ref[h].astype(BF16)
        rt_ref[h, DV:, L:] = jnp.broadcast_to(n_ref[h][0:1, :].astype(BF16), (LANES, DK))
        res = _dot(rt_ref[h], lhs)
        den = res[DV:DV + 1, :]
        hh = res[:DV, :] * (1.0 / jnp.maximum(jnp.abs(den), jnp.exp2(-m_t)))
        if reverse:
            out_ref[sl, :] = hh.astype(out_ref.dtype)
        else:
            hh = hh + hbt_ref[sl, :].astype(F32)
            mu = jnp.mean(hh, axis=0, keepdims=True)
            d = hh - mu
            var = jnp.mean(d * d, axis=0, keepdims=True)
            hn = d * lax.rsqrt(var + LN_EPS) * gain_ref[sl, :]
            out_ref[sl, :] = (hn + hn * jnp.tanh(0.5 * ot_ref[sl, :].astype(F32))).astype(out_ref.dtype)

    for h in range(H):
        g_tot = g_rows[h][:, last:last + 1]
        m_new = jnp.maximum(g_tot + ms[h], jnp.max(g_tot + b_rows[h], axis=-1, keepdims=True))
        wk = ks[h].astype(F32) * jnp.exp2(g_tot + b_cols[h] - m_new)
        decay = jnp.exp2(g_tot + ms[h] - m_new)
        ct_new = decay * ct_ref[h] + _dot(vts[h], wk.astype(BF16))
        n_new = decay * n_ref[h] + jnp.sum(wk, axis=0, keepdims=True)
        ct_ref[h] = ct_new
        n_ref[h] = n_new
        m_ref[h:h + 1, :] = jnp.broadcast_to(m_new, (1, m_ref.shape[1]))


def _mlstm_dir(qt, k, vot, gc, gr, hbt, gain_half, batch, seq_len, reverse):
    T = k.shape[0]
    L = MLSTM_CHUNK
    nc = seq_len // L
    H = MLSTM_HEADS
    pos = (lambda b, c: b * nc + nc - 1 - c) if reverse else (lambda b, c: b * nc + c)
    cols = lambda height, blk: pl.BlockSpec((height, L), lambda b, c: (blk, pos(b, c)))
    in_specs = [cols(MLSTM_QK, 0),
                pl.BlockSpec((L, MLSTM_QK), lambda b, c: (pos(b, c), 0)),
                cols(MLSTM_V, 0),
                pl.BlockSpec((L, N_GATES), lambda b, c: (pos(b, c), 0)),
                cols(N_GATES, 0)]
    args = [qt, k, vot, gc, gr]
    if not reverse:
        in_specs += [cols(MLSTM_V, 0), cols(MLSTM_V, 1), _resident((MLSTM_V, L))]
        args += [hbt, vot, gain_half]
    return pl.pallas_call(
        functools.partial(_mlstm_kernel, reverse=reverse),
        out_shape=jax.ShapeDtypeStruct((MLSTM_V, T), BF16),
        grid=(batch, nc),
        in_specs=in_specs,
        out_specs=cols(MLSTM_V, 0),
        scratch_shapes=[pltpu.VMEM((H, MLSTM_DV, MLSTM_DK), F32),
                        pltpu.VMEM((H, MLSTM_RW, L + MLSTM_DK), BF16),
                        pltpu.VMEM((H, SUBLANES, MLSTM_DK), F32),
                        pltpu.VMEM((SUBLANES, LANES), F32)],
        compiler_params=_params("arbitrary", "arbitrary"),
        name="mlstm_bwd" if reverse else "mlstm_fwd",
    )(*args)


def _mlstm(qt, k, vot, gc, gr, gain_half, batch, seq_len):
    hbt = _mlstm_dir(qt, k, vot, gc, gr, None, None, batch, seq_len, True)
    return _mlstm_dir(qt, k, vot, gc, gr, hbt, gain_half, batch, seq_len, False)


def _rglru_kernel(*refs, nc, reverse):
    if reverse:
        x_ref, xp_ref, xn_ref, cw_ref, cb_ref, wa_ref, ba_ref, wx_ref, bx_ref, lam_ref, o_ref, \
            a_ref, u_ref, state_ref, k_ref = refs
    else:
        x_ref, xp_ref, xn_ref, hb_ref, cw_ref, cb_ref, wa_ref, ba_ref, wx_ref, bx_ref, lam_ref, \
            o_ref, a_ref, u_ref, state_ref, k_ref, gy_ref = refs
    tc = x_ref.shape[0]
    c = pl.program_id(1)
    cc = nc - 1 - c if reverse else c

    @pl.when(c == 0)
    def _():
        state_ref[...] = jnp.zeros(state_ref.shape, F32)
        k_ref[...] = (-0.5 * LRU_C * LOG2E) * _softplus(-lam_ref[...])

    xr, yr = _unpack_bf16_pair(x_ref[...])
    prev = jnp.where(cc > 0, _unpack_bf16_pair(xp_ref[...])[0], 0.0)
    nxt = jnp.where(cc < nc - 1, _unpack_bf16_pair(xn_ref[...])[0], 0.0)
    xe = jnp.concatenate([prev, xr, nxt], axis=0)
    xc = cb_ref[...]
    for j in range(CONV_W):
        xc = xc + xe[j:j + tc] * cw_ref[j:j + 1, :]

    x2 = xc.reshape(tc * LRU_GROUP, D_RNN)
    xb = x2.astype(BF16)
    for n in range(LRU_BLOCKS):
        sl = slice(n * LRU_BW, (n + 1) * LRU_BW)
        t_a = jnp.tanh(_dot(xb[:, sl], wa_ref[n]) + ba_ref[:, sl])
        t_i = jnp.tanh(_dot(xb[:, sl], wx_ref[n]) + bx_ref[:, sl])
        a = jnp.exp2(k_ref[:, sl] + k_ref[:, sl] * t_a)
        y = 1.0 - a * a
        root = jnp.where(y > 0.0, y * lax.rsqrt(y), 0.0)
        hx = 0.5 * x2[:, sl]
        u = root * (hx + hx * t_i)
        a_ref[:, :, sl] = a.reshape(tc, LRU_GROUP, LRU_BW)
        u_ref[:, :, sl] = u.reshape(tc, LRU_GROUP, LRU_BW)

    if reverse:
        def step(k, h):
            t = tc - 1 - k
            h = a_ref[t] * h + u_ref[t]
            o_ref[t] = _pack_bf16_pair(h[:, :D_RNN // 2], h[:, D_RNN // 2:])
            return h
    else:
        gy_ref[...] = jax.nn.gelu(yr)

        def step(t, h):
            h = a_ref[t] * h + u_ref[t]
            hb = jnp.concatenate(_unpack_bf16_pair(hb_ref[t]), axis=1)
            o_ref[t] = (h + hb) * gy_ref[t]
            return h

    state_ref[...] = lax.fori_loop(0, tc, step, state_ref[...], unroll=LRU_UNROLL)


def _rglru_dir(xy, hb, cw, cb, wa, ba, wx, bx, lam, batch, seq_len, reverse):
    tc, G, C = LRU_TCHUNK, LRU_GROUP, D_RNN
    nc = seq_len // tc
    chunk = (lambda c: nc - 1 - c) if reverse else (lambda c: c)
    blk = lambda width: pl.BlockSpec((tc, G, width), lambda g, c: (chunk(c), g, 0))
    in_specs = [blk(C),
                pl.BlockSpec((1, G, C), lambda g, c: (jnp.maximum(chunk(c) * tc - 1, 0), g, 0)),
                pl.BlockSpec((2, G, C),
                             lambda g, c: (jnp.minimum((chunk(c) + 1) * (tc // 2), seq_len // 2 - 1), g, 0))]
    args = [xy, xy, xy]
    scratch = [pltpu.VMEM((tc, G, C), F32), pltpu.VMEM((tc, G, C), F32),
               pltpu.VMEM((G, C), F32), pltpu.VMEM((1, C), F32)]
    if not reverse:
        in_specs += [blk(C // 2)]
        args += [hb]
        scratch += [pltpu.VMEM((tc, G, C), F32)]
    in_specs += [_resident((CONV_W, C)), _resident((1, C)),
                 _resident((LRU_BLOCKS, LRU_BW, LRU_BW)), _resident((1, C)),
                 _resident((LRU_BLOCKS, LRU_BW, LRU_BW)), _resident((1, C)), _resident((1, C))]
    args += [cw, cb, wa, ba, wx, bx, lam]
    return pl.pallas_call(
        functools.partial(_rglru_kernel, nc=nc, reverse=reverse),
        out_shape=jax.ShapeDtypeStruct((seq_len, batch, C // 2), jnp.uint32) if reverse
        else jax.ShapeDtypeStruct((seq_len, batch, C), F32),
        grid=(batch // G, nc),
        in_specs=in_specs,
        out_specs=blk(C // 2) if reverse else blk(C),
        scratch_shapes=scratch,
        compiler_params=_params("arbitrary", "arbitrary"),
        name="rglru_bwd" if reverse else "rglru_fwd",
    )(*args)


def _rglru(xy, w, batch, seq_len):
    common = (w["cw_r"], w["cb_r"])
    hb = _rglru_dir(xy, None, *common, w["lru_wa"][1], w["lru_ba"][1:2], w["lru_wx"][1], w["lru_bx"][1:2],
                    w["lru_lam"][1:2], batch, seq_len, True)
    return _rglru_dir(xy, hb, *common, w["lru_wa"][0], w["lru_ba"][0:1], w["lru_wx"][0], w["lru_bx"][0:1],
                      w["lru_lam"][0:1], batch, seq_len, False)


def _kv_kernel(m_ref, g_ref, b_ref, w_ref, o_ref):
    m = _layer_norm(m_ref[...], g_ref[...], b_ref[...])
    o_ref[...] = _dot(m.astype(BF16), w_ref[...]).astype(o_ref.dtype)


def _mem_kv(mem, g, b, w_kv):
    R = mem.shape[0]
    return pl.pallas_call(
        _kv_kernel,
        out_shape=jax.ShapeDtypeStruct((R, 2 * D_MODEL), BF16),
        grid=(R // N_MEM,),
        in_specs=[pl.BlockSpec((N_MEM, D_MODEL), lambda i: (i, 0)), _resident((1, D_MODEL)),
                  _resident((1, D_MODEL)), _resident((D_MODEL, 2 * D_MODEL))],
        out_specs=pl.BlockSpec((N_MEM, 2 * D_MODEL), lambda i: (i, 0)),
        compiler_params=_params("arbitrary"),
        name="mem_kv",
    )(mem, g, b, w_kv)


def _merge_xattn_kernel(x_ref, hmt_ref, mg_ref, hr_ref, kv_ref,
                        wpm_ref, wpr_ref, wout_ref, wq_ref, wo_ref, lng_ref, lnb_ref, y_ref):
    half = x_ref.shape[0] // 2
    rows = [slice(0, half), slice(half, 2 * half)]
    pm = [_dot_tn(hmt_ref[:, r], wpm_ref[...]) for r in rows]
    pr = [_dot(hr_ref[r, :].astype(BF16), wpr_ref[...]) for r in rows]
    merged = []
    for r, a, b in zip(rows, pm, pr):
        g_m = _sigmoid(mg_ref[r, :D_MODEL].astype(F32))
        g_r = _sigmoid(mg_ref[r, D_MODEL:].astype(F32))
        merged.append((g_m * a + g_r * b).astype(BF16))
    mix = [_dot(m, wout_ref[...]) for m in merged]
    xs = [_layer_norm(DN_ALPHA * x_ref[r, :] + m, lng_ref[0:1, :], lnb_ref[0:1, :]) for r, m in zip(rows, mix)]

    qs = [_dot(x.astype(BF16), wq_ref[...]).astype(BF16) for x in xs]
    heads = [[], []]
    for h in range(XATTN_HEADS):
        sl = slice(h * XATTN_HD, (h + 1) * XATTN_HD)
        ss = [_dot_nt(q[:, sl], kv_ref[:, sl]) * (XATTN_HD ** -0.5) for q in qs]
        for i, s in enumerate(ss):
            e = jnp.exp(s - jnp.max(s, axis=-1, keepdims=True))
            p = e * (1.0 / jnp.sum(e, axis=-1, keepdims=True))
            oh = _dot(p.astype(BF16), kv_ref[:, D_MODEL + h * XATTN_HD:D_MODEL + (h + 1) * XATTN_HD])
            heads[i].append(oh.astype(BF16))
    for r, x, ohs in zip(rows, xs, heads):
        xa = _dot(jnp.concatenate(ohs, axis=1), wo_ref[...])
        y_ref[r, :] = _layer_norm(DN_ALPHA * x + xa, lng_ref[1:2, :], lnb_ref[1:2, :])


def _merge_xattn(x, hmt, mg, hr, kv, wpm, wpr, wout, wq, wo, lng, lnb, seq_len):
    T = x.shape[0]
    tm = MERGE_TILE
    tiles_per_seq = seq_len // tm
    row = lambda i: (i, 0)
    return pl.pallas_call(
        _merge_xattn_kernel,
        out_shape=jax.ShapeDtypeStruct((T, D_MODEL), F32),
        grid=(T // tm,),
        in_specs=[pl.BlockSpec((tm, D_MODEL), row),
                  pl.BlockSpec((MLSTM_V, tm), lambda i: (0, i)),
                  pl.BlockSpec((tm, 2 * D_MODEL), row),
                  pl.BlockSpec((tm, D_RNN), lambda i: (i % tiles_per_seq, i // tiles_per_seq)),
                  pl.BlockSpec((N_MEM, 2 * D_MODEL), lambda i: (i // tiles_per_seq, 0)),
                  _resident((MLSTM_V, D_MODEL)), _resident((D_RNN, D_MODEL)),
                  _resident((D_MODEL, D_MODEL)), _resident((D_MODEL, D_MODEL)),
                  _resident((D_MODEL, D_MODEL)), _resident((2, D_MODEL)), _resident((2, D_MODEL))],
        out_specs=pl.BlockSpec((tm, D_MODEL), row),
        compiler_params=_params("arbitrary"),
        name="merge_xattn",
    )(x, hmt, mg, hr, kv, wpm, wpr, wout, wq, wo, lng, lnb)


def _prep_layer(p, l):
    w_in, b_in = p["w_in"][l], p["b_in"][l]
    cols = lambda lo, hi: w_in[:, lo:hi]
    bias = lambda lo, hi: b_in[lo:hi]
    w_g = cols(OFF_GATE, OFF_XR).astype(BF16)
    b_g = bias(OFF_GATE, OFF_XR)
    return dict(
        ff1_in=p["ff1_in"][l].astype(BF16), ff1_out=p["ff1_out"][l].astype(BF16),
        ff2_in=p["ff2_in"][l].astype(BF16), ff2_out=p["ff2_out"][l].astype(BF16),
        ln_g=p["ln_g"][l], ln_b=p["ln_b"][l],
        w_q=cols(0, MLSTM_QK).astype(BF16), b_q=bias(0, MLSTM_QK)[None],
        cw_q=p["w_conv_qk"][l][:, :MLSTM_QK], cb_q=p["b_conv_qk"][l][None, :MLSTM_QK],
        w_k=cols(MLSTM_QK, OFF_V).astype(BF16), b_k=bias(MLSTM_QK, OFF_V)[None],
        cw_k=p["w_conv_qk"][l][:, MLSTM_QK:], cb_k=p["b_conv_qk"][l][None, MLSTM_QK:],
        w_vot=cols(OFF_V, OFF_GATE).astype(BF16).T, b_vot=bias(OFF_V, OFF_GATE)[:, None],
        w_mg=cols(OFF_MG, D_IN).astype(BF16), b_mg=bias(OFF_MG, D_IN)[None],
        w_xy=cols(OFF_XR, OFF_MG).astype(BF16), b_xy=bias(OFF_XR, OFF_MG)[None],
        w_g=w_g, w_gt=w_g.T, b_g=b_g[None], b_gt=b_g[:, None],
        cw_r=p["w_conv_r"][l], cb_r=p["b_conv_r"][l][None],
        lru_wa=(0.5 * p["lru_wa"][l]).astype(BF16), lru_ba=0.5 * p["lru_ba"][l],
        lru_wx=(0.5 * p["lru_wx"][l]).astype(BF16), lru_bx=0.5 * p["lru_bx"][l], lru_lam=p["lru_lam"][l],
        mh_gain_half=jnp.broadcast_to(0.5 * p["mh_gain"][l][:, None], (MLSTM_V, MLSTM_CHUNK)),
        w_pm=p["w_pm"][l].astype(BF16), w_pr=p["w_pr"][l].astype(BF16), w_out=p["w_out"][l].astype(BF16),
        xa_wq=p["xa_wq"][l].astype(BF16), xa_wkv=p["xa_wkv"][l].astype(BF16), xa_wo=p["xa_wo"][l].astype(BF16),
        mem_ln_g=p["mem_ln_g"][l][None], mem_ln_b=p["mem_ln_b"][l][None],
    )


def _trunk(x, mem, layers):
    B, S, _ = x.shape
    T = B * S
    x = x.reshape(T, D_MODEL)
    mem = mem.reshape(B * N_MEM, D_MODEL)
    for w in layers:
        x, xb = _ffn_ln(x, w["ff1_in"], w["ff1_out"], w["ln_g"][0:1], w["ln_b"][0:1])
        qt, k, vot = _qkvo(xb, w, S)
        mg, xy, gc, gr = _mgxy_gates(xb, w, S)
        hmt = _mlstm(qt, k, vot, gc, gr, w["mh_gain_half"], B, S)
        hr = _rglru(xy.reshape(S, B, D_RNN), w, B, S)
        kv = _mem_kv(mem, w["mem_ln_g"], w["mem_ln_b"], w["xa_wkv"])
        x = _merge_xattn(x, hmt, mg, hr.reshape(S, B * D_RNN), kv, w["w_pm"], w["w_pr"],
                         w["w_out"], w["xa_wq"], w["xa_wo"], w["ln_g"][1:3], w["ln_b"][1:3], S)
        x, _ = _ffn_ln(x, w["ff2_in"], w["ff2_out"], w["ln_g"][3:4], w["ln_b"][3:4])
    return x.reshape(B, S, D_MODEL)


def kernel(x_prompt, x_sample, mem_prompt, mem_sample, w_in, b_in, w_conv_qk, b_conv_qk, w_conv_r, b_conv_r,
           mh_gain, lru_wa, lru_ba, lru_wx, lru_bx, lru_lam, w_pm, w_pr, w_out, xa_wq, xa_wkv, xa_wo,
           mem_ln_g, mem_ln_b, ff1_in, ff1_out, ff2_in, ff2_out, ln_g, ln_b):
    p = dict(w_in=w_in, b_in=b_in, w_conv_qk=w_conv_qk, b_conv_qk=b_conv_qk, w_conv_r=w_conv_r,
             b_conv_r=b_conv_r, mh_gain=mh_gain, lru_wa=lru_wa, lru_ba=lru_ba, lru_wx=lru_wx,
             lru_bx=lru_bx, lru_lam=lru_lam, w_pm=w_pm, w_pr=w_pr, w_out=w_out, xa_wq=xa_wq,
             xa_wkv=xa_wkv, xa_wo=xa_wo, mem_ln_g=mem_ln_g, mem_ln_b=mem_ln_b, ff1_in=ff1_in,
             ff1_out=ff1_out, ff2_in=ff2_in, ff2_out=ff2_out, ln_g=ln_g, ln_b=ln_b)
    layers = [_prep_layer(p, l) for l in range(DEPTH)]
    return (_trunk(x_prompt, mem_prompt, layers), _trunk(x_sample, mem_sample, layers))
```

```python
import functools
import math

import jax
import jax.numpy as jnp
from jax import lax
from jax.experimental import pallas as pl
from jax.experimental.pallas import tpu as pltpu

F32 = jnp.float32
BF16 = jnp.bfloat16

D_MODEL = 1024
DEPTH = 2
N_MEM = 256
MLSTM_HEADS = 4
MLSTM_DK = D_MODEL // 4
MLSTM_DV = D_MODEL // 2
MLSTM_QK = MLSTM_HEADS * MLSTM_DK
MLSTM_V = MLSTM_HEADS * MLSTM_DV
D_RNN = D_MODEL
LRU_BLOCKS = 8
LRU_BW = D_RNN // LRU_BLOCKS
LRU_C = 8.0
CONV_W = 4
XATTN_HEADS = 4
XATTN_HD = D_MODEL // XATTN_HEADS
D_FF = 2816
DN_ALPHA = (2.0 * DEPTH) ** 0.25
LN_EPS = 1e-5

OFF_V = 2 * MLSTM_QK
OFF_O = OFF_V + MLSTM_V
OFF_GATE = OFF_O + MLSTM_V
OFF_XR = OFF_GATE + 4 * MLSTM_HEADS
OFF_YR = OFF_XR + D_RNN
OFF_MG = OFF_YR + D_RNN
D_IN = OFF_MG + 2 * D_MODEL
N_GATES = 4 * MLSTM_HEADS

VMEM_LIMIT_BYTES = 56 * 1024 * 1024
LANES = 128
SUBLANES = 8
BF16_ROWS = 16

TOKEN_TILE = 512
MERGE_TILE = 512
MXU_DEPTH = 256
FF_CHUNKS = ((0, 6 * MXU_DEPTH), (6 * MXU_DEPTH, D_FF))
MLSTM_CHUNK = 256
MLSTM_ROWS = 2
LRU_TCHUNK = 128
LRU_GROUP = SUBLANES
LRU_UNROLL = 8
CONV_HALO = BF16_ROWS
QKVO_ROWS = 1024
MLSTM_RW = MLSTM_DV + LANES
LOG2E = math.log2(math.e)


def _dot(a, b):
    return jnp.dot(a, b, preferred_element_type=F32)


def _dot_nt(a, b):
    return lax.dot_general(a, b, (((1,), (1,)), ((), ())), preferred_element_type=F32)


def _dot_tn(a, b):
    return lax.dot_general(a, b, (((0,), (0,)), ((), ())), preferred_element_type=F32)


def _layer_norm(y, g, b):
    mu = jnp.mean(y, axis=-1, keepdims=True)
    d = y - mu
    var = jnp.mean(d * d, axis=-1, keepdims=True)
    return d * lax.rsqrt(var + LN_EPS) * g + b


def _sigmoid(x):
    return 0.5 + 0.5 * jnp.tanh(0.5 * x)


def _silu(x):
    h = 0.5 * x
    return h + h * jnp.tanh(h)


def _log_sigmoid(x):
    return jnp.minimum(x, 0.0) - jnp.log1p(jnp.exp(-jnp.abs(x)))


def _softplus(x):
    return jnp.maximum(x, 0.0) + jnp.log1p(jnp.exp(-jnp.abs(x)))


def _pack_bf16_pair(hi, lo):
    hi_bits = lax.bitcast_convert_type(hi.astype(BF16).astype(F32), jnp.uint32)
    lo_bits = lax.bitcast_convert_type(lo.astype(BF16).astype(F32), jnp.uint32)
    return hi_bits | (lo_bits >> 16)


def _unpack_bf16_pair(w):
    hi = lax.bitcast_convert_type(w & jnp.uint32(0xFFFF0000), F32)
    lo = lax.bitcast_convert_type(w << 16, F32)
    return hi, lo


def _split3(x):
    hi = x.astype(BF16)
    r = x - hi.astype(F32)
    mid = r.astype(BF16)
    lo = (r - mid.astype(F32)).astype(BF16)
    return hi, mid, lo


def _resident(shape):
    zeros = (0,) * len(shape)
    return pl.BlockSpec(shape, lambda *_: zeros, pipeline_mode=pl.Buffered(1))


def _params(*sem):
    return pltpu.CompilerParams(dimension_semantics=sem, vmem_limit_bytes=VMEM_LIMIT_BYTES)


def _ffn_ln_kernel(x_ref, win_ref, wout_ref, g_ref, b_ref, y_ref, yb_ref):
    half = x_ref.shape[0] // 2
    rows = [slice(0, half), slice(half, 2 * half)]
    ups = []
    for r in rows:
        xb = x_ref[r, :].astype(BF16)
        ups.append(([_dot(xb, win_ref[:, lo:hi]) for lo, hi in FF_CHUNKS],
                    [_dot(xb, win_ref[:, D_FF + lo:D_FF + hi]) for lo, hi in FF_CHUNKS]))
    for r, (gates, vals) in zip(rows, ups):
        acc = jnp.zeros((half, D_MODEL), F32)
        for (lo, hi), gate, up in zip(FF_CHUNKS, gates, vals):
            acc = acc + _dot((_silu(gate) * up).astype(BF16), wout_ref[lo:hi, :])
        y = _layer_norm(DN_ALPHA * x_ref[r, :] + 0.5 * acc, g_ref[...], b_ref[...])
        y_ref[r, :] = y
        yb_ref[r, :] = y.astype(BF16)


def _ffn_ln(x, w_in, w_out, g, b):
    T = x.shape[0]
    tm = TOKEN_TILE
    row = lambda i: (i, 0)
    return pl.pallas_call(
        _ffn_ln_kernel,
        out_shape=(jax.ShapeDtypeStruct((T, D_MODEL), F32), jax.ShapeDtypeStruct((T, D_MODEL), BF16)),
        grid=(T // tm,),
        in_specs=[pl.BlockSpec((tm, D_MODEL), row), _resident((D_MODEL, 2 * D_FF)),
                  _resident((D_FF, D_MODEL)), _resident((1, D_MODEL)), _resident((1, D_MODEL))],
        out_specs=(pl.BlockSpec((tm, D_MODEL), row), pl.BlockSpec((tm, D_MODEL), row)),
        compiler_params=_params("arbitrary"),
        name="ffn_ln",
    )(x, w_in, w_out, g, b)


def _qkvo_kernel(x_ref, xp_ref, xn_ref, wq_ref, bq_ref, cwq_ref, cbq_ref, wk_ref, bk_ref, cwk_ref, cbk_ref,
                 wvot_ref, bvot_ref, qt_ref, k_ref, vot_ref, xe_ref, xt_ref, *, tiles_per_seq):
    tm = x_ref.shape[0]
    H = CONV_HALO
    xt_ref[...] = x_ref[...].astype(F32).T.astype(BF16)
    xe_ref[0:H, :] = xp_ref[...]
    xe_ref[H:H + tm, :] = x_ref[...]
    xe_ref[H + tm:, :] = xn_ref[...]
    pos = pl.program_id(0) % tiles_per_seq
    first = pos == 0
    last = pos == tiles_per_seq - 1

    rows_ext = tm + 2 * H
    halo_row = lax.broadcasted_iota(jnp.int32, (rows_ext, 1), 0)
    outside = ((halo_row < H) & first) | ((halo_row >= H + tm) & last)

    def conv_silu(w_ref, b_ref, cw_ref, cb_ref):
        u = jnp.where(outside, 0.0, _dot(xe_ref[...], w_ref[...]) + b_ref[...])
        out = cb_ref[...] + u[H:H + tm, :] * cw_ref[1:2, :]
        for j, shift in ((0, 1), (2, rows_ext - 1), (3, rows_ext - 2)):
            out = out + pltpu.roll(u, shift, axis=0)[H:H + tm, :] * cw_ref[j:j + 1, :]
        return _silu(out)

    qt_ref[...] = (conv_silu(wq_ref, bq_ref, cwq_ref, cbq_ref) * (MLSTM_DK ** -0.5)).T.astype(qt_ref.dtype)
    k_ref[...] = conv_silu(wk_ref, bk_ref, cwk_ref, cbk_ref).astype(k_ref.dtype)
    for i in range(vot_ref.shape[0] // QKVO_ROWS):
        rows = slice(i * QKVO_ROWS, (i + 1) * QKVO_ROWS)
        vot_ref[rows, :] = (_dot(wvot_ref[rows, :], xt_ref[...]) + bvot_ref[rows, :]).astype(vot_ref.dtype)


def _qkvo(xb, w, seq_len):
    T = xb.shape[0]
    tm, H = TOKEN_TILE, CONV_HALO
    tiles_per_seq = seq_len // tm
    hpt = tm // H
    last_halo = T // H - 1
    conv_w = lambda: [_resident((D_MODEL, MLSTM_QK)), _resident((1, MLSTM_QK)),
                      _resident((CONV_W, MLSTM_QK)), _resident((1, MLSTM_QK))]
    return pl.pallas_call(
        functools.partial(_qkvo_kernel, tiles_per_seq=tiles_per_seq),
        out_shape=(jax.ShapeDtypeStruct((MLSTM_QK, T), BF16), jax.ShapeDtypeStruct((T, MLSTM_QK), BF16),
                   jax.ShapeDtypeStruct((2 * MLSTM_V, T), BF16)),
        grid=(T // tm,),
        in_specs=[pl.BlockSpec((tm, D_MODEL), lambda i: (i, 0)),
                  pl.BlockSpec((H, D_MODEL), lambda i: (jnp.maximum(i * hpt - 1, 0), 0)),
                  pl.BlockSpec((H, D_MODEL), lambda i: (jnp.minimum((i + 1) * hpt, last_halo), 0))]
        + conv_w() + conv_w() + [_resident((2 * MLSTM_V, D_MODEL)), _resident((2 * MLSTM_V, 1))],
        out_specs=(pl.BlockSpec((MLSTM_QK, tm), lambda i: (0, i)), pl.BlockSpec((tm, MLSTM_QK), lambda i: (i, 0)),
                   pl.BlockSpec((2 * MLSTM_V, tm), lambda i: (0, i))),
        scratch_shapes=[pltpu.VMEM((tm + 2 * H, D_MODEL), BF16), pltpu.VMEM((D_MODEL, tm), BF16)],
        compiler_params=_params("arbitrary"),
        name="qkvo",
    )(xb, xb, xb, w["w_q"], w["b_q"], w["cw_q"], w["cb_q"], w["w_k"], w["b_k"], w["cw_k"], w["cb_k"],
      w["w_vot"], w["b_vot"])


def _mgxy_gates_kernel(x_ref, wmg_ref, bmg_ref, wxy_ref, bxy_ref, w_ref, wt_ref, b_ref, bt_ref,
                       mg_ref, xy_ref, gc_ref, gr_ref):
    L = MLSTM_CHUNK
    H = MLSTM_HEADS
    x = x_ref[...]
    mg_ref[...] = (_dot(x, wmg_ref[...]) + bmg_ref[...]).astype(mg_ref.dtype)
    xy = _dot(x, wxy_ref[...]) + bxy_ref[...]
    xy_ref[...] = _pack_bf16_pair(xy[:, :D_RNN], xy[:, D_RNN:])
    g = _dot(x, w_ref[...]) + b_ref[...]
    gt = _dot_nt(wt_ref[...], x) + bt_ref[...]
    ls_c = _log_sigmoid(g)
    ls_r = _log_sigmoid(gt)
    row = lax.broadcasted_iota(jnp.int32, (L, L), 0)
    col = lax.broadcasted_iota(jnp.int32, (L, L), 1)
    tril = (col <= row).astype(BF16)
    triu = (col >= row).astype(BF16)
    c_idx = lax.broadcasted_iota(jnp.int32, (L, N_GATES), 1)
    r_idx = lax.broadcasted_iota(jnp.int32, (N_GATES, L), 0)
    for k in range(x.shape[0] // L):
        sl = slice(k * L, (k + 1) * L)
        lc = ls_c[sl, :]
        pre = sum(_dot(tril, p) for p in _split3(lc))
        suf = pre[L - 1:L, :] - pre + lc
        cum = jnp.where(c_idx >= 2 * H, suf, pre)
        gc_ref[sl, :] = jnp.where(c_idx % (2 * H) >= H, cum, g[sl, :])
        lr = ls_r[:, sl]
        pre = sum(_dot(p, triu) for p in _split3(lr))
        suf = pre[:, L - 1:L] - pre + lr
        cum = jnp.where(r_idx >= 2 * H, suf, pre)
        gr_ref[:, sl] = jnp.where(r_idx % (2 * H) >= H, cum, gt[:, sl])


def _mgxy_gates(xb, w, seq_len):
    T = xb.shape[0]
    tm = TOKEN_TILE
    tiles_per_seq = seq_len // tm
    row = lambda i: (i, 0)
    return pl.pallas_call(
        _mgxy_gates_kernel,
        out_shape=(jax.ShapeDtypeStruct((T, 2 * D_MODEL), BF16),
                   jax.ShapeDtypeStruct((seq_len, (T // seq_len) * D_RNN), jnp.uint32),
                   jax.ShapeDtypeStruct((T, N_GATES), F32), jax.ShapeDtypeStruct((N_GATES, T), F32)),
        grid=(T // tm,),
        in_specs=[pl.BlockSpec((tm, D_MODEL), row),
                  _resident((D_MODEL, 2 * D_MODEL)), _resident((1, 2 * D_MODEL)),
                  _resident((D_MODEL, 2 * D_RNN)), _resident((1, 2 * D_RNN)),
                  _resident((D_MODEL, N_GATES)), _resident((N_GATES, D_MODEL)),
                  _resident((1, N_GATES)), _resident((N_GATES, 1))],
        out_specs=(pl.BlockSpec((tm, 2 * D_MODEL), row),
                   pl.BlockSpec((tm, D_RNN), lambda i: (i % tiles_per_seq, i // tiles_per_seq)),
                   pl.BlockSpec((tm, N_GATES), row), pl.BlockSpec((N_GATES, tm), lambda i: (0, i))),
        compiler_params=_params("arbitrary"),
        name="mgxy_gates",
    )(xb, w["w_mg"], w["b_mg"], w["w_xy"], w["b_xy"], w["w_g"], w["w_gt"], w["b_g"], w["b_gt"])


def _mlstm_kernel(*refs, reverse):
    R = MLSTM_ROWS
    per_row = 5 if reverse else 6
    row_refs = [refs[r * per_row:(r + 1) * per_row] for r in range(R)]
    rest = refs[R * per_row:]
    if reverse:
        out_ref, ct_ref, rt_ref, n_ref, m_ref = rest
    else:
        hbt_ref, gain_ref, out_ref, ct_ref, rt_ref, n_ref, m_ref = rest
    L = row_refs[0][1].shape[0]
    H, DK, DV = MLSTM_HEADS, MLSTM_DK, MLSTM_DV
    base = 2 * H if reverse else 0

    @pl.when(pl.program_id(1) == 0)
    def _():
        ct_ref[...] = jnp.zeros(ct_ref.shape, F32)
        n_ref[...] = jnp.zeros(n_ref.shape, F32)
        m_ref[...] = jnp.zeros(m_ref.shape, F32)

    key = lax.broadcasted_iota(jnp.int32, (L, L), 0)
    qry = lax.broadcasted_iota(jnp.int32, (L, L), 1)
    mask = (key >= qry) if reverse else (key <= qry)
    last = 0 if reverse else L - 1

    for r in range(R):
        qt_ref, k_ref, vt_ref, gc_ref, gr_ref = row_refs[r][:5]
        gc = gc_ref[:, base:base + 2 * H] * LOG2E
        gr = gr_ref[base:base + 2 * H, :] * LOG2E
        g_rows = [gr[H + h:H + h + 1, :] for h in range(H)]
        b_rows = [gr[h:h + 1, :] - gr[H + h:H + h + 1, :] for h in range(H)]
        b_cols = [gc[:, h:h + 1] - gc[:, H + h:H + h + 1] for h in range(H)]
        ms = [m_ref[r * H + h:r * H + h + 1, 0:1] for h in range(H)]
        qts = [qt_ref[h * DK:(h + 1) * DK, :] for h in range(H)]
        ks = [k_ref[:, h * DK:(h + 1) * DK] for h in range(H)]
        vts = [vt_ref[h * DV:(h + 1) * DV, :] for h in range(H)]

        sts = [_dot(ks[h], qts[h]) for h in range(H)]

        for h in range(H):
            u = r * H + h
            sl = slice(h * DV, (h + 1) * DV)
            log_d = jnp.where(mask, g_rows[h] + b_cols[h], -jnp.inf)
            log_inter = g_rows[h] + ms[h]
            m_t = jnp.maximum(log_inter, jnp.max(log_d, axis=0, keepdims=True))
            s = sts[h] * jnp.exp2(log_d - m_t)
            w_inter = jnp.exp2(log_inter - m_t)
            lhs = jnp.concatenate([s.astype(BF16), qts[h] * w_inter.astype(BF16)], axis=0)
            rt_ref[u, :DV, :L] = vts[h]
            rt_ref[u, DV:, :L] = jnp.ones((LANES, L), BF16)
            rt_ref[u, :DV, L:] = ct_ref[u].astype(BF16)
            rt_ref[u, DV:, L:] = jnp.broadcast_to(n_ref[u][0:1, :].astype(BF16), (LANES, DK))
            res = _dot(rt_ref[u], lhs)
            den = res[DV:DV + 1, :]
            hh = res[:DV, :] * (1.0 / jnp.maximum(jnp.abs(den), jnp.exp2(-m_t)))
            if reverse:
                out_ref[r, sl, :] = hh.astype(out_ref.dtype)
            else:
                ot_ref = row_refs[r][5]
                hh = hh + hbt_ref[r, sl, :].astype(F32)
                mu = jnp.mean(hh, axis=0, keepdims=True)
                d = hh - mu
                var = jnp.mean(d * d, axis=0, keepdims=True)
                hn = d * lax.rsqrt(var + LN_EPS) * gain_ref[sl, :]
                out_ref[r, sl, :] = (hn + hn * jnp.tanh(0.5 * ot_ref[sl, :].astype(F32))).astype(out_ref.dtype)

        for h in range(H):
            u = r * H + h
            g_tot = g_rows[h][:, last:last + 1]
            m_new = jnp.maximum(g_tot + ms[h], jnp.max(g_tot + b_rows[h], axis=-1, keepdims=True))
            wk = ks[h].astype(F32) * jnp.exp2(g_tot + b_cols[h] - m_new)
            decay = jnp.exp2(g_tot + ms[h] - m_new)
            ct_ref[u] = decay * ct_ref[u] + _dot(vts[h], wk.astype(BF16))
            n_ref[u] = decay * n_ref[u] + jnp.sum(wk, axis=0, keepdims=True)
            m_ref[u:u + 1, :] = jnp.broadcast_to(m_new, (1, m_ref.shape[1]))


def _mlstm_dir(qt, k, vot, gc, gr, hbt, gain_half, batch, seq_len, reverse):
    L, R, H = MLSTM_CHUNK, MLSTM_ROWS, MLSTM_HEADS
    nc = seq_len // L
    chunk = (lambda c: nc - 1 - c) if reverse else (lambda c: c)
    in_specs, args = [], []
    for r in range(R):
        pos = lambda b, c, r=r: (b * R + r) * nc + chunk(c)
        cols = lambda height, blk, pos=pos: pl.BlockSpec((height, L), lambda b, c: (blk, pos(b, c)))
        in_specs += [cols(MLSTM_QK, 0),
                     pl.BlockSpec((L, MLSTM_QK), lambda b, c, pos=pos: (pos(b, c), 0)),
                     cols(MLSTM_V, 0),
                     pl.BlockSpec((L, N_GATES), lambda b, c, pos=pos: (pos(b, c), 0)),
                     cols(N_GATES, 0)]
        args += [qt, k, vot, gc, gr]
        if not reverse:
            in_specs += [cols(MLSTM_V, 1)]
            args += [vot]
    slab = pl.BlockSpec((R, MLSTM_V, L), lambda b, c: (b, 0, chunk(c)))
    if not reverse:
        in_specs += [slab, _resident((MLSTM_V, L))]
        args += [hbt, gain_half]
    return pl.pallas_call(
        functools.partial(_mlstm_kernel, reverse=reverse),
        out_shape=jax.ShapeDtypeStruct((batch, MLSTM_V, seq_len), BF16),
        grid=(batch // R, nc),
        in_specs=in_specs,
        out_specs=slab,
        scratch_shapes=[pltpu.VMEM((R * H, MLSTM_DV, MLSTM_DK), F32),
                        pltpu.VMEM((R * H, MLSTM_RW, L + MLSTM_DK), BF16),
                        pltpu.VMEM((R * H, SUBLANES, MLSTM_DK), F32),
                        pltpu.VMEM((R * H, LANES), F32)],
        compiler_params=_params("arbitrary", "arbitrary"),
        name="mlstm_bwd" if reverse else "mlstm_fwd",
    )(*args)


def _mlstm(qt, k, vot, gc, gr, gain_half, batch, seq_len):
    hbt = _mlstm_dir(qt, k, vot, gc, gr, None, None, batch, seq_len, True)
    return _mlstm_dir(qt, k, vot, gc, gr, hbt, gain_half, batch, seq_len, False)


def _rglru_kernel(*refs, nc, reverse):
    if reverse:
        x_ref, xp_ref, xn_ref, cw_ref, cb_ref, wa_ref, ba_ref, wx_ref, bx_ref, lam_ref, o_ref, \
            a_ref, u_ref, state_ref, k_ref = refs
    else:
        x_ref, xp_ref, xn_ref, hb_ref, cw_ref, cb_ref, wa_ref, ba_ref, wx_ref, bx_ref, lam_ref, \
            o_ref, a_ref, u_ref, state_ref, k_ref, gy_ref = refs
    tc = x_ref.shape[0]
    c = pl.program_id(1)
    cc = nc - 1 - c if reverse else c

    @pl.when(c == 0)
    def _():
        state_ref[...] = jnp.zeros(state_ref.shape, F32)
        k_ref[...] = (-0.5 * LRU_C * LOG2E) * _softplus(-lam_ref[...])

    xr, yr = _unpack_bf16_pair(x_ref[...])
    prev = jnp.where(cc > 0, _unpack_bf16_pair(xp_ref[...])[0], 0.0)
    nxt = jnp.where(cc < nc - 1, _unpack_bf16_pair(xn_ref[...])[0], 0.0)
    xe = jnp.concatenate([prev, xr, nxt], axis=0)
    xc = cb_ref[...]
    for j in range(CONV_W):
        xc = xc + xe[j:j + tc] * cw_ref[j:j + 1, :]

    x2 = xc.reshape(tc * LRU_GROUP, D_RNN)
    xb = x2.astype(BF16)
    for n in range(LRU_BLOCKS):
        sl = slice(n * LRU_BW, (n + 1) * LRU_BW)
        t_a = jnp.tanh(_dot(xb[:, sl], wa_ref[n]) + ba_ref[:, sl])
        t_i = jnp.tanh(_dot(xb[:, sl], wx_ref[n]) + bx_ref[:, sl])
        a = jnp.exp2(k_ref[:, sl] + k_ref[:, sl] * t_a)
        y = 1.0 - a * a
        root = jnp.where(y > 0.0, y * lax.rsqrt(y), 0.0)
        hx = 0.5 * x2[:, sl]
        u = root * (hx + hx * t_i)
        a_ref[:, :, sl] = a.reshape(tc, LRU_GROUP, LRU_BW)
        u_ref[:, :, sl] = u.reshape(tc, LRU_GROUP, LRU_BW)

    if reverse:
        def step(k, h):
            t = tc - 1 - k
            h = a_ref[t] * h + u_ref[t]
            o_ref[t] = _pack_bf16_pair(h[:, :D_RNN // 2], h[:, D_RNN // 2:])
            return h
    else:
        gy_ref[...] = jax.nn.gelu(yr)

        def step(t, h):
            h = a_ref[t] * h + u_ref[t]
            hb = jnp.concatenate(_unpack_bf16_pair(hb_ref[t]), axis=1)
            o_ref[t] = (h + hb) * gy_ref[t]
            return h

    state_ref[...] = lax.fori_loop(0, tc, step, state_ref[...], unroll=LRU_UNROLL)


def _rglru_dir(xy, hb, cw, cb, wa, ba, wx, bx, lam, batch, seq_len, reverse):
    tc, G, C = LRU_TCHUNK, LRU_GROUP, D_RNN
    nc = seq_len // tc
    chunk = (lambda c: nc - 1 - c) if reverse else (lambda c: c)
    blk = lambda width: pl.BlockSpec((tc, G, width), lambda g, c: (chunk(c), g, 0))
    in_specs = [blk(C),
                pl.BlockSpec((1, G, C), lambda g, c: (jnp.maximum(chunk(c) * tc - 1, 0), g, 0)),
                pl.BlockSpec((2, G, C),
                             lambda g, c: (jnp.minimum((chunk(c) + 1) * (tc // 2), seq_len // 2 - 1), g, 0))]
    args = [xy, xy, xy]
    scratch = [pltpu.VMEM((tc, G, C), F32), pltpu.VMEM((tc, G, C), F32),
               pltpu.VMEM((G, C), F32), pltpu.VMEM((1, C), F32)]
    if not reverse:
        in_specs += [blk(C // 2)]
        args += [hb]
        scratch += [pltpu.VMEM((tc, G, C), F32)]
    in_specs += [_resident((CONV_W, C)), _resident((1, C)),
                 _resident((LRU_BLOCKS, LRU_BW, LRU_BW)), _resident((1, C)),
                 _resident((LRU_BLOCKS, LRU_BW, LRU_BW)), _resident((1, C)), _resident((1, C))]
    args += [cw, cb, wa, ba, wx, bx, lam]
    return pl.pallas_call(
        functools.partial(_rglru_kernel, nc=nc, reverse=reverse),
        out_shape=jax.ShapeDtypeStruct((seq_len, batch, C // 2), jnp.uint32) if reverse
        else jax.ShapeDtypeStruct((seq_len, batch, C), F32),
        grid=(batch // G, nc),
        in_specs=in_specs,
        out_specs=blk(C // 2) if reverse else blk(C),
        scratch_shapes=scratch,
        compiler_params=_params("arbitrary", "arbitrary"),
        name="rglru_bwd" if reverse else "rglru_fwd",
    )(*args)


def _rglru(xy, w, batch, seq_len):
    common = (w["cw_r"], w["cb_r"])
    hb = _rglru_dir(xy, None, *common, w["lru_wa"][1], w["lru_ba"][1:2], w["lru_wx"][1], w["lru_bx"][1:2],
                    w["lru_lam"][1:2], batch, seq_len, True)
    return _rglru_dir(xy, hb, *common, w["lru_wa"][0], w["lru_ba"][0:1], w["lru_wx"][0], w["lru_bx"][0:1],
                      w["lru_lam"][0:1], batch, seq_len, False)


def _kv_kernel(m_ref, g_ref, b_ref, w_ref, o_ref):
    m = _layer_norm(m_ref[...], g_ref[...], b_ref[...])
    o_ref[...] = _dot(m.astype(BF16), w_ref[...]).astype(o_ref.dtype)


def _mem_kv(mem, g, b, w_kv):
    R = mem.shape[0]
    return pl.pallas_call(
        _kv_kernel,
        out_shape=jax.ShapeDtypeStruct((R, 2 * D_MODEL), BF16),
        grid=(R // N_MEM,),
        in_specs=[pl.BlockSpec((N_MEM, D_MODEL), lambda i: (i, 0)), _resident((1, D_MODEL)),
                  _resident((1, D_MODEL)), _resident((D_MODEL, 2 * D_MODEL))],
        out_specs=pl.BlockSpec((N_MEM, 2 * D_MODEL), lambda i: (i, 0)),
        compiler_params=_params("arbitrary"),
        name="mem_kv",
    )(mem, g, b, w_kv)


def _merge_xattn_kernel(x_ref, hmt_ref, mg_ref, hr_ref, kv_ref,
                        wpm_ref, wpr_ref, wout_ref, wq_ref, wo_ref, lng_ref, lnb_ref, y_ref):
    half = x_ref.shape[0] // 2
    rows = [slice(0, half), slice(half, 2 * half)]
    pr = [_dot(hr_ref[r, :].astype(BF16), wpr_ref[...]) for r in rows]
    pm = [_dot_tn(hmt_ref[:, r], wpm_ref[...]) for r in rows]
    merged = []
    for r, a, b in zip(rows, pm, pr):
        g_m = _sigmoid(mg_ref[r, :D_MODEL].astype(F32))
        g_r = _sigmoid(mg_ref[r, D_MODEL:].astype(F32))
        merged.append((g_m * a + g_r * b).astype(BF16))
    mix = [_dot(m, wout_ref[...]) for m in merged]
    xs = [_layer_norm(DN_ALPHA * x_ref[r, :] + m, lng_ref[0:1, :], lnb_ref[0:1, :]) for r, m in zip(rows, mix)]

    qs = [_dot(x.astype(BF16), wq_ref[...]).astype(BF16) for x in xs]
    heads = [[], []]
    for h in range(XATTN_HEADS):
        sl = slice(h * XATTN_HD, (h + 1) * XATTN_HD)
        ss = [_dot_nt(q[:, sl], kv_ref[:, sl]) * (XATTN_HD ** -0.5) for q in qs]
        for i, s in enumerate(ss):
            e = jnp.exp(s - jnp.max(s, axis=-1, keepdims=True))
            p = e * (1.0 / jnp.sum(e, axis=-1, keepdims=True))
            oh = _dot(p.astype(BF16), kv_ref[:, D_MODEL + h * XATTN_HD:D_MODEL + (h + 1) * XATTN_HD])
            heads[i].append(oh.astype(BF16))
    for r, x, ohs in zip(rows, xs, heads):
        xa = _dot(jnp.concatenate(ohs, axis=1), wo_ref[...])
        y_ref[r, :] = _layer_norm(DN_ALPHA * x + xa, lng_ref[1:2, :], lnb_ref[1:2, :])


def _merge_xattn(x, hmt, mg, hr, kv, wpm, wpr, wout, wq, wo, lng, lnb, seq_len):
    T = x.shape[0]
    tm = MERGE_TILE
    tiles_per_seq = seq_len // tm
    row = lambda i: (i, 0)
    return pl.pallas_call(
        _merge_xattn_kernel,
        out_shape=jax.ShapeDtypeStruct((T, D_MODEL), F32),
        grid=(T // tm,),
        in_specs=[pl.BlockSpec((tm, D_MODEL), row),
                  pl.BlockSpec((None, MLSTM_V, tm), lambda i: (i // tiles_per_seq, 0, i % tiles_per_seq)),
                  pl.BlockSpec((tm, 2 * D_MODEL), row),
                  pl.BlockSpec((tm, D_RNN), lambda i: (i % tiles_per_seq, i // tiles_per_seq)),
                  pl.BlockSpec((N_MEM, 2 * D_MODEL), lambda i: (i // tiles_per_seq, 0)),
                  _resident((MLSTM_V, D_MODEL)), _resident((D_RNN, D_MODEL)),
                  _resident((D_MODEL, D_MODEL)), _resident((D_MODEL, D_MODEL)),
                  _resident((D_MODEL, D_MODEL)), _resident((2, D_MODEL)), _resident((2, D_MODEL))],
        out_specs=pl.BlockSpec((tm, D_MODEL), row),
        compiler_params=_params("arbitrary"),
        name="merge_xattn",
    )(x, hmt, mg, hr, kv, wpm, wpr, wout, wq, wo, lng, lnb)


def _prep_layer(p, l):
    w_in, b_in = p["w_in"][l], p["b_in"][l]
    cols = lambda lo, hi: w_in[:, lo:hi]
    bias = lambda lo, hi: b_in[lo:hi]
    w_g = cols(OFF_GATE, OFF_XR).astype(BF16)
    b_g = bias(OFF_GATE, OFF_XR)
    return dict(
        ff1_in=p["ff1_in"][l].astype(BF16), ff1_out=p["ff1_out"][l].astype(BF16),
        ff2_in=p["ff2_in"][l].astype(BF16), ff2_out=p["ff2_out"][l].astype(BF16),
        ln_g=p["ln_g"][l], ln_b=p["ln_b"][l],
        w_q=cols(0, MLSTM_QK).astype(BF16), b_q=bias(0, MLSTM_QK)[None],
        cw_q=p["w_conv_qk"][l][:, :MLSTM_QK], cb_q=p["b_conv_qk"][l][None, :MLSTM_QK],
        w_k=cols(MLSTM_QK, OFF_V).astype(BF16), b_k=bias(MLSTM_QK, OFF_V)[None],
        cw_k=p["w_conv_qk"][l][:, MLSTM_QK:], cb_k=p["b_conv_qk"][l][None, MLSTM_QK:],
        w_vot=cols(OFF_V, OFF_GATE).astype(BF16).T, b_vot=bias(OFF_V, OFF_GATE)[:, None],
        w_mg=cols(OFF_MG, D_IN).astype(BF16), b_mg=bias(OFF_MG, D_IN)[None],
        w_xy=cols(OFF_XR, OFF_MG).astype(BF16), b_xy=bias(OFF_XR, OFF_MG)[None],
        w_g=w_g, w_gt=w_g.T, b_g=b_g[None], b_gt=b_g[:, None],
        cw_r=p["w_conv_r"][l], cb_r=p["b_conv_r"][l][None],
        lru_wa=(0.5 * p["lru_wa"][l]).astype(BF16), lru_ba=0.5 * p["lru_ba"][l],
        lru_wx=(0.5 * p["lru_wx"][l]).astype(BF16), lru_bx=0.5 * p["lru_bx"][l], lru_lam=p["lru_lam"][l],
        mh_gain_half=jnp.broadcast_to(0.5 * p["mh_gain"][l][:, None], (MLSTM_V, MLSTM_CHUNK)),
        w_pm=p["w_pm"][l].astype(BF16), w_pr=p["w_pr"][l].astype(BF16), w_out=p["w_out"][l].astype(BF16),
        xa_wq=p["xa_wq"][l].astype(BF16), xa_wkv=p["xa_wkv"][l].astype(BF16), xa_wo=p["xa_wo"][l].astype(BF16),
        mem_ln_g=p["mem_ln_g"][l][None], mem_ln_b=p["mem_ln_b"][l][None],
    )


def _trunk(x, mem, layers):
    B, S, _ = x.shape
    T = B * S
    x = x.reshape(T, D_MODEL)
    mem = mem.reshape(B * N_MEM, D_MODEL)
    for w in layers:
        x, xb = _ffn_ln(x, w["ff1_in"], w["ff1_out"], w["ln_g"][0:1], w["ln_b"][0:1])
        qt, k, vot = _qkvo(xb, w, S)
        mg, xy, gc, gr = _mgxy_gates(xb, w, S)
        hmt = _mlstm(qt, k, vot, gc, gr, w["mh_gain_half"], B, S)
        hr = _rglru(xy.reshape(S, B, D_RNN), w, B, S)
        kv = _mem_kv(mem, w["mem_ln_g"], w["mem_ln_b"], w["xa_wkv"])
        x = _merge_xattn(x, hmt, mg, hr.reshape(S, B * D_RNN), kv, w["w_pm"], w["w_pr"],
                         w["w_out"], w["xa_wq"], w["xa_wo"], w["ln_g"][1:3], w["ln_b"][1:3], S)
        x, _ = _ffn_ln(x, w["ff2_in"], w["ff2_out"], w["ln_g"][3:4], w["ln_b"][3:4])
    return x.reshape(B, S, D_MODEL)


def kernel(x_prompt, x_sample, mem_prompt, mem_sample, w_in, b_in, w_conv_qk, b_conv_qk, w_conv_r, b_conv_r,
           mh_gain, lru_wa, lru_ba, lru_wx, lru_bx, lru_lam, w_pm, w_pr, w_out, xa_wq, xa_wkv, xa_wo,
           mem_ln_g, mem_ln_b, ff1_in, ff1_out, ff2_in, ff2_out, ln_g, ln_b):
    p = dict(w_in=w_in, b_in=b_in, w_conv_qk=w_conv_qk, b_conv_qk=b_conv_qk, w_conv_r=w_conv_r,
             b_conv_r=b_conv_r, mh_gain=mh_gain, lru_wa=lru_wa, lru_ba=lru_ba, lru_wx=lru_wx,
             lru_bx=lru_bx, lru_lam=lru_lam, w_pm=w_pm, w_pr=w_pr, w_out=w_out, xa_wq=xa_wq,
             xa_wkv=xa_wkv, xa_wo=xa_wo, mem_ln_g=mem_ln_g, mem_ln_b=mem_ln_b, ff1_in=ff1_in,
             ff1_out=ff1_out, ff2_in=ff2_in, ff2_out=ff2_out, ln_g=ln_g, ln_b=ln_b)
    layers = [_prep_layer(p, l) for l in range(DEPTH)]
    return (_trunk(x_prompt, mem_prompt, layers), _trunk(x_sample, mem_sample, layers))
```

```python
import functools
import math

import jax
import jax.numpy as jnp
from jax import lax
from jax.experimental import pallas as pl
from jax.experimental.pallas import tpu as pltpu

F32 = jnp.float32
BF16 = jnp.bfloat16

D_MODEL = 1024
DEPTH = 2
N_MEM = 256
MLSTM_HEADS = 4
MLSTM_DK = D_MODEL // 4
MLSTM_DV = D_MODEL // 2
MLSTM_QK = MLSTM_HEADS * MLSTM_DK
MLSTM_V = MLSTM_HEADS * MLSTM_DV
D_RNN = D_MODEL
LRU_BLOCKS = 8
LRU_BW = D_RNN // LRU_BLOCKS
LRU_C = 8.0
CONV_W = 4
XATTN_HEADS = 4
XATTN_HD = D_MODEL // XATTN_HEADS
D_FF = 2816
DN_ALPHA = (2.0 * DEPTH) ** 0.25
LN_EPS = 1e-5

OFF_V = 2 * MLSTM_QK
OFF_O = OFF_V + MLSTM_V
OFF_GATE = OFF_O + MLSTM_V
OFF_XR = OFF_GATE + 4 * MLSTM_HEADS
OFF_YR = OFF_XR + D_RNN
OFF_MG = OFF_YR + D_RNN
D_IN = OFF_MG + 2 * D_MODEL
N_GATES = 4 * MLSTM_HEADS

VMEM_LIMIT_BYTES = 56 * 1024 * 1024
LANES = 128
SUBLANES = 8
BF16_ROWS = 16

TOKEN_TILE = 512
MERGE_TILE = 512
MXU_DEPTH = 256
FF_CHUNKS = ((0, 6 * MXU_DEPTH), (6 * MXU_DEPTH, D_FF))
MLSTM_CHUNK = 256
MLSTM_ROWS = 2
LRU_TCHUNK = 128
LRU_GROUP = SUBLANES
LRU_UNROLL = 8
CONV_HALO = BF16_ROWS
QKVO_ROWS = 1024
MLSTM_RW = MLSTM_DV + LANES
LOG2E = math.log2(math.e)


def _dot(a, b):
    return jnp.dot(a, b, preferred_element_type=F32)


def _dot_nt(a, b):
    return lax.dot_general(a, b, (((1,), (1,)), ((), ())), preferred_element_type=F32)


def _dot_tn(a, b):
    return lax.dot_general(a, b, (((0,), (0,)), ((), ())), preferred_element_type=F32)


def _layer_norm(y, g, b):
    mu = jnp.mean(y, axis=-1, keepdims=True)
    d = y - mu
    var = jnp.mean(d * d, axis=-1, keepdims=True)
    return d * lax.rsqrt(var + LN_EPS) * g + b


def _sigmoid(x):
    return 0.5 + 0.5 * jnp.tanh(0.5 * x)


def _silu(x):
    h = 0.5 * x
    return h + h * jnp.tanh(h)


def _log_sigmoid(x):
    return jnp.minimum(x, 0.0) - jnp.log1p(jnp.exp(-jnp.abs(x)))


def _softplus(x):
    return jnp.maximum(x, 0.0) + jnp.log1p(jnp.exp(-jnp.abs(x)))


def _pack_bf16_pair(hi, lo):
    hi_bits = lax.bitcast_convert_type(hi.astype(BF16).astype(F32), jnp.uint32)
    lo_bits = lax.bitcast_convert_type(lo.astype(BF16).astype(F32), jnp.uint32)
    return hi_bits | (lo_bits >> 16)


def _unpack_bf16_pair(w):
    hi = lax.bitcast_convert_type(w & jnp.uint32(0xFFFF0000), F32)
    lo = lax.bitcast_convert_type(w << 16, F32)
    return hi, lo


def _split3(x):
    hi = x.astype(BF16)
    r = x - hi.astype(F32)
    mid = r.astype(BF16)
    lo = (r - mid.astype(F32)).astype(BF16)
    return hi, mid, lo


def _resident(shape):
    zeros = (0,) * len(shape)
    return pl.BlockSpec(shape, lambda *_: zeros, pipeline_mode=pl.Buffered(1))


def _params(*sem):
    return pltpu.CompilerParams(dimension_semantics=sem, vmem_limit_bytes=VMEM_LIMIT_BYTES)


def _ffn_ln_kernel(x_ref, win_ref, wout_ref, g_ref, b_ref, y_ref, yb_ref):
    half = x_ref.shape[0] // 2
    rows = [slice(0, half), slice(half, 2 * half)]
    ups = []
    for r in rows:
        xb = x_ref[r, :].astype(BF16)
        ups.append(([_dot(xb, win_ref[:, lo:hi]) for lo, hi in FF_CHUNKS],
                    [_dot(xb, win_ref[:, D_FF + lo:D_FF + hi]) for lo, hi in FF_CHUNKS]))
    for r, (gates, vals) in zip(rows, ups):
        acc = jnp.zeros((half, D_MODEL), F32)
        for (lo, hi), gate, up in zip(FF_CHUNKS, gates, vals):
            acc = acc + _dot((_silu(gate) * up).astype(BF16), wout_ref[lo:hi, :])
        y = _layer_norm(DN_ALPHA * x_ref[r, :] + 0.5 * acc, g_ref[...], b_ref[...])
        y_ref[r, :] = y
        yb_ref[r, :] = y.astype(BF16)


def _ffn_ln(x, w_in, w_out, g, b):
    T = x.shape[0]
    tm = TOKEN_TILE
    row = lambda i: (i, 0)
    return pl.pallas_call(
        _ffn_ln_kernel,
        out_shape=(jax.ShapeDtypeStruct((T, D_MODEL), F32), jax.ShapeDtypeStruct((T, D_MODEL), BF16)),
        grid=(T // tm,),
        in_specs=[pl.BlockSpec((tm, D_MODEL), row), _resident((D_MODEL, 2 * D_FF)),
                  _resident((D_FF, D_MODEL)), _resident((1, D_MODEL)), _resident((1, D_MODEL))],
        out_specs=(pl.BlockSpec((tm, D_MODEL), row), pl.BlockSpec((tm, D_MODEL), row)),
        compiler_params=_params("arbitrary"),
        name="ffn_ln",
    )(x, w_in, w_out, g, b)


def _qkvo_kernel(x_ref, xp_ref, xn_ref, wq_ref, bq_ref, cwq_ref, cbq_ref, wk_ref, bk_ref, cwk_ref, cbk_ref,
                 wvot_ref, bvot_ref, qt_ref, k_ref, vot_ref, xe_ref, xt_ref, *, tiles_per_seq):
    tm = x_ref.shape[0]
    H = CONV_HALO
    xt_ref[...] = x_ref[...].astype(F32).T.astype(BF16)
    xe_ref[0:H, :] = xp_ref[...]
    xe_ref[H:H + tm, :] = x_ref[...]
    xe_ref[H + tm:, :] = xn_ref[...]
    pos = pl.program_id(0) % tiles_per_seq
    first = pos == 0
    last = pos == tiles_per_seq - 1

    rows_ext = tm + 2 * H
    halo_row = lax.broadcasted_iota(jnp.int32, (rows_ext, 1), 0)
    outside = ((halo_row < H) & first) | ((halo_row >= H + tm) & last)

    def conv_silu(w_ref, b_ref, cw_ref, cb_ref):
        u = jnp.where(outside, 0.0, _dot(xe_ref[...], w_ref[...]) + b_ref[...])
        out = cb_ref[...] + u[H:H + tm, :] * cw_ref[1:2, :]
        for j, shift in ((0, 1), (2, rows_ext - 1), (3, rows_ext - 2)):
            out = out + pltpu.roll(u, shift, axis=0)[H:H + tm, :] * cw_ref[j:j + 1, :]
        return _silu(out)

    qt_ref[...] = (conv_silu(wq_ref, bq_ref, cwq_ref, cbq_ref) * (MLSTM_DK ** -0.5)).T.astype(qt_ref.dtype)
    k_ref[...] = conv_silu(wk_ref, bk_ref, cwk_ref, cbk_ref).astype(k_ref.dtype)
    for i in range(vot_ref.shape[0] // QKVO_ROWS):
        rows = slice(i * QKVO_ROWS, (i + 1) * QKVO_ROWS)
        vot_ref[rows, :] = (_dot(wvot_ref[rows, :], xt_ref[...]) + bvot_ref[rows, :]).astype(vot_ref.dtype)


def _qkvo(xb, w, seq_len):
    T = xb.shape[0]
    tm, H = TOKEN_TILE, CONV_HALO
    tiles_per_seq = seq_len // tm
    hpt = tm // H
    last_halo = T // H - 1
    conv_w = lambda: [_resident((D_MODEL, MLSTM_QK)), _resident((1, MLSTM_QK)),
                      _resident((CONV_W, MLSTM_QK)), _resident((1, MLSTM_QK))]
    return pl.pallas_call(
        functools.partial(_qkvo_kernel, tiles_per_seq=tiles_per_seq),
        out_shape=(jax.ShapeDtypeStruct((MLSTM_QK, T), BF16), jax.ShapeDtypeStruct((T, MLSTM_QK), BF16),
                   jax.ShapeDtypeStruct((2 * MLSTM_V, T), BF16)),
        grid=(T // tm,),
        in_specs=[pl.BlockSpec((tm, D_MODEL), lambda i: (i, 0)),
                  pl.BlockSpec((H, D_MODEL), lambda i: (jnp.maximum(i * hpt - 1, 0), 0)),
                  pl.BlockSpec((H, D_MODEL), lambda i: (jnp.minimum((i + 1) * hpt, last_halo), 0))]
        + conv_w() + conv_w() + [_resident((2 * MLSTM_V, D_MODEL)), _resident((2 * MLSTM_V, 1))],
        out_specs=(pl.BlockSpec((MLSTM_QK, tm), lambda i: (0, i)), pl.BlockSpec((tm, MLSTM_QK), lambda i: (i, 0)),
                   pl.BlockSpec((2 * MLSTM_V, tm), lambda i: (0, i))),
        scratch_shapes=[pltpu.VMEM((tm + 2 * H, D_MODEL), BF16), pltpu.VMEM((D_MODEL, tm), BF16)],
        compiler_params=_params("arbitrary"),
        name="qkvo",
    )(xb, xb, xb, w["w_q"], w["b_q"], w["cw_q"], w["cb_q"], w["w_k"], w["b_k"], w["cw_k"], w["cb_k"],
      w["w_vot"], w["b_vot"])


def _mgxy_gates_kernel(x_ref, wmg_ref, bmg_ref, wxy_ref, bxy_ref, w_ref, b_ref, mg_ref, xy_ref, gc_ref, gr_ref):
    L = MLSTM_CHUNK
    H = MLSTM_HEADS
    x = x_ref[...]
    g = _dot(x, w_ref[...]) + b_ref[...]
    ls = _log_sigmoid(g)
    pieces = [_split3(ls[k * L:(k + 1) * L, :]) for k in range(x.shape[0] // L)]
    mg_ref[...] = (_dot(x, wmg_ref[...]) + bmg_ref[...]).astype(mg_ref.dtype)
    row = lax.broadcasted_iota(jnp.int32, (L, L), 0)
    col = lax.broadcasted_iota(jnp.int32, (L, L), 1)
    tril = (col <= row).astype(BF16)
    pres = [sum(_dot(tril, p) for p in ps) for ps in pieces]
    xy = _dot(x, wxy_ref[...]) + bxy_ref[...]
    c_idx = lax.broadcasted_iota(jnp.int32, (L, N_GATES), 1)
    for k, pre in enumerate(pres):
        sl = slice(k * L, (k + 1) * L)
        suf = pre[L - 1:L, :] - pre + ls[sl, :]
        cum = jnp.where(c_idx >= 2 * H, suf, pre)
        gc = jnp.where(c_idx % (2 * H) >= H, cum, g[sl, :])
        gc_ref[sl, :] = gc
        wide = jnp.concatenate([gc, jnp.zeros((L, LANES - N_GATES), F32)], axis=1)
        gr_ref[:, sl] = wide.T[:N_GATES, :]
    xy_ref[...] = _pack_bf16_pair(xy[:, :D_RNN], xy[:, D_RNN:])


def _mgxy_gates(xb, w, seq_len):
    T = xb.shape[0]
    tm = TOKEN_TILE
    tiles_per_seq = seq_len // tm
    row = lambda i: (i, 0)
    return pl.pallas_call(
        _mgxy_gates_kernel,
        out_shape=(jax.ShapeDtypeStruct((T, 2 * D_MODEL), BF16),
                   jax.ShapeDtypeStruct((seq_len, (T // seq_len) * D_RNN), jnp.uint32),
                   jax.ShapeDtypeStruct((T, N_GATES), F32), jax.ShapeDtypeStruct((N_GATES, T), F32)),
        grid=(T // tm,),
        in_specs=[pl.BlockSpec((tm, D_MODEL), row),
                  _resident((D_MODEL, 2 * D_MODEL)), _resident((1, 2 * D_MODEL)),
                  _resident((D_MODEL, 2 * D_RNN)), _resident((1, 2 * D_RNN)),
                  _resident((D_MODEL, N_GATES)), _resident((1, N_GATES))],
        out_specs=(pl.BlockSpec((tm, 2 * D_MODEL), row),
                   pl.BlockSpec((tm, D_RNN), lambda i: (i % tiles_per_seq, i // tiles_per_seq)),
                   pl.BlockSpec((tm, N_GATES), row), pl.BlockSpec((N_GATES, tm), lambda i: (0, i))),
        compiler_params=_params("arbitrary"),
        name="mgxy_gates",
    )(xb, w["w_mg"], w["b_mg"], w["w_xy"], w["b_xy"], w["w_g"], w["b_g"])


def _mlstm_kernel(*refs, reverse):
    R = MLSTM_ROWS
    per_row = 5 if reverse else 6
    row_refs = [refs[r * per_row:(r + 1) * per_row] for r in range(R)]
    rest = refs[R * per_row:]
    if reverse:
        out_ref, ct_ref, rt_ref, n_ref, m_ref = rest
    else:
        hbt_ref, gain_ref, out_ref, ct_ref, rt_ref, n_ref, m_ref = rest
    L = row_refs[0][1].shape[0]
    H, DK, DV = MLSTM_HEADS, MLSTM_DK, MLSTM_DV
    base = 2 * H if reverse else 0

    @pl.when(pl.program_id(1) == 0)
    def _():
        ct_ref[...] = jnp.zeros(ct_ref.shape, F32)
        n_ref[...] = jnp.zeros(n_ref.shape, F32)
        m_ref[...] = jnp.zeros(m_ref.shape, F32)

    key = lax.broadcasted_iota(jnp.int32, (L, L), 0)
    qry = lax.broadcasted_iota(jnp.int32, (L, L), 1)
    mask = (key >= qry) if reverse else (key <= qry)
    last = 0 if reverse else L - 1

    for r in range(R):
        qt_ref, k_ref, vt_ref, gc_ref, gr_ref = row_refs[r][:5]
        gc = gc_ref[:, base:base + 2 * H] * LOG2E
        gr = gr_ref[base:base + 2 * H, :] * LOG2E
        g_rows = [gr[H + h:H + h + 1, :] for h in range(H)]
        b_rows = [gr[h:h + 1, :] - gr[H + h:H + h + 1, :] for h in range(H)]
        b_cols = [gc[:, h:h + 1] - gc[:, H + h:H + h + 1] for h in range(H)]
        ms = [m_ref[r * H + h:r * H + h + 1, 0:1] for h in range(H)]
        qts = [qt_ref[h * DK:(h + 1) * DK, :] for h in range(H)]
        ks = [k_ref[:, h * DK:(h + 1) * DK] for h in range(H)]
        vts = [vt_ref[h * DV:(h + 1) * DV, :] for h in range(H)]

        sts = [_dot(ks[h], qts[h]) for h in range(H)]

        for h in range(H):
            u = r * H + h
            sl = slice(h * DV, (h + 1) * DV)
            log_d = jnp.where(mask, g_rows[h] + b_cols[h], -jnp.inf)
            log_inter = g_rows[h] + ms[h]
            m_t = jnp.maximum(log_inter, jnp.max(log_d, axis=0, keepdims=True))
            s = sts[h] * jnp.exp2(log_d - m_t)
            w_inter = jnp.exp2(log_inter - m_t)
            lhs = jnp.concatenate([s.astype(BF16), qts[h] * w_inter.astype(BF16)], axis=0)
            rt_ref[u, :DV, :L] = vts[h]
            rt_ref[u, DV:, :L] = jnp.ones((LANES, L), BF16)
            rt_ref[u, :DV, L:] = ct_ref[u].astype(BF16)
            rt_ref[u, DV:, L:] = jnp.broadcast_to(n_ref[u][0:1, :].astype(BF16), (LANES, DK))
            res = _dot(rt_ref[u], lhs)
            den = res[DV:DV + 1, :]
            hh = res[:DV, :] * (1.0 / jnp.maximum(jnp.abs(den), jnp.exp2(-m_t)))
            if reverse:
                out_ref[r, sl, :] = hh.astype(out_ref.dtype)
            else:
                ot_ref = row_refs[r][5]
                hh = hh + hbt_ref[r, sl, :].astype(F32)
                mu = jnp.mean(hh, axis=0, keepdims=True)
                d = hh - mu
                var = jnp.mean(d * d, axis=0, keepdims=True)
                hn = d * lax.rsqrt(var + LN_EPS) * gain_ref[sl, :]
                out_ref[r, sl, :] = (hn + hn * jnp.tanh(0.5 * ot_ref[sl, :].astype(F32))).astype(out_ref.dtype)

        for h in range(H):
            u = r * H + h
            g_tot = g_rows[h][:, last:last + 1]
            m_new = jnp.maximum(g_tot + ms[h], jnp.max(g_tot + b_rows[h], axis=-1, keepdims=True))
            wk = ks[h].astype(F32) * jnp.exp2(g_tot + b_cols[h] - m_new)
            decay = jnp.exp2(g_tot + ms[h] - m_new)
            ct_ref[u] = decay * ct_ref[u] + _dot(vts[h], wk.astype(BF16))
            n_ref[u] = decay * n_ref[u] + jnp.sum(wk, axis=0, keepdims=True)
            m_ref[u:u + 1, :] = jnp.broadcast_to(m_new, (1, m_ref.shape[1]))


def _mlstm_dir(qt, k, vot, gc, gr, hbt, gain_half, batch, seq_len, reverse):
    L, R, H = MLSTM_CHUNK, MLSTM_ROWS, MLSTM_HEADS
    nc = seq_len // L
    chunk = (lambda c: nc - 1 - c) if reverse else (lambda c: c)
    in_specs, args = [], []
    for r in range(R):
        pos = lambda b, c, r=r: (b * R + r) * nc + chunk(c)
        cols = lambda height, blk, pos=pos: pl.BlockSpec((height, L), lambda b, c: (blk, pos(b, c)))
        in_specs += [cols(MLSTM_QK, 0),
                     pl.BlockSpec((L, MLSTM_QK), lambda b, c, pos=pos: (pos(b, c), 0)),
                     cols(MLSTM_V, 0),
                     pl.BlockSpec((L, N_GATES), lambda b, c, pos=pos: (pos(b, c), 0)),
                     cols(N_GATES, 0)]
        args += [qt, k, vot, gc, gr]
        if not reverse:
            in_specs += [cols(MLSTM_V, 1)]
            args += [vot]
    slab = pl.BlockSpec((R, MLSTM_V, L), lambda b, c: (b, 0, chunk(c)))
    if not reverse:
        in_specs += [slab, _resident((MLSTM_V, L))]
        args += [hbt, gain_half]
    return pl.pallas_call(
        functools.partial(_mlstm_kernel, reverse=reverse),
        out_shape=jax.ShapeDtypeStruct((batch, MLSTM_V, seq_len), BF16),
        grid=(batch // R, nc),
        in_specs=in_specs,
        out_specs=slab,
        scratch_shapes=[pltpu.VMEM((R * H, MLSTM_DV, MLSTM_DK), F32),
                        pltpu.VMEM((R * H, MLSTM_RW, L + MLSTM_DK), BF16),
                        pltpu.VMEM((R * H, SUBLANES, MLSTM_DK), F32),
                        pltpu.VMEM((R * H, LANES), F32)],
        compiler_params=_params("arbitrary", "arbitrary"),
        name="mlstm_bwd" if reverse else "mlstm_fwd",
    )(*args)


def _mlstm(qt, k, vot, gc, gr, gain_half, batch, seq_len):
    hbt = _mlstm_dir(qt, k, vot, gc, gr, None, None, batch, seq_len, True)
    return _mlstm_dir(qt, k, vot, gc, gr, hbt, gain_half, batch, seq_len, False)


def _rglru_kernel(*refs, nc, reverse):
    if reverse:
        x_ref, xp_ref, xn_ref, cw_ref, cb_ref, wa_ref, ba_ref, wx_ref, bx_ref, lam_ref, o_ref, \
            a_ref, u_ref, state_ref, k_ref = refs
    else:
        x_ref, xp_ref, xn_ref, hb_ref, cw_ref, cb_ref, wa_ref, ba_ref, wx_ref, bx_ref, lam_ref, \
            o_ref, a_ref, u_ref, state_ref, k_ref, gy_ref = refs
    tc = x_ref.shape[0]
    c = pl.program_id(1)
    cc = nc - 1 - c if reverse else c

    @pl.when(c == 0)
    def _():
        state_ref[...] = jnp.zeros(state_ref.shape, F32)
        k_ref[...] = (-0.5 * LRU_C * LOG2E) * _softplus(-lam_ref[...])

    xr, yr = _unpack_bf16_pair(x_ref[...])
    prev = jnp.where(cc > 0, _unpack_bf16_pair(xp_ref[...])[0], 0.0)
    nxt = jnp.where(cc < nc - 1, _unpack_bf16_pair(xn_ref[...])[0], 0.0)
    xe = jnp.concatenate([prev, xr, nxt], axis=0)
    xc = cb_ref[...]
    for j in range(CONV_W):
        xc = xc + xe[j:j + tc] * cw_ref[j:j + 1, :]

    x2 = xc.reshape(tc * LRU_GROUP, D_RNN)
    xb = x2.astype(BF16)
    for n in range(LRU_BLOCKS):
        sl = slice(n * LRU_BW, (n + 1) * LRU_BW)
        t_a = jnp.tanh(_dot(xb[:, sl], wa_ref[n]) + ba_ref[:, sl])
        t_i = jnp.tanh(_dot(xb[:, sl], wx_ref[n]) + bx_ref[:, sl])
        a = jnp.exp2(k_ref[:, sl] + k_ref[:, sl] * t_a)
        y = 1.0 - a * a
        root = jnp.where(y > 0.0, y * lax.rsqrt(y), 0.0)
        hx = 0.5 * x2[:, sl]
        u = root * (hx + hx * t_i)
        a_ref[:, :, sl] = a.reshape(tc, LRU_GROUP, LRU_BW)
        u_ref[:, :, sl] = u.reshape(tc, LRU_GROUP, LRU_BW)

    if reverse:
        def step(k, h):
            t = tc - 1 - k
            h = a_ref[t] * h + u_ref[t]
            o_ref[t] = _pack_bf16_pair(h[:, :D_RNN // 2], h[:, D_RNN // 2:])
            return h
    else:
        gy_ref[...] = jax.nn.gelu(yr)

        def step(t, h):
            h = a_ref[t] * h + u_ref[t]
            hb = jnp.concatenate(_unpack_bf16_pair(hb_ref[t]), axis=1)
            o = (h + hb) * gy_ref[t]
            o_ref[t] = _pack_bf16_pair(o[:, :D_RNN // 2], o[:, D_RNN // 2:])
            return h

    state_ref[...] = lax.fori_loop(0, tc, step, state_ref[...], unroll=LRU_UNROLL)


def _rglru_dir(xy, hb, cw, cb, wa, ba, wx, bx, lam, batch, seq_len, reverse):
    tc, G, C = LRU_TCHUNK, LRU_GROUP, D_RNN
    nc = seq_len // tc
    chunk = (lambda c: nc - 1 - c) if reverse else (lambda c: c)
    blk = lambda width: pl.BlockSpec((tc, G, width), lambda g, c: (chunk(c), g, 0))
    in_specs = [blk(C),
                pl.BlockSpec((1, G, C), lambda g, c: (jnp.maximum(chunk(c) * tc - 1, 0), g, 0)),
                pl.BlockSpec((2, G, C),
                             lambda g, c: (jnp.minimum((chunk(c) + 1) * (tc // 2), seq_len // 2 - 1), g, 0))]
    args = [xy, xy, xy]
    scratch = [pltpu.VMEM((tc, G, C), F32), pltpu.VMEM((tc, G, C), F32),
               pltpu.VMEM((G, C), F32), pltpu.VMEM((1, C), F32)]
    if not reverse:
        in_specs += [blk(C // 2)]
        args += [hb]
        scratch += [pltpu.VMEM((tc, G, C), F32)]
    in_specs += [_resident((CONV_W, C)), _resident((1, C)),
                 _resident((LRU_BLOCKS, LRU_BW, LRU_BW)), _resident((1, C)),
                 _resident((LRU_BLOCKS, LRU_BW, LRU_BW)), _resident((1, C)), _resident((1, C))]
    args += [cw, cb, wa, ba, wx, bx, lam]
    return pl.pallas_call(
        functools.partial(_rglru_kernel, nc=nc, reverse=reverse),
        out_shape=jax.ShapeDtypeStruct((seq_len, batch, C // 2), jnp.uint32),
        grid=(batch // G, nc),
        in_specs=in_specs,
        out_specs=blk(C // 2),
        scratch_shapes=scratch,
        compiler_params=_params("arbitrary", "arbitrary"),
        name="rglru_bwd" if reverse else "rglru_fwd",
    )(*args)


def _rglru(xy, w, batch, seq_len):
    common = (w["cw_r"], w["cb_r"])
    hb = _rglru_dir(xy, None, *common, w["lru_wa"][1], w["lru_ba"][1:2], w["lru_wx"][1], w["lru_bx"][1:2],
                    w["lru_lam"][1:2], batch, seq_len, True)
    return _rglru_dir(xy, hb, *common, w["lru_wa"][0], w["lru_ba"][0:1], w["lru_wx"][0], w["lru_bx"][0:1],
                      w["lru_lam"][0:1], batch, seq_len, False)


def _kv_kernel(m_ref, g_ref, b_ref, w_ref, o_ref):
    m = _layer_norm(m_ref[...], g_ref[...], b_ref[...])
    o_ref[...] = _dot(m.astype(BF16), w_ref[...]).astype(o_ref.dtype)


def _mem_kv(mem, g, b, w_kv):
    R = mem.shape[0]
    return pl.pallas_call(
        _kv_kernel,
        out_shape=jax.ShapeDtypeStruct((R, 2 * D_MODEL), BF16),
        grid=(R // N_MEM,),
        in_specs=[pl.BlockSpec((N_MEM, D_MODEL), lambda i: (i, 0)), _resident((1, D_MODEL)),
                  _resident((1, D_MODEL)), _resident((D_MODEL, 2 * D_MODEL))],
        out_specs=pl.BlockSpec((N_MEM, 2 * D_MODEL), lambda i: (i, 0)),
        compiler_params=_params("arbitrary"),
        name="mem_kv",
    )(mem, g, b, w_kv)


def _merge_xattn_kernel(x_ref, hmt_ref, mg_ref, hr_ref, kv_ref,
                        wpm_ref, wpr_ref, wout_ref, wq_ref, wo_ref, lng_ref, lnb_ref, y_ref):
    half = x_ref.shape[0] // 2
    rows = [slice(0, half), slice(half, 2 * half)]
    hr = [jnp.concatenate(_unpack_bf16_pair(hr_ref[r, :]), axis=1).astype(BF16) for r in rows]
    pr = [_dot(h, wpr_ref[...]) for h in hr]
    pm = [_dot_tn(hmt_ref[:, r], wpm_ref[...]) for r in rows]
    merged = []
    for r, a, b in zip(rows, pm, pr):
        g_m = _sigmoid(mg_ref[r, :D_MODEL].astype(F32))
        g_r = _sigmoid(mg_ref[r, D_MODEL:].astype(F32))
        merged.append((g_m * a + g_r * b).astype(BF16))
    mix = [_dot(m, wout_ref[...]) for m in merged]
    xs = [_layer_norm(DN_ALPHA * x_ref[r, :] + m, lng_ref[0:1, :], lnb_ref[0:1, :]) for r, m in zip(rows, mix)]

    qs = [_dot(x.astype(BF16), wq_ref[...]).astype(BF16) for x in xs]
    heads = [[], []]
    for h in range(XATTN_HEADS):
        sl = slice(h * XATTN_HD, (h + 1) * XATTN_HD)
        ss = [_dot_nt(q[:, sl], kv_ref[:, sl]) * (XATTN_HD ** -0.5) for q in qs]
        for i, s in enumerate(ss):
            e = jnp.exp(s - jnp.max(s, axis=-1, keepdims=True))
            p = e * (1.0 / jnp.sum(e, axis=-1, keepdims=True))
            oh = _dot(p.astype(BF16), kv_ref[:, D_MODEL + h * XATTN_HD:D_MODEL + (h + 1) * XATTN_HD])
            heads[i].append(oh.astype(BF16))
    for r, x, ohs in zip(rows, xs, heads):
        xa = _dot(jnp.concatenate(ohs, axis=1), wo_ref[...])
        y_ref[r, :] = _layer_norm(DN_ALPHA * x + xa, lng_ref[1:2, :], lnb_ref[1:2, :])


def _merge_xattn(x, hmt, mg, hr, kv, wpm, wpr, wout, wq, wo, lng, lnb, seq_len):
    T = x.shape[0]
    tm = MERGE_TILE
    tiles_per_seq = seq_len // tm
    row = lambda i: (i, 0)
    return pl.pallas_call(
        _merge_xattn_kernel,
        out_shape=jax.ShapeDtypeStruct((T, D_MODEL), F32),
        grid=(T // tm,),
        in_specs=[pl.BlockSpec((tm, D_MODEL), row),
                  pl.BlockSpec((None, MLSTM_V, tm), lambda i: (i // tiles_per_seq, 0, i % tiles_per_seq)),
                  pl.BlockSpec((tm, 2 * D_MODEL), row),
                  pl.BlockSpec((tm, D_RNN // 2), lambda i: (i % tiles_per_seq, i // tiles_per_seq)),
                  pl.BlockSpec((N_MEM, 2 * D_MODEL), lambda i: (i // tiles_per_seq, 0)),
                  _resident((MLSTM_V, D_MODEL)), _resident((D_RNN, D_MODEL)),
                  _resident((D_MODEL, D_MODEL)), _resident((D_MODEL, D_MODEL)),
                  _resident((D_MODEL, D_MODEL)), _resident((2, D_MODEL)), _resident((2, D_MODEL))],
        out_specs=pl.BlockSpec((tm, D_MODEL), row),
        compiler_params=_params("arbitrary"),
        name="merge_xattn",
    )(x, hmt, mg, hr, kv, wpm, wpr, wout, wq, wo, lng, lnb)


def _prep_layer(p, l):
    w_in, b_in = p["w_in"][l], p["b_in"][l]
    cols = lambda lo, hi: w_in[:, lo:hi]
    bias = lambda lo, hi: b_in[lo:hi]
    return dict(
        ff1_in=p["ff1_in"][l].astype(BF16), ff1_out=p["ff1_out"][l].astype(BF16),
        ff2_in=p["ff2_in"][l].astype(BF16), ff2_out=p["ff2_out"][l].astype(BF16),
        ln_g=p["ln_g"][l], ln_b=p["ln_b"][l],
        w_q=cols(0, MLSTM_QK).astype(BF16), b_q=bias(0, MLSTM_QK)[None],
        cw_q=p["w_conv_qk"][l][:, :MLSTM_QK], cb_q=p["b_conv_qk"][l][None, :MLSTM_QK],
        w_k=cols(MLSTM_QK, OFF_V).astype(BF16), b_k=bias(MLSTM_QK, OFF_V)[None],
        cw_k=p["w_conv_qk"][l][:, MLSTM_QK:], cb_k=p["b_conv_qk"][l][None, MLSTM_QK:],
        w_vot=cols(OFF_V, OFF_GATE).astype(BF16).T, b_vot=bias(OFF_V, OFF_GATE)[:, None],
        w_mg=cols(OFF_MG, D_IN).astype(BF16), b_mg=bias(OFF_MG, D_IN)[None],
        w_xy=cols(OFF_XR, OFF_MG).astype(BF16), b_xy=bias(OFF_XR, OFF_MG)[None],
        w_g=cols(OFF_GATE, OFF_XR).astype(BF16), b_g=bias(OFF_GATE, OFF_XR)[None],
        cw_r=p["w_conv_r"][l], cb_r=p["b_conv_r"][l][None],
        lru_wa=(0.5 * p["lru_wa"][l]).astype(BF16), lru_ba=0.5 * p["lru_ba"][l],
        lru_wx=(0.5 * p["lru_wx"][l]).astype(BF16), lru_bx=0.5 * p["lru_bx"][l], lru_lam=p["lru_lam"][l],
        mh_gain_half=jnp.broadcast_to(0.5 * p["mh_gain"][l][:, None], (MLSTM_V, MLSTM_CHUNK)),
        w_pm=p["w_pm"][l].astype(BF16), w_pr=p["w_pr"][l].astype(BF16), w_out=p["w_out"][l].astype(BF16),
        xa_wq=p["xa_wq"][l].astype(BF16), xa_wkv=p["xa_wkv"][l].astype(BF16), xa_wo=p["xa_wo"][l].astype(BF16),
        mem_ln_g=p["mem_ln_g"][l][None], mem_ln_b=p["mem_ln_b"][l][None],
    )


def _trunk(x, mem, layers):
    B, S, _ = x.shape
    T = B * S
    x = x.reshape(T, D_MODEL)
    mem = mem.reshape(B * N_MEM, D_MODEL)
    for w in layers:
        x, xb = _ffn_ln(x, w["ff1_in"], w["ff1_out"], w["ln_g"][0:1], w["ln_b"][0:1])
        qt, k, vot = _qkvo(xb, w, S)
        mg, xy, gc, gr = _mgxy_gates(xb, w, S)
        hmt = _mlstm(qt, k, vot, gc, gr, w["mh_gain_half"], B, S)
        hr = _rglru(xy.reshape(S, B, D_RNN), w, B, S)
        kv = _mem_kv(mem, w["mem_ln_g"], w["mem_ln_b"], w["xa_wkv"])
        x = _merge_xattn(x, hmt, mg, hr.reshape(S, B * (D_RNN // 2)), kv, w["w_pm"], w["w_pr"],
                         w["w_out"], w["xa_wq"], w["xa_wo"], w["ln_g"][1:3], w["ln_b"][1:3], S)
        x, _ = _ffn_ln(x, w["ff2_in"], w["ff2_out"], w["ln_g"][3:4], w["ln_b"][3:4])
    return x.reshape(B, S, D_MODEL)


def kernel(x_prompt, x_sample, mem_prompt, mem_sample, w_in, b_in, w_conv_qk, b_conv_qk, w_conv_r, b_conv_r,
           mh_gain, lru_wa, lru_ba, lru_wx, lru_bx, lru_lam, w_pm, w_pr, w_out, xa_wq, xa_wkv, xa_wo,
           mem_ln_g, mem_ln_b, ff1_in, ff1_out, ff2_in, ff2_out, ln_g, ln_b):
    p = dict(w_in=w_in, b_in=b_in, w_conv_qk=w_conv_qk, b_conv_qk=b_conv_qk, w_conv_r=w_conv_r,
             b_conv_r=b_conv_r, mh_gain=mh_gain, lru_wa=lru_wa, lru_ba=lru_ba, lru_wx=lru_wx,
             lru_bx=lru_bx, lru_lam=lru_lam, w_pm=w_pm, w_pr=w_pr, w_out=w_out, xa_wq=xa_wq,
             xa_wkv=xa_wkv, xa_wo=xa_wo, mem_ln_g=mem_ln_g, mem_ln_b=mem_ln_b, ff1_in=ff1_in,
             ff1_out=ff1_out, ff2_in=ff2_in, ff2_out=ff2_out, ln_g=ln_g, ln_b=ln_b)
    layers = [_prep_layer(p, l) for l in range(DEPTH)]
    return (_trunk(x_prompt, mem_prompt, layers), _trunk(x_sample, mem_sample, layers))
```

```python
import functools
import math

import jax
import jax.numpy as jnp
from jax import lax
from jax.experimental import pallas as pl
from jax.experimental.pallas import tpu as pltpu

F32 = jnp.float32
BF16 = jnp.bfloat16

D_MODEL = 1024
DEPTH = 2
N_MEM = 256
MLSTM_HEADS = 4
MLSTM_DK = D_MODEL // 4
MLSTM_DV = D_MODEL // 2
MLSTM_QK = MLSTM_HEADS * MLSTM_DK
MLSTM_V = MLSTM_HEADS * MLSTM_DV
D_RNN = D_MODEL
LRU_BLOCKS = 8
LRU_BW = D_RNN // LRU_BLOCKS
LRU_C = 8.0
CONV_W = 4
XATTN_HEADS = 4
XATTN_HD = D_MODEL // XATTN_HEADS
D_FF = 2816
DN_ALPHA = (2.0 * DEPTH) ** 0.25
LN_EPS = 1e-5

OFF_V = 2 * MLSTM_QK
OFF_O = OFF_V + MLSTM_V
OFF_GATE = OFF_O + MLSTM_V
OFF_XR = OFF_GATE + 4 * MLSTM_HEADS
OFF_YR = OFF_XR + D_RNN
OFF_MG = OFF_YR + D_RNN
D_IN = OFF_MG + 2 * D_MODEL
N_GATES = 4 * MLSTM_HEADS

VMEM_LIMIT_BYTES = 56 * 1024 * 1024
LANES = 128
SUBLANES = 8
BF16_ROWS = 16

TOKEN_TILE = 512
MERGE_TILE = 512
MXU_DEPTH = 256
FF_CHUNKS = ((0, 6 * MXU_DEPTH), (6 * MXU_DEPTH, D_FF))
MLSTM_CHUNK = 256
MLSTM_ROWS = 2
LRU_TCHUNK = 128
LRU_GROUP = SUBLANES
LRU_UNROLL = 8
CONV_HALO = BF16_ROWS
QKVO_ROWS = 1024
MLSTM_RW = MLSTM_DV + LANES
LOG2E = math.log2(math.e)


def _dot(a, b):
    return jnp.dot(a, b, preferred_element_type=F32)


def _dot_nt(a, b):
    return lax.dot_general(a, b, (((1,), (1,)), ((), ())), preferred_element_type=F32)


def _dot_tn(a, b):
    return lax.dot_general(a, b, (((0,), (0,)), ((), ())), preferred_element_type=F32)


def _layer_norm(y, g, b):
    mu = jnp.mean(y, axis=-1, keepdims=True)
    d = y - mu
    var = jnp.mean(d * d, axis=-1, keepdims=True)
    return d * lax.rsqrt(var + LN_EPS) * g + b


def _sigmoid(x):
    return 0.5 + 0.5 * jnp.tanh(0.5 * x)


def _silu(x):
    h = 0.5 * x
    return h + h * jnp.tanh(h)


def _log_sigmoid(x):
    return jnp.minimum(x, 0.0) - jnp.log1p(jnp.exp(-jnp.abs(x)))


def _softplus(x):
    return jnp.maximum(x, 0.0) + jnp.log1p(jnp.exp(-jnp.abs(x)))


def _pack_bf16_pair(hi, lo):
    hi_bits = lax.bitcast_convert_type(hi.astype(BF16).astype(F32), jnp.uint32)
    lo_bits = lax.bitcast_convert_type(lo.astype(BF16).astype(F32), jnp.uint32)
    return hi_bits | (lo_bits >> 16)


def _unpack_bf16_pair(w):
    hi = lax.bitcast_convert_type(w & jnp.uint32(0xFFFF0000), F32)
    lo = lax.bitcast_convert_type(w << 16, F32)
    return hi, lo


def _split3(x):
    hi = x.astype(BF16)
    r = x - hi.astype(F32)
    mid = r.astype(BF16)
    lo = (r - mid.astype(F32)).astype(BF16)
    return hi, mid, lo


def _resident(shape):
    zeros = (0,) * len(shape)
    return pl.BlockSpec(shape, lambda *_: zeros, pipeline_mode=pl.Buffered(1))


def _params(*sem):
    return pltpu.CompilerParams(dimension_semantics=sem, vmem_limit_bytes=VMEM_LIMIT_BYTES)


def _ffn_ln_kernel(x_ref, win_ref, wout_ref, g_ref, b_ref, y_ref, yb_ref):
    half = x_ref.shape[0] // 2
    rows = [slice(0, half), slice(half, 2 * half)]
    ups = []
    for r in rows:
        xb = x_ref[r, :].astype(BF16)
        ups.append(([_dot(xb, win_ref[:, lo:hi]) for lo, hi in FF_CHUNKS],
                    [_dot(xb, win_ref[:, D_FF + lo:D_FF + hi]) for lo, hi in FF_CHUNKS]))
    for r, (gates, vals) in zip(rows, ups):
        acc = jnp.zeros((half, D_MODEL), F32)
        for (lo, hi), gate, up in zip(FF_CHUNKS, gates, vals):
            acc = acc + _dot((_silu(gate) * up).astype(BF16), wout_ref[lo:hi, :])
        y = _layer_norm(DN_ALPHA * x_ref[r, :] + 0.5 * acc, g_ref[...], b_ref[...])
        y_ref[r, :] = y
        yb_ref[r, :] = y.astype(BF16)


def _ffn_ln(x, w_in, w_out, g, b):
    T = x.shape[0]
    tm = TOKEN_TILE
    row = lambda i: (i, 0)
    return pl.pallas_call(
        _ffn_ln_kernel,
        out_shape=(jax.ShapeDtypeStruct((T, D_MODEL), F32), jax.ShapeDtypeStruct((T, D_MODEL), BF16)),
        grid=(T // tm,),
        in_specs=[pl.BlockSpec((tm, D_MODEL), row), _resident((D_MODEL, 2 * D_FF)),
                  _resident((D_FF, D_MODEL)), _resident((1, D_MODEL)), _resident((1, D_MODEL))],
        out_specs=(pl.BlockSpec((tm, D_MODEL), row), pl.BlockSpec((tm, D_MODEL), row)),
        compiler_params=_params("arbitrary"),
        name="ffn_ln",
    )(x, w_in, w_out, g, b)


def _qkvo_kernel(x_ref, xp_ref, xn_ref, wq_ref, bq_ref, cwq_ref, cbq_ref, wk_ref, bk_ref, cwk_ref, cbk_ref,
                 wvot_ref, bvot_ref, qt_ref, k_ref, vot_ref, xe_ref, xt_ref, *, tiles_per_seq):
    tm = x_ref.shape[0]
    H = CONV_HALO
    xt_ref[...] = x_ref[...].astype(F32).T.astype(BF16)
    xe_ref[0:H, :] = xp_ref[...]
    xe_ref[H:H + tm, :] = x_ref[...]
    xe_ref[H + tm:, :] = xn_ref[...]
    pos = pl.program_id(0) % tiles_per_seq
    first = pos == 0
    last = pos == tiles_per_seq - 1

    rows_ext = tm + 2 * H
    halo_row = lax.broadcasted_iota(jnp.int32, (rows_ext, 1), 0)
    outside = ((halo_row < H) & first) | ((halo_row >= H + tm) & last)

    def conv_silu(w_ref, b_ref, cw_ref, cb_ref):
        u = jnp.where(outside, 0.0, _dot(xe_ref[...], w_ref[...]) + b_ref[...])
        out = cb_ref[...] + u[H:H + tm, :] * cw_ref[1:2, :]
        for j, shift in ((0, 1), (2, rows_ext - 1), (3, rows_ext - 2)):
            out = out + pltpu.roll(u, shift, axis=0)[H:H + tm, :] * cw_ref[j:j + 1, :]
        return _silu(out)

    qt_ref[...] = (conv_silu(wq_ref, bq_ref, cwq_ref, cbq_ref) * (MLSTM_DK ** -0.5)).T.astype(qt_ref.dtype)
    k_ref[...] = conv_silu(wk_ref, bk_ref, cwk_ref, cbk_ref).astype(k_ref.dtype)
    for i in range(vot_ref.shape[0] // QKVO_ROWS):
        rows = slice(i * QKVO_ROWS, (i + 1) * QKVO_ROWS)
        vot_ref[rows, :] = (_dot(wvot_ref[rows, :], xt_ref[...]) + bvot_ref[rows, :]).astype(vot_ref.dtype)


def _qkvo(xb, w, seq_len):
    T = xb.shape[0]
    tm, H = TOKEN_TILE, CONV_HALO
    tiles_per_seq = seq_len // tm
    hpt = tm // H
    last_halo = T // H - 1
    conv_w = lambda: [_resident((D_MODEL, MLSTM_QK)), _resident((1, MLSTM_QK)),
                      _resident((CONV_W, MLSTM_QK)), _resident((1, MLSTM_QK))]
    return pl.pallas_call(
        functools.partial(_qkvo_kernel, tiles_per_seq=tiles_per_seq),
        out_shape=(jax.ShapeDtypeStruct((MLSTM_QK, T), BF16), jax.ShapeDtypeStruct((T, MLSTM_QK), BF16),
                   jax.ShapeDtypeStruct((2 * MLSTM_V, T), BF16)),
        grid=(T // tm,),
        in_specs=[pl.BlockSpec((tm, D_MODEL), lambda i: (i, 0)),
                  pl.BlockSpec((H, D_MODEL), lambda i: (jnp.maximum(i * hpt - 1, 0), 0)),
                  pl.BlockSpec((H, D_MODEL), lambda i: (jnp.minimum((i + 1) * hpt, last_halo), 0))]
        + conv_w() + conv_w() + [_resident((2 * MLSTM_V, D_MODEL)), _resident((2 * MLSTM_V, 1))],
        out_specs=(pl.BlockSpec((MLSTM_QK, tm), lambda i: (0, i)), pl.BlockSpec((tm, MLSTM_QK), lambda i: (i, 0)),
                   pl.BlockSpec((2 * MLSTM_V, tm), lambda i: (0, i))),
        scratch_shapes=[pltpu.VMEM((tm + 2 * H, D_MODEL), BF16), pltpu.VMEM((D_MODEL, tm), BF16)],
        compiler_params=_params("arbitrary"),
        name="qkvo",
    )(xb, xb, xb, w["w_q"], w["b_q"], w["cw_q"], w["cb_q"], w["w_k"], w["b_k"], w["cw_k"], w["cb_k"],
      w["w_vot"], w["b_vot"])


def _mgxy_gates_kernel(x_ref, wmg_ref, bmg_ref, wxy_ref, bxy_ref, w_ref, b_ref, mg_ref, xy_ref, gc_ref, gr_ref):
    L = MLSTM_CHUNK
    H = MLSTM_HEADS
    x = x_ref[...]
    g = _dot(x, w_ref[...]) + b_ref[...]
    ls = _log_sigmoid(g)
    pieces = [_split3(ls[k * L:(k + 1) * L, :]) for k in range(x.shape[0] // L)]
    mg_ref[...] = (_dot(x, wmg_ref[...]) + bmg_ref[...]).astype(mg_ref.dtype)
    row = lax.broadcasted_iota(jnp.int32, (L, L), 0)
    col = lax.broadcasted_iota(jnp.int32, (L, L), 1)
    tril = (col <= row).astype(BF16)
    pres = [sum(_dot(tril, p) for p in ps) for ps in pieces]
    xy = _dot(x, wxy_ref[...]) + bxy_ref[...]
    c_idx = lax.broadcasted_iota(jnp.int32, (L, N_GATES), 1)
    for k, pre in enumerate(pres):
        sl = slice(k * L, (k + 1) * L)
        suf = pre[L - 1:L, :] - pre + ls[sl, :]
        cum = jnp.where(c_idx >= 2 * H, suf, pre)
        gc = jnp.where(c_idx % (2 * H) >= H, cum, g[sl, :])
        gc_ref[sl, :] = gc
        wide = jnp.concatenate([gc, jnp.zeros((L, LANES - N_GATES), F32)], axis=1)
        gr_ref[:, sl] = wide.T[:N_GATES, :]
    xy_ref[...] = _pack_bf16_pair(xy[:, :D_RNN], xy[:, D_RNN:])


def _mgxy_gates(xb, w, seq_len):
    T = xb.shape[0]
    tm = TOKEN_TILE
    tiles_per_seq = seq_len // tm
    row = lambda i: (i, 0)
    return pl.pallas_call(
        _mgxy_gates_kernel,
        out_shape=(jax.ShapeDtypeStruct((T, 2 * D_MODEL), BF16),
                   jax.ShapeDtypeStruct((seq_len, (T // seq_len) * D_RNN), jnp.uint32),
                   jax.ShapeDtypeStruct((T, N_GATES), F32), jax.ShapeDtypeStruct((N_GATES, T), F32)),
        grid=(T // tm,),
        in_specs=[pl.BlockSpec((tm, D_MODEL), row),
                  _resident((D_MODEL, 2 * D_MODEL)), _resident((1, 2 * D_MODEL)),
                  _resident((D_MODEL, 2 * D_RNN)), _resident((1, 2 * D_RNN)),
                  _resident((D_MODEL, N_GATES)), _resident((1, N_GATES))],
        out_specs=(pl.BlockSpec((tm, 2 * D_MODEL), row),
                   pl.BlockSpec((tm, D_RNN), lambda i: (i % tiles_per_seq, i // tiles_per_seq)),
                   pl.BlockSpec((tm, N_GATES), row), pl.BlockSpec((N_GATES, tm), lambda i: (0, i))),
        compiler_params=_params("arbitrary"),
        name="mgxy_gates",
    )(xb, w["w_mg"], w["b_mg"], w["w_xy"], w["b_xy"], w["w_g"], w["b_g"])


def _mlstm_kernel(*refs, reverse):
    R = MLSTM_ROWS
    per_row = 5 if reverse else 6
    row_refs = [refs[r * per_row:(r + 1) * per_row] for r in range(R)]
    rest = refs[R * per_row:]
    if reverse:
        out_ref, ct_ref, rt_ref, n_ref, m_ref = rest
    else:
        hbt_ref, gain_ref, out_ref, ct_ref, rt_ref, n_ref, m_ref = rest
    L = row_refs[0][1].shape[0]
    H, DK, DV = MLSTM_HEADS, MLSTM_DK, MLSTM_DV
    base = 2 * H if reverse else 0

    @pl.when(pl.program_id(1) == 0)
    def _():
        ct_ref[...] = jnp.zeros(ct_ref.shape, F32)
        n_ref[...] = jnp.zeros(n_ref.shape, F32)
        m_ref[...] = jnp.zeros(m_ref.shape, F32)

    key = lax.broadcasted_iota(jnp.int32, (L, L), 0)
    qry = lax.broadcasted_iota(jnp.int32, (L, L), 1)
    mask = (key >= qry) if reverse else (key <= qry)
    last = 0 if reverse else L - 1

    for r in range(R):
        qt_ref, k_ref, vt_ref, gc_ref, gr_ref = row_refs[r][:5]
        gc = gc_ref[:, base:base + 2 * H] * LOG2E
        gr = gr_ref[base:base + 2 * H, :] * LOG2E
        g_rows = [gr[H + h:H + h + 1, :] for h in range(H)]
        b_rows = [gr[h:h + 1, :] - gr[H + h:H + h + 1, :] for h in range(H)]
        b_cols = [gc[:, h:h + 1] - gc[:, H + h:H + h + 1] for h in range(H)]
        ms = [m_ref[r * H + h:r * H + h + 1, 0:1] for h in range(H)]
        qts = [qt_ref[h * DK:(h + 1) * DK, :] for h in range(H)]
        ks = [k_ref[:, h * DK:(h + 1) * DK] for h in range(H)]
        vts = [vt_ref[h * DV:(h + 1) * DV, :] for h in range(H)]

        sts = [_dot(ks[h], qts[h]) for h in range(H)]

        for h in range(H):
            u = r * H + h
            sl = slice(h * DV, (h + 1) * DV)
            log_d = jnp.where(mask, g_rows[h] + b_cols[h], -jnp.inf)
            log_inter = g_rows[h] + ms[h]
            m_t = jnp.maximum(log_inter, jnp.max(log_d, axis=0, keepdims=True))
            s = sts[h] * jnp.exp2(log_d - m_t)
            w_inter = jnp.exp2(log_inter - m_t)
            lhs = jnp.concatenate([s.astype(BF16), qts[h] * w_inter.astype(BF16)], axis=0)
            rt_ref[u, :DV, :L] = vts[h]
            rt_ref[u, DV:, :L] = jnp.ones((LANES, L), BF16)
            rt_ref[u, :DV, L:] = ct_ref[u].astype(BF16)
            rt_ref[u, DV:, L:] = jnp.broadcast_to(n_ref[u][0:1, :].astype(BF16), (LANES, DK))
            res = _dot(rt_ref[u], lhs)
            den = res[DV:DV + 1, :]
            hh = res[:DV, :] * (1.0 / jnp.maximum(jnp.abs(den), jnp.exp2(-m_t)))
            if reverse:
                out_ref[r, sl, :] = hh.astype(out_ref.dtype)
            else:
                ot_ref = row_refs[r][5]
                hh = hh + hbt_ref[r, sl, :].astype(F32)
                mu = jnp.mean(hh, axis=0, keepdims=True)
                d = hh - mu
                var = jnp.mean(d * d, axis=0, keepdims=True)
                hn = d * lax.rsqrt(var + LN_EPS) * gain_ref[sl, :]
                out_ref[r, sl, :] = (hn + hn * jnp.tanh(0.5 * ot_ref[sl, :].astype(F32))).astype(out_ref.dtype)

        for h in range(H):
            u = r * H + h
            g_tot = g_rows[h][:, last:last + 1]
            m_new = jnp.maximum(g_tot + ms[h], jnp.max(g_tot + b_rows[h], axis=-1, keepdims=True))
            wk = ks[h].astype(F32) * jnp.exp2(g_tot + b_cols[h] - m_new)
            decay = jnp.exp2(g_tot + ms[h] - m_new)
            ct_ref[u] = decay * ct_ref[u] + _dot(vts[h], wk.astype(BF16))
            n_ref[u] = decay * n_ref[u] + jnp.sum(wk, axis=0, keepdims=True)
            m_ref[u:u + 1, :] = jnp.broadcast_to(m_new, (1, m_ref.shape[1]))


def _mlstm_dir(qt, k, vot, gc, gr, hbt, gain_half, batch, seq_len, reverse):
    L, R, H = MLSTM_CHUNK, MLSTM_ROWS, MLSTM_HEADS
    nc = seq_len // L
    chunk = (lambda c: nc - 1 - c) if reverse else (lambda c: c)
    in_specs, args = [], []
    for r in range(R):
        pos = lambda b, c, r=r: (b * R + r) * nc + chunk(c)
        cols = lambda height, blk, pos=pos: pl.BlockSpec((height, L), lambda b, c: (blk, pos(b, c)))
        in_specs += [cols(MLSTM_QK, 0),
                     pl.BlockSpec((L, MLSTM_QK), lambda b, c, pos=pos: (pos(b, c), 0)),
                     cols(MLSTM_V, 0),
                     pl.BlockSpec((L, N_GATES), lambda b, c, pos=pos: (pos(b, c), 0)),
                     cols(N_GATES, 0)]
        args += [qt, k, vot, gc, gr]
        if not reverse:
            in_specs += [cols(MLSTM_V, 1)]
            args += [vot]
    slab = pl.BlockSpec((R, MLSTM_V, L), lambda b, c: (b, 0, chunk(c)))
    if not reverse:
        in_specs += [slab, _resident((MLSTM_V, L))]
        args += [hbt, gain_half]
    return pl.pallas_call(
        functools.partial(_mlstm_kernel, reverse=reverse),
        out_shape=jax.ShapeDtypeStruct((batch, MLSTM_V, seq_len), BF16),
        grid=(batch // R, nc),
        in_specs=in_specs,
        out_specs=slab,
        scratch_shapes=[pltpu.VMEM((R * H, MLSTM_DV, MLSTM_DK), F32),
                        pltpu.VMEM((R * H, MLSTM_RW, L + MLSTM_DK), BF16),
                        pltpu.VMEM((R * H, SUBLANES, MLSTM_DK), F32),
                        pltpu.VMEM((R * H, LANES), F32)],
        compiler_params=_params("arbitrary", "arbitrary"),
        name="mlstm_bwd" if reverse else "mlstm_fwd",
    )(*args)


def _mlstm(qt, k, vot, gc, gr, gain_half, batch, seq_len):
    hbt = _mlstm_dir(qt, k, vot, gc, gr, None, None, batch, seq_len, True)
    return _mlstm_dir(qt, k, vot, gc, gr, hbt, gain_half, batch, seq_len, False)


def _rglru_kernel(*refs, nc, reverse):
    if reverse:
        x_ref, xp_ref, xn_ref, cw_ref, cb_ref, wa_ref, ba_ref, wx_ref, bx_ref, lam_ref, o_ref, \
            a_ref, u_ref, state_ref, k_ref = refs
    else:
        x_ref, xp_ref, xn_ref, hb_ref, cw_ref, cb_ref, wa_ref, ba_ref, wx_ref, bx_ref, lam_ref, \
            o_ref, a_ref, u_ref, state_ref, k_ref, gy_ref = refs
    tc = x_ref.shape[0]
    c = pl.program_id(1)
    cc = nc - 1 - c if reverse else c

    @pl.when(c == 0)
    def _():
        state_ref[...] = jnp.zeros(state_ref.shape, F32)
        k_ref[...] = (-0.5 * LRU_C * LOG2E) * _softplus(-lam_ref[...])

    xr, yr = _unpack_bf16_pair(x_ref[...])
    prev = jnp.where(cc > 0, _unpack_bf16_pair(xp_ref[...])[0], 0.0)
    nxt = jnp.where(cc < nc - 1, _unpack_bf16_pair(xn_ref[...])[0], 0.0)
    xe = jnp.concatenate([prev, xr, nxt], axis=0)
    xc = cb_ref[...]
    for j in range(CONV_W):
        xc = xc + xe[j:j + tc] * cw_ref[j:j + 1, :]

    x2 = xc.reshape(tc * LRU_GROUP, D_RNN)
    xb = x2.astype(BF16)
    for n in range(LRU_BLOCKS):
        sl = slice(n * LRU_BW, (n + 1) * LRU_BW)
        t_a = jnp.tanh(_dot(xb[:, sl], wa_ref[n]) + ba_ref[:, sl])
        t_i = jnp.tanh(_dot(xb[:, sl], wx_ref[n]) + bx_ref[:, sl])
        a = jnp.exp2(k_ref[:, sl] + k_ref[:, sl] * t_a)
        y = 1.0 - a * a
        root = jnp.where(y > 0.0, y * lax.rsqrt(y), 0.0)
        hx = 0.5 * x2[:, sl]
        u = root * (hx + hx * t_i)
        a_ref[:, :, sl] = a.reshape(tc, LRU_GROUP, LRU_BW)
        u_ref[:, :, sl] = u.reshape(tc, LRU_GROUP, LRU_BW)

    if reverse:
        def step(k, h):
            t = tc - 1 - k
            h = a_ref[t] * h + u_ref[t]
            o_ref[t] = _pack_bf16_pair(h[:, :D_RNN // 2], h[:, D_RNN // 2:])
            return h
    else:
        gy_ref[...] = jax.nn.gelu(yr)

        def step(t, h):
            h = a_ref[t] * h + u_ref[t]
            hb = jnp.concatenate(_unpack_bf16_pair(hb_ref[t]), axis=1)
            o_ref[t] = (h + hb) * gy_ref[t]
            return h

    state_ref[...] = lax.fori_loop(0, tc, step, state_ref[...], unroll=LRU_UNROLL)


def _rglru_dir(xy, hb, cw, cb, wa, ba, wx, bx, lam, batch, seq_len, reverse):
    tc, G, C = LRU_TCHUNK, LRU_GROUP, D_RNN
    nc = seq_len // tc
    chunk = (lambda c: nc - 1 - c) if reverse else (lambda c: c)
    blk = lambda width: pl.BlockSpec((tc, G, width), lambda g, c: (chunk(c), g, 0))
    in_specs = [blk(C),
                pl.BlockSpec((1, G, C), lambda g, c: (jnp.maximum(chunk(c) * tc - 1, 0), g, 0)),
                pl.BlockSpec((2, G, C),
                             lambda g, c: (jnp.minimum((chunk(c) + 1) * (tc // 2), seq_len // 2 - 1), g, 0))]
    args = [xy, xy, xy]
    scratch = [pltpu.VMEM((tc, G, C), F32), pltpu.VMEM((tc, G, C), F32),
               pltpu.VMEM((G, C), F32), pltpu.VMEM((1, C), F32)]
    if not reverse:
        in_specs += [blk(C // 2)]
        args += [hb]
        scratch += [pltpu.VMEM((tc, G, C), F32)]
    in_specs += [_resident((CONV_W, C)), _resident((1, C)),
                 _resident((LRU_BLOCKS, LRU_BW, LRU_BW)), _resident((1, C)),
                 _resident((LRU_BLOCKS, LRU_BW, LRU_BW)), _resident((1, C)), _resident((1, C))]
    args += [cw, cb, wa, ba, wx, bx, lam]
    return pl.pallas_call(
        functools.partial(_rglru_kernel, nc=nc, reverse=reverse),
        out_shape=jax.ShapeDtypeStruct((seq_len, batch, C // 2), jnp.uint32) if reverse
        else jax.ShapeDtypeStruct((seq_len, batch, C), F32),
        grid=(batch // G, nc),
        in_specs=in_specs,
        out_specs=blk(C // 2) if reverse else blk(C),
        scratch_shapes=scratch,
        compiler_params=_params("arbitrary", "arbitrary"),
        name="rglru_bwd" if reverse else "rglru_fwd",
    )(*args)


def _rglru(xy, w, batch, seq_len):
    common = (w["cw_r"], w["cb_r"])
    hb = _rglru_dir(xy, None, *common, w["lru_wa"][1], w["lru_ba"][1:2], w["lru_wx"][1], w["lru_bx"][1:2],
                    w["lru_lam"][1:2], batch, seq_len, True)
    return _rglru_dir(xy, hb, *common, w["lru_wa"][0], w["lru_ba"][0:1], w["lru_wx"][0], w["lru_bx"][0:1],
                      w["lru_lam"][0:1], batch, seq_len, False)


def _kv_kernel(m_ref, g_ref, b_ref, w_ref, o_ref):
    m = _layer_norm(m_ref[...], g_ref[...], b_ref[...])
    o_ref[...] = _dot(m.astype(BF16), w_ref[...]).astype(o_ref.dtype)


def _mem_kv(mem, g, b, w_kv):
    R = mem.shape[0]
    return pl.pallas_call(
        _kv_kernel,
        out_shape=jax.ShapeDtypeStruct((R, 2 * D_MODEL), BF16),
        grid=(R // N_MEM,),
        in_specs=[pl.BlockSpec((N_MEM, D_MODEL), lambda i: (i, 0)), _resident((1, D_MODEL)),
                  _resident((1, D_MODEL)), _resident((D_MODEL, 2 * D_MODEL))],
        out_specs=pl.BlockSpec((N_MEM, 2 * D_MODEL), lambda i: (i, 0)),
        compiler_params=_params("arbitrary"),
        name="mem_kv",
    )(mem, g, b, w_kv)


def _merge_xattn_kernel(x_ref, hmt_ref, mg_ref, hr_ref, kv_ref,
                        wpm_ref, wpr_ref, wout_ref, wq_ref, wo_ref, lng_ref, lnb_ref, y_ref):
    half = x_ref.shape[0] // 2
    rows = [slice(0, half), slice(half, 2 * half)]
    pr = [_dot(hr_ref[r, :].astype(BF16), wpr_ref[...]) for r in rows]
    pm = [_dot_tn(hmt_ref[:, r], wpm_ref[...]) for r in rows]
    merged = []
    for r, a, b in zip(rows, pm, pr):
        g_m = _sigmoid(mg_ref[r, :D_MODEL].astype(F32))
        g_r = _sigmoid(mg_ref[r, D_MODEL:].astype(F32))
        merged.append((g_m * a + g_r * b).astype(BF16))
    mix = [_dot(m, wout_ref[...]) for m in merged]
    xs = [_layer_norm(DN_ALPHA * x_ref[r, :] + m, lng_ref[0:1, :], lnb_ref[0:1, :]) for r, m in zip(rows, mix)]

    qs = [_dot(x.astype(BF16), wq_ref[...]).astype(BF16) for x in xs]
    heads = [[], []]
    for h in range(XATTN_HEADS):
        sl = slice(h * XATTN_HD, (h + 1) * XATTN_HD)
        ss = [_dot_nt(q[:, sl], kv_ref[:, sl]) * (XATTN_HD ** -0.5) for q in qs]
        for i, s in enumerate(ss):
            e = jnp.exp(s - jnp.max(s, axis=-1, keepdims=True))
            p = e * (1.0 / jnp.sum(e, axis=-1, keepdims=True))
            oh = _dot(p.astype(BF16), kv_ref[:, D_MODEL + h * XATTN_HD:D_MODEL + (h + 1) * XATTN_HD])
            heads[i].append(oh.astype(BF16))
    for r, x, ohs in zip(rows, xs, heads):
        xa = _dot(jnp.concatenate(ohs, axis=1), wo_ref[...])
        y_ref[r, :] = _layer_norm(DN_ALPHA * x + xa, lng_ref[1:2, :], lnb_ref[1:2, :])


def _merge_xattn(x, hmt, mg, hr, kv, wpm, wpr, wout, wq, wo, lng, lnb, seq_len):
    T = x.shape[0]
    tm = MERGE_TILE
    tiles_per_seq = seq_len // tm
    row = lambda i: (i, 0)
    return pl.pallas_call(
        _merge_xattn_kernel,
        out_shape=jax.ShapeDtypeStruct((T, D_MODEL), F32),
        grid=(T // tm,),
        in_specs=[pl.BlockSpec((tm, D_MODEL), row),
                  pl.BlockSpec((None, MLSTM_V, tm), lambda i: (i // tiles_per_seq, 0, i % tiles_per_seq)),
                  pl.BlockSpec((tm, 2 * D_MODEL), row),
                  pl.BlockSpec((tm, D_RNN), lambda i: (i % tiles_per_seq, i // tiles_per_seq)),
                  pl.BlockSpec((N_MEM, 2 * D_MODEL), lambda i: (i // tiles_per_seq, 0)),
                  _resident((MLSTM_V, D_MODEL)), _resident((D_RNN, D_MODEL)),
                  _resident((D_MODEL, D_MODEL)), _resident((D_MODEL, D_MODEL)),
                  _resident((D_MODEL, D_MODEL)), _resident((2, D_MODEL)), _resident((2, D_MODEL))],
        out_specs=pl.BlockSpec((tm, D_MODEL), row),
        compiler_params=_params("arbitrary"),
        name="merge_xattn",
    )(x, hmt, mg, hr, kv, wpm, wpr, wout, wq, wo, lng, lnb)


def _prep_layer(p, l):
    w_in, b_in = p["w_in"][l], p["b_in"][l]
    cols = lambda lo, hi: w_in[:, lo:hi]
    bias = lambda lo, hi: b_in[lo:hi]
    return dict(
        ff1_in=p["ff1_in"][l].astype(BF16), ff1_out=p["ff1_out"][l].astype(BF16),
        ff2_in=p["ff2_in"][l].astype(BF16), ff2_out=p["ff2_out"][l].astype(BF16),
        ln_g=p["ln_g"][l], ln_b=p["ln_b"][l],
        w_q=cols(0, MLSTM_QK).astype(BF16), b_q=bias(0, MLSTM_QK)[None],
        cw_q=p["w_conv_qk"][l][:, :MLSTM_QK], cb_q=p["b_conv_qk"][l][None, :MLSTM_QK],
        w_k=cols(MLSTM_QK, OFF_V).astype(BF16), b_k=bias(MLSTM_QK, OFF_V)[None],
        cw_k=p["w_conv_qk"][l][:, MLSTM_QK:], cb_k=p["b_conv_qk"][l][None, MLSTM_QK:],
        w_vot=cols(OFF_V, OFF_GATE).astype(BF16).T, b_vot=bias(OFF_V, OFF_GATE)[:, None],
        w_mg=cols(OFF_MG, D_IN).astype(BF16), b_mg=bias(OFF_MG, D_IN)[None],
        w_xy=cols(OFF_XR, OFF_MG).astype(BF16), b_xy=bias(OFF_XR, OFF_MG)[None],
        w_g=cols(OFF_GATE, OFF_XR).astype(BF16), b_g=bias(OFF_GATE, OFF_XR)[None],
        cw_r=p["w_conv_r"][l], cb_r=p["b_conv_r"][l][None],
        lru_wa=(0.5 * p["lru_wa"][l]).astype(BF16), lru_ba=0.5 * p["lru_ba"][l],
        lru_wx=(0.5 * p["lru_wx"][l]).astype(BF16), lru_bx=0.5 * p["lru_bx"][l], lru_lam=p["lru_lam"][l],
        mh_gain_half=jnp.broadcast_to(0.5 * p["mh_gain"][l][:, None], (MLSTM_V, MLSTM_CHUNK)),
        w_pm=p["w_pm"][l].astype(BF16), w_pr=p["w_pr"][l].astype(BF16), w_out=p["w_out"][l].astype(BF16),
        xa_wq=p["xa_wq"][l].astype(BF16), xa_wkv=p["xa_wkv"][l].astype(BF16), xa_wo=p["xa_wo"][l].astype(BF16),
        mem_ln_g=p["mem_ln_g"][l][None], mem_ln_b=p["mem_ln_b"][l][None],
    )


def _trunk(x, mem, layers):
    B, S, _ = x.shape
    T = B * S
    x = x.reshape(T, D_MODEL)
    mem = mem.reshape(B * N_MEM, D_MODEL)
    for w in layers:
        x, xb = _ffn_ln(x, w["ff1_in"], w["ff1_out"], w["ln_g"][0:1], w["ln_b"][0:1])
        qt, k, vot = _qkvo(xb, w, S)
        mg, xy, gc, gr = _mgxy_gates(xb, w, S)
        hmt = _mlstm(qt, k, vot, gc, gr, w["mh_gain_half"], B, S)
        hr = _rglru(xy.reshape(S, B, D_RNN), w, B, S)
        kv = _mem_kv(mem, w["mem_ln_g"], w["mem_ln_b"], w["xa_wkv"])
        x = _merge_xattn(x, hmt, mg, hr.reshape(S, B * D_RNN), kv, w["w_pm"], w["w_pr"],
                         w["w_out"], w["xa_wq"], w["xa_wo"], w["ln_g"][1:3], w["ln_b"][1:3], S)
        x, _ = _ffn_ln(x, w["ff2_in"], w["ff2_out"], w["ln_g"][3:4], w["ln_b"][3:4])
    return x.reshape(B, S, D_MODEL)


def kernel(x_prompt, x_sample, mem_prompt, mem_sample, w_in, b_in, w_conv_qk, b_conv_qk, w_conv_r, b_conv_r,
           mh_gain, lru_wa, lru_ba, lru_wx, lru_bx, lru_lam, w_pm, w_pr, w_out, xa_wq, xa_wkv, xa_wo,
           mem_ln_g, mem_ln_b, ff1_in, ff1_out, ff2_in, ff2_out, ln_g, ln_b):
    p = dict(w_in=w_in, b_in=b_in, w_conv_qk=w_conv_qk, b_conv_qk=b_conv_qk, w_conv_r=w_conv_r,
             b_conv_r=b_conv_r, mh_gain=mh_gain, lru_wa=lru_wa, lru_ba=lru_ba, lru_wx=lru_wx,
             lru_bx=lru_bx, lru_lam=lru_lam, w_pm=w_pm, w_pr=w_pr, w_out=w_out, xa_wq=xa_wq,
             xa_wkv=xa_wkv, xa_wo=xa_wo, mem_ln_g=mem_ln_g, mem_ln_b=mem_ln_b, ff1_in=ff1_in,
             ff1_out=ff1_out, ff2_in=ff2_in, ff2_out=ff2_out, ln_g=ln_g, ln_b=ln_b)
    layers = [_prep_layer(p, l) for l in range(DEPTH)]
    return (_trunk(x_prompt, mem_prompt, layers), _trunk(x_sample, mem_sample, layers))
```

```python
import functools
import math

import jax
import jax.numpy as jnp
from jax import lax
from jax.experimental import pallas as pl
from jax.experimental.pallas import tpu as pltpu

F32 = jnp.float32
BF16 = jnp.bfloat16

D_MODEL = 1024
DEPTH = 2
N_MEM = 256
MLSTM_HEADS = 4
MLSTM_DK = D_MODEL // 4
MLSTM_DV = D_MODEL // 2
MLSTM_QK = MLSTM_HEADS * MLSTM_DK
MLSTM_V = MLSTM_HEADS * MLSTM_DV
D_RNN = D_MODEL
LRU_BLOCKS = 8
LRU_BW = D_RNN // LRU_BLOCKS
LRU_C = 8.0
CONV_W = 4
XATTN_HEADS = 4
XATTN_HD = D_MODEL // XATTN_HEADS
D_FF = 2816
DN_ALPHA = (2.0 * DEPTH) ** 0.25
LN_EPS = 1e-5

OFF_V = 2 * MLSTM_QK
OFF_O = OFF_V + MLSTM_V
OFF_GATE = OFF_O + MLSTM_V
OFF_XR = OFF_GATE + 4 * MLSTM_HEADS
OFF_YR = OFF_XR + D_RNN
OFF_MG = OFF_YR + D_RNN
D_IN = OFF_MG + 2 * D_MODEL
N_GATES = 4 * MLSTM_HEADS

VMEM_LIMIT_BYTES = 56 * 1024 * 1024
LANES = 128
SUBLANES = 8
BF16_ROWS = 16

TOKEN_TILE = 512
MERGE_TILE = 512
MXU_DEPTH = 256
FF_CHUNKS = ((0, 6 * MXU_DEPTH), (6 * MXU_DEPTH, D_FF))
MLSTM_CHUNK = 256
MLSTM_ROWS = 2
LRU_TCHUNK = 128
LRU_GROUP = SUBLANES
LRU_UNROLL = 8
CONV_HALO = BF16_ROWS
QKVO_ROWS = 1024
MLSTM_RW = MLSTM_DV + LANES
LOG2E = math.log2(math.e)


def _dot(a, b):
    return jnp.dot(a, b, preferred_element_type=F32)


def _dot_nt(a, b):
    return lax.dot_general(a, b, (((1,), (1,)), ((), ())), preferred_element_type=F32)


def _dot_tn(a, b):
    return lax.dot_general(a, b, (((0,), (0,)), ((), ())), preferred_element_type=F32)


def _layer_norm(y, g, b):
    mu = jnp.mean(y, axis=-1, keepdims=True)
    d = y - mu
    var = jnp.mean(d * d, axis=-1, keepdims=True)
    return d * lax.rsqrt(var + LN_EPS) * g + b


def _sigmoid(x):
    return 0.5 + 0.5 * jnp.tanh(0.5 * x)


def _silu(x):
    h = 0.5 * x
    return h + h * jnp.tanh(h)


def _log_sigmoid(x):
    return jnp.minimum(x, 0.0) - jnp.log1p(jnp.exp(-jnp.abs(x)))


def _softplus(x):
    return jnp.maximum(x, 0.0) + jnp.log1p(jnp.exp(-jnp.abs(x)))


def _pack_bf16_pair(hi, lo):
    hi_bits = lax.bitcast_convert_type(hi.astype(BF16).astype(F32), jnp.uint32)
    lo_bits = lax.bitcast_convert_type(lo.astype(BF16).astype(F32), jnp.uint32)
    return hi_bits | (lo_bits >> 16)


def _unpack_bf16_pair(w):
    hi = lax.bitcast_convert_type(w & jnp.uint32(0xFFFF0000), F32)
    lo = lax.bitcast_convert_type(w << 16, F32)
    return hi, lo


def _split3(x):
    hi = x.astype(BF16)
    r = x - hi.astype(F32)
    mid = r.astype(BF16)
    lo = (r - mid.astype(F32)).astype(BF16)
    return hi, mid, lo


def _resident(shape):
    zeros = (0,) * len(shape)
    return pl.BlockSpec(shape, lambda *_: zeros, pipeline_mode=pl.Buffered(1))


def _params(*sem):
    return pltpu.CompilerParams(dimension_semantics=sem, vmem_limit_bytes=VMEM_LIMIT_BYTES)


def _ffn_ln_kernel(x_ref, win_ref, wout_ref, g_ref, b_ref, y_ref, yb_ref):
    half = x_ref.shape[0] // 2
    rows = [slice(0, half), slice(half, 2 * half)]
    ups = []
    for r in rows:
        xb = x_ref[r, :].astype(BF16)
        ups.append(([_dot(xb, win_ref[:, lo:hi]) for lo, hi in FF_CHUNKS],
                    [_dot(xb, win_ref[:, D_FF + lo:D_FF + hi]) for lo, hi in FF_CHUNKS]))
    for r, (gates, vals) in zip(rows, ups):
        acc = jnp.zeros((half, D_MODEL), F32)
        for (lo, hi), gate, up in zip(FF_CHUNKS, gates, vals):
            acc = acc + _dot((_silu(gate) * up).astype(BF16), wout_ref[lo:hi, :])
        y = _layer_norm(DN_ALPHA * x_ref[r, :] + 0.5 * acc, g_ref[...], b_ref[...])
        y_ref[r, :] = y
        yb_ref[r, :] = y.astype(BF16)


def _ffn_ln(x, w_in, w_out, g, b):
    T = x.shape[0]
    tm = TOKEN_TILE
    row = lambda i: (i, 0)
    return pl.pallas_call(
        _ffn_ln_kernel,
        out_shape=(jax.ShapeDtypeStruct((T, D_MODEL), F32), jax.ShapeDtypeStruct((T, D_MODEL), BF16)),
        grid=(T // tm,),
        in_specs=[pl.BlockSpec((tm, D_MODEL), row), _resident((D_MODEL, 2 * D_FF)),
                  _resident((D_FF, D_MODEL)), _resident((1, D_MODEL)), _resident((1, D_MODEL))],
        out_specs=(pl.BlockSpec((tm, D_MODEL), row), pl.BlockSpec((tm, D_MODEL), row)),
        compiler_params=_params("arbitrary"),
        name="ffn_ln",
    )(x, w_in, w_out, g, b)


def _qkvo_kernel(x_ref, xp_ref, xn_ref, wq_ref, bq_ref, cwq_ref, cbq_ref, wk_ref, bk_ref, cwk_ref, cbk_ref,
                 wvot_ref, bvot_ref, qt_ref, k_ref, vot_ref, xe_ref, xt_ref, *, tiles_per_seq):
    tm = x_ref.shape[0]
    H = CONV_HALO
    xt_ref[...] = x_ref[...].astype(F32).T.astype(BF16)
    xe_ref[0:H, :] = xp_ref[...]
    xe_ref[H:H + tm, :] = x_ref[...]
    xe_ref[H + tm:, :] = xn_ref[...]
    pos = pl.program_id(0) % tiles_per_seq
    first = pos == 0
    last = pos == tiles_per_seq - 1

    rows_ext = tm + 2 * H
    halo_row = lax.broadcasted_iota(jnp.int32, (rows_ext, 1), 0)
    outside = ((halo_row < H) & first) | ((halo_row >= H + tm) & last)

    def conv_silu(w_ref, b_ref, cw_ref, cb_ref):
        u = jnp.where(outside, 0.0, _dot(xe_ref[...], w_ref[...]) + b_ref[...])
        out = cb_ref[...] + u[H:H + tm, :] * cw_ref[1:2, :]
        for j, shift in ((0, 1), (2, rows_ext - 1), (3, rows_ext - 2)):
            out = out + pltpu.roll(u, shift, axis=0)[H:H + tm, :] * cw_ref[j:j + 1, :]
        return _silu(out)

    qt_ref[...] = (conv_silu(wq_ref, bq_ref, cwq_ref, cbq_ref) * (MLSTM_DK ** -0.5)).T.astype(qt_ref.dtype)
    k_ref[...] = conv_silu(wk_ref, bk_ref, cwk_ref, cbk_ref).astype(k_ref.dtype)
    for i in range(vot_ref.shape[0] // QKVO_ROWS):
        rows = slice(i * QKVO_ROWS, (i + 1) * QKVO_ROWS)
        vot_ref[rows, :] = (_dot(wvot_ref[rows, :], xt_ref[...]) + bvot_ref[rows, :]).astype(vot_ref.dtype)


def _qkvo(xb, w, seq_len):
    T = xb.shape[0]
    tm, H = TOKEN_TILE, CONV_HALO
    tiles_per_seq = seq_len // tm
    hpt = tm // H
    last_halo = T // H - 1
    conv_w = lambda: [_resident((D_MODEL, MLSTM_QK)), _resident((1, MLSTM_QK)),
                      _resident((CONV_W, MLSTM_QK)), _resident((1, MLSTM_QK))]
    return pl.pallas_call(
        functools.partial(_qkvo_kernel, tiles_per_seq=tiles_per_seq),
        out_shape=(jax.ShapeDtypeStruct((MLSTM_QK, T), BF16), jax.ShapeDtypeStruct((T, MLSTM_QK), BF16),
                   jax.ShapeDtypeStruct((2 * MLSTM_V, T), BF16)),
        grid=(T // tm,),
        in_specs=[pl.BlockSpec((tm, D_MODEL), lambda i: (i, 0)),
                  pl.BlockSpec((H, D_MODEL), lambda i: (jnp.maximum(i * hpt - 1, 0), 0)),
                  pl.BlockSpec((H, D_MODEL), lambda i: (jnp.minimum((i + 1) * hpt, last_halo), 0))]
        + conv_w() + conv_w() + [_resident((2 * MLSTM_V, D_MODEL)), _resident((2 * MLSTM_V, 1))],
        out_specs=(pl.BlockSpec((MLSTM_QK, tm), lambda i: (0, i)), pl.BlockSpec((tm, MLSTM_QK), lambda i: (i, 0)),
                   pl.BlockSpec((2 * MLSTM_V, tm), lambda i: (0, i))),
        scratch_shapes=[pltpu.VMEM((tm + 2 * H, D_MODEL), BF16), pltpu.VMEM((D_MODEL, tm), BF16)],
        compiler_params=_params("arbitrary"),
        name="qkvo",
    )(xb, xb, xb, w["w_q"], w["b_q"], w["cw_q"], w["cb_q"], w["w_k"], w["b_k"], w["cw_k"], w["cb_k"],
      w["w_vot"], w["b_vot"])


def _mgxy_gates_kernel(x_ref, wmg_ref, bmg_ref, wxy_ref, bxy_ref, w_ref, b_ref, mg_ref, xy_ref, gc_ref, gr_ref):
    L = MLSTM_CHUNK
    H = MLSTM_HEADS
    x = x_ref[...]
    g = _dot(x, w_ref[...]) + b_ref[...]
    ls = _log_sigmoid(g)
    pieces = [_split3(ls[k * L:(k + 1) * L, :]) for k in range(x.shape[0] // L)]
    xy = _dot(x, wxy_ref[...]) + bxy_ref[...]
    row = lax.broadcasted_iota(jnp.int32, (L, L), 0)
    col = lax.broadcasted_iota(jnp.int32, (L, L), 1)
    tril = (col <= row).astype(BF16)
    pres = [sum(_dot(tril, p) for p in ps) for ps in pieces]
    mg_ref[...] = (_dot(x, wmg_ref[...]) + bmg_ref[...]).astype(mg_ref.dtype)
    c_idx = lax.broadcasted_iota(jnp.int32, (L, N_GATES), 1)
    for k, pre in enumerate(pres):
        sl = slice(k * L, (k + 1) * L)
        suf = pre[L - 1:L, :] - pre + ls[sl, :]
        cum = jnp.where(c_idx >= 2 * H, suf, pre)
        gc = jnp.where(c_idx % (2 * H) >= H, cum, g[sl, :])
        gc_ref[sl, :] = gc
        wide = jnp.concatenate([gc, jnp.zeros((L, LANES - N_GATES), F32)], axis=1)
        gr_ref[:, sl] = wide.T[:N_GATES, :]
    xy_ref[...] = _pack_bf16_pair(xy[:, :D_RNN], jax.nn.gelu(xy[:, D_RNN:]))


def _mgxy_gates(xb, w, seq_len):
    T = xb.shape[0]
    tm = TOKEN_TILE
    tiles_per_seq = seq_len // tm
    row = lambda i: (i, 0)
    return pl.pallas_call(
        _mgxy_gates_kernel,
        out_shape=(jax.ShapeDtypeStruct((T, 2 * D_MODEL), BF16),
                   jax.ShapeDtypeStruct((seq_len, (T // seq_len) * D_RNN), jnp.uint32),
                   jax.ShapeDtypeStruct((T, N_GATES), F32), jax.ShapeDtypeStruct((N_GATES, T), F32)),
        grid=(T // tm,),
        in_specs=[pl.BlockSpec((tm, D_MODEL), row),
                  _resident((D_MODEL, 2 * D_MODEL)), _resident((1, 2 * D_MODEL)),
                  _resident((D_MODEL, 2 * D_RNN)), _resident((1, 2 * D_RNN)),
                  _resident((D_MODEL, N_GATES)), _resident((1, N_GATES))],
        out_specs=(pl.BlockSpec((tm, 2 * D_MODEL), row),
                   pl.BlockSpec((tm, D_RNN), lambda i: (i % tiles_per_seq, i // tiles_per_seq)),
                   pl.BlockSpec((tm, N_GATES), row), pl.BlockSpec((N_GATES, tm), lambda i: (0, i))),
        compiler_params=_params("arbitrary"),
        name="mgxy_gates",
    )(xb, w["w_mg"], w["b_mg"], w["w_xy"], w["b_xy"], w["w_g"], w["b_g"])


def _mlstm_kernel(*refs, reverse):
    R = MLSTM_ROWS
    per_row = 5 if reverse else 6
    row_refs = [refs[r * per_row:(r + 1) * per_row] for r in range(R)]
    rest = refs[R * per_row:]
    if reverse:
        out_ref, ct_ref, rt_ref, n_ref, m_ref = rest
    else:
        hbt_ref, gain_ref, out_ref, ct_ref, rt_ref, n_ref, m_ref = rest
    L = row_refs[0][1].shape[0]
    H, DK, DV = MLSTM_HEADS, MLSTM_DK, MLSTM_DV
    base = 2 * H if reverse else 0

    @pl.when(pl.program_id(1) == 0)
    def _():
        ct_ref[...] = jnp.zeros(ct_ref.shape, F32)
        n_ref[...] = jnp.zeros(n_ref.shape, F32)
        m_ref[...] = jnp.zeros(m_ref.shape, F32)

    key = lax.broadcasted_iota(jnp.int32, (L, L), 0)
    qry = lax.broadcasted_iota(jnp.int32, (L, L), 1)
    mask = (key >= qry) if reverse else (key <= qry)
    last = 0 if reverse else L - 1

    for r in range(R):
        qt_ref, k_ref, vt_ref, gc_ref, gr_ref = row_refs[r][:5]
        gc = gc_ref[:, base:base + 2 * H] * LOG2E
        gr = gr_ref[base:base + 2 * H, :] * LOG2E
        g_rows = [gr[H + h:H + h + 1, :] for h in range(H)]
        b_rows = [gr[h:h + 1, :] - gr[H + h:H + h + 1, :] for h in range(H)]
        b_cols = [gc[:, h:h + 1] - gc[:, H + h:H + h + 1] for h in range(H)]
        ms = [m_ref[r * H + h:r * H + h + 1, 0:1] for h in range(H)]
        qts = [qt_ref[h * DK:(h + 1) * DK, :] for h in range(H)]
        ks = [k_ref[:, h * DK:(h + 1) * DK] for h in range(H)]
        vts = [vt_ref[h * DV:(h + 1) * DV, :] for h in range(H)]

        sts = [_dot(ks[h], qts[h]) for h in range(H)]

        for h in range(H):
            u = r * H + h
            sl = slice(h * DV, (h + 1) * DV)
            log_d = jnp.where(mask, g_rows[h] + b_cols[h], -jnp.inf)
            log_inter = g_rows[h] + ms[h]
            m_t = jnp.maximum(log_inter, jnp.max(log_d, axis=0, keepdims=True))
            s = sts[h] * jnp.exp2(log_d - m_t)
            w_inter = jnp.exp2(log_inter - m_t)
            lhs = jnp.concatenate([s.astype(BF16), qts[h] * w_inter.astype(BF16)], axis=0)
            rt_ref[u, :DV, :L] = vts[h]
            rt_ref[u, DV:, :L] = jnp.ones((LANES, L), BF16)
            rt_ref[u, :DV, L:] = ct_ref[u].astype(BF16)
            rt_ref[u, DV:, L:] = jnp.broadcast_to(n_ref[u][0:1, :].astype(BF16), (LANES, DK))
            res = _dot(rt_ref[u], lhs)
            den = res[DV:DV + 1, :]
            hh = res[:DV, :] * (1.0 / jnp.maximum(jnp.abs(den), jnp.exp2(-m_t)))
            if reverse:
                out_ref[r, sl, :] = hh.astype(out_ref.dtype)
            else:
                ot_ref = row_refs[r][5]
                hh = hh + hbt_ref[r, sl, :].astype(F32)
                mu = jnp.mean(hh, axis=0, keepdims=True)
                d = hh - mu
                var = jnp.mean(d * d, axis=0, keepdims=True)
                hn = d * lax.rsqrt(var + LN_EPS) * gain_ref[sl, :]
                out_ref[r, sl, :] = (hn + hn * jnp.tanh(0.5 * ot_ref[sl, :].astype(F32))).astype(out_ref.dtype)

        for h in range(H):
            u = r * H + h
            g_tot = g_rows[h][:, last:last + 1]
            m_new = jnp.maximum(g_tot + ms[h], jnp.max(g_tot + b_rows[h], axis=-1, keepdims=True))
            wk = ks[h].astype(F32) * jnp.exp2(g_tot + b_cols[h] - m_new)
            decay = jnp.exp2(g_tot + ms[h] - m_new)
            ct_ref[u] = decay * ct_ref[u] + _dot(vts[h], wk.astype(BF16))
            n_ref[u] = decay * n_ref[u] + jnp.sum(wk, axis=0, keepdims=True)
            m_ref[u:u + 1, :] = jnp.broadcast_to(m_new, (1, m_ref.shape[1]))


def _mlstm_dir(qt, k, vot, gc, gr, hbt, gain_half, batch, seq_len, reverse):
    L, R, H = MLSTM_CHUNK, MLSTM_ROWS, MLSTM_HEADS
    nc = seq_len // L
    chunk = (lambda c: nc - 1 - c) if reverse else (lambda c: c)
    in_specs, args = [], []
    for r in range(R):
        pos = lambda b, c, r=r: (b * R + r) * nc + chunk(c)
        cols = lambda height, blk, pos=pos: pl.BlockSpec((height, L), lambda b, c: (blk, pos(b, c)))
        in_specs += [cols(MLSTM_QK, 0),
                     pl.BlockSpec((L, MLSTM_QK), lambda b, c, pos=pos: (pos(b, c), 0)),
                     cols(MLSTM_V, 0),
                     pl.BlockSpec((L, N_GATES), lambda b, c, pos=pos: (pos(b, c), 0)),
                     cols(N_GATES, 0)]
        args += [qt, k, vot, gc, gr]
        if not reverse:
            in_specs += [cols(MLSTM_V, 1)]
            args += [vot]
    slab = pl.BlockSpec((R, MLSTM_V, L), lambda b, c: (b, 0, chunk(c)))
    if not reverse:
        in_specs += [slab, _resident((MLSTM_V, L))]
        args += [hbt, gain_half]
    return pl.pallas_call(
        functools.partial(_mlstm_kernel, reverse=reverse),
        out_shape=jax.ShapeDtypeStruct((batch, MLSTM_V, seq_len), BF16),
        grid=(batch // R, nc),
        in_specs=in_specs,
        out_specs=slab,
        scratch_shapes=[pltpu.VMEM((R * H, MLSTM_DV, MLSTM_DK), F32),
                        pltpu.VMEM((R * H, MLSTM_RW, L + MLSTM_DK), BF16),
                        pltpu.VMEM((R * H, SUBLANES, MLSTM_DK), F32),
                        pltpu.VMEM((R * H, LANES), F32)],
        compiler_params=_params("arbitrary", "arbitrary"),
        name="mlstm_bwd" if reverse else "mlstm_fwd",
    )(*args)


def _mlstm(qt, k, vot, gc, gr, gain_half, batch, seq_len):
    hbt = _mlstm_dir(qt, k, vot, gc, gr, None, None, batch, seq_len, True)
    return _mlstm_dir(qt, k, vot, gc, gr, hbt, gain_half, batch, seq_len, False)


def _rglru_kernel(*refs, nc, reverse):
    if reverse:
        x_ref, xp_ref, xn_ref, cw_ref, cb_ref, wa_ref, ba_ref, wx_ref, bx_ref, lam_ref, o_ref, \
            a_ref, u_ref, state_ref, k_ref = refs
    else:
        x_ref, xp_ref, xn_ref, hb_ref, cw_ref, cb_ref, wa_ref, ba_ref, wx_ref, bx_ref, lam_ref, \
            o_ref, a_ref, u_ref, state_ref, k_ref, gy_ref = refs
    tc = x_ref.shape[0]
    c = pl.program_id(1)
    cc = nc - 1 - c if reverse else c

    @pl.when(c == 0)
    def _():
        state_ref[...] = jnp.zeros(state_ref.shape, F32)
        k_ref[...] = (-0.5 * LRU_C * LOG2E) * _softplus(-lam_ref[...])

    xr, gy = _unpack_bf16_pair(x_ref[...])
    prev = jnp.where(cc > 0, _unpack_bf16_pair(xp_ref[...])[0], 0.0)
    nxt = jnp.where(cc < nc - 1, _unpack_bf16_pair(xn_ref[...])[0], 0.0)
    xe = jnp.concatenate([prev, xr, nxt], axis=0)
    xc = cb_ref[...]
    for j in range(CONV_W):
        xc = xc + xe[j:j + tc] * cw_ref[j:j + 1, :]

    x2 = xc.reshape(tc * LRU_GROUP, D_RNN)
    xb = x2.astype(BF16)
    for n in range(LRU_BLOCKS):
        sl = slice(n * LRU_BW, (n + 1) * LRU_BW)
        t_a = jnp.tanh(_dot(xb[:, sl], wa_ref[n]) + ba_ref[:, sl])
        t_i = jnp.tanh(_dot(xb[:, sl], wx_ref[n]) + bx_ref[:, sl])
        a = jnp.exp2(k_ref[:, sl] + k_ref[:, sl] * t_a)
        y = 1.0 - a * a
        root = jnp.where(y > 0.0, y * lax.rsqrt(y), 0.0)
        hx = 0.5 * x2[:, sl]
        u = root * (hx + hx * t_i)
        a_ref[:, :, sl] = a.reshape(tc, LRU_GROUP, LRU_BW)
        u_ref[:, :, sl] = u.reshape(tc, LRU_GROUP, LRU_BW)

    if reverse:
        def step(k, h):
            t = tc - 1 - k
            h = a_ref[t] * h + u_ref[t]
            o_ref[t] = _pack_bf16_pair(h[:, :D_RNN // 2], h[:, D_RNN // 2:])
            return h
    else:
        gy_ref[...] = gy

        def step(t, h):
            h = a_ref[t] * h + u_ref[t]
            hb = jnp.concatenate(_unpack_bf16_pair(hb_ref[t]), axis=1)
            o_ref[t] = (h + hb) * gy_ref[t]
            return h

    state_ref[...] = lax.fori_loop(0, tc, step, state_ref[...], unroll=LRU_UNROLL)


def _rglru_dir(xy, hb, cw, cb, wa, ba, wx, bx, lam, batch, seq_len, reverse):
    tc, G, C = LRU_TCHUNK, LRU_GROUP, D_RNN
    nc = seq_len // tc
    chunk = (lambda c: nc - 1 - c) if reverse else (lambda c: c)
    blk = lambda width: pl.BlockSpec((tc, G, width), lambda g, c: (chunk(c), g, 0))
    in_specs = [blk(C),
                pl.BlockSpec((1, G, C), lambda g, c: (jnp.maximum(chunk(c) * tc - 1, 0), g, 0)),
                pl.BlockSpec((2, G, C),
                             lambda g, c: (jnp.minimum((chunk(c) + 1) * (tc // 2), seq_len // 2 - 1), g, 0))]
    args = [xy, xy, xy]
    scratch = [pltpu.VMEM((tc, G, C), F32), pltpu.VMEM((tc, G, C), F32),
               pltpu.VMEM((G, C), F32), pltpu.VMEM((1, C), F32)]
    if not reverse:
        in_specs += [blk(C // 2)]
        args += [hb]
        scratch += [pltpu.VMEM((tc, G, C), F32)]
    in_specs += [_resident((CONV_W, C)), _resident((1, C)),
                 _resident((LRU_BLOCKS, LRU_BW, LRU_BW)), _resident((1, C)),
                 _resident((LRU_BLOCKS, LRU_BW, LRU_BW)), _resident((1, C)), _resident((1, C))]
    args += [cw, cb, wa, ba, wx, bx, lam]
    return pl.pallas_call(
        functools.partial(_rglru_kernel, nc=nc, reverse=reverse),
        out_shape=jax.ShapeDtypeStruct((seq_len, batch, C // 2), jnp.uint32) if reverse
        else jax.ShapeDtypeStruct((seq_len, batch, C), F32),
        grid=(batch // G, nc),
        in_specs=in_specs,
        out_specs=blk(C // 2) if reverse else blk(C),
        scratch_shapes=scratch,
        compiler_params=_params("arbitrary", "arbitrary"),
        name="rglru_bwd" if reverse else "rglru_fwd",
    )(*args)


def _rglru(xy, w, batch, seq_len):
    common = (w["cw_r"], w["cb_r"])
    hb = _rglru_dir(xy, None, *common, w["lru_wa"][1], w["lru_ba"][1:2], w["lru_wx"][1], w["lru_bx"][1:2],
                    w["lru_lam"][1:2], batch, seq_len, True)
    return _rglru_dir(xy, hb, *common, w["lru_wa"][0], w["lru_ba"][0:1], w["lru_wx"][0], w["lru_bx"][0:1],
                      w["lru_lam"][0:1], batch, seq_len, False)


def _kv_kernel(m_ref, g_ref, b_ref, w_ref, o_ref):
    m = _layer_norm(m_ref[...], g_ref[...], b_ref[...])
    o_ref[...] = _dot(m.astype(BF16), w_ref[...]).astype(o_ref.dtype)


def _mem_kv(mem, g, b, w_kv):
    R = mem.shape[0]
    return pl.pallas_call(
        _kv_kernel,
        out_shape=jax.ShapeDtypeStruct((R, 2 * D_MODEL), BF16),
        grid=(R // N_MEM,),
        in_specs=[pl.BlockSpec((N_MEM, D_MODEL), lambda i: (i, 0)), _resident((1, D_MODEL)),
                  _resident((1, D_MODEL)), _resident((D_MODEL, 2 * D_MODEL))],
        out_specs=pl.BlockSpec((N_MEM, 2 * D_MODEL), lambda i: (i, 0)),
        compiler_params=_params("arbitrary"),
        name="mem_kv",
    )(mem, g, b, w_kv)


def _merge_xattn_kernel(x_ref, hmt_ref, mg_ref, hr_ref, kv_ref,
                        wpm_ref, wpr_ref, wout_ref, wq_ref, wo_ref, lng_ref, lnb_ref, y_ref):
    half = x_ref.shape[0] // 2
    rows = [slice(0, half), slice(half, 2 * half)]
    pr = [_dot(hr_ref[r, :].astype(BF16), wpr_ref[...]) for r in rows]
    pm = [_dot_tn(hmt_ref[:, r], wpm_ref[...]) for r in rows]
    merged = []
    for r, a, b in zip(rows, pm, pr):
        g_m = _sigmoid(mg_ref[r, :D_MODEL].astype(F32))
        g_r = _sigmoid(mg_ref[r, D_MODEL:].astype(F32))
        merged.append((g_m * a + g_r * b).astype(BF16))
    mix = [_dot(m, wout_ref[...]) for m in merged]
    xs = [_layer_norm(DN_ALPHA * x_ref[r, :] + m, lng_ref[0:1, :], lnb_ref[0:1, :]) for r, m in zip(rows, mix)]

    qs = [_dot(x.astype(BF16), wq_ref[...]).astype(BF16) for x in xs]
    heads = [[], []]
    for h in range(XATTN_HEADS):
        sl = slice(h * XATTN_HD, (h + 1) * XATTN_HD)
        ss = [_dot_nt(q[:, sl], kv_ref[:, sl]) * (XATTN_HD ** -0.5) for q in qs]
        for i, s in enumerate(ss):
            e = jnp.exp(s - jnp.max(s, axis=-1, keepdims=True))
            p = e * (1.0 / jnp.sum(e, axis=-1, keepdims=True))
            oh = _dot(p.astype(BF16), kv_ref[:, D_MODEL + h * XATTN_HD:D_MODEL + (h + 1) * XATTN_HD])
            heads[i].append(oh.astype(BF16))
    for r, x, ohs in zip(rows, xs, heads):
        xa = _dot(jnp.concatenate(ohs, axis=1), wo_ref[...])
        y_ref[r, :] = _layer_norm(DN_ALPHA * x + xa, lng_ref[1:2, :], lnb_ref[1:2, :])


def _merge_xattn(x, hmt, mg, hr, kv, wpm, wpr, wout, wq, wo, lng, lnb, seq_len):
    T = x.shape[0]
    tm = MERGE_TILE
    tiles_per_seq = seq_len // tm
    row = lambda i: (i, 0)
    return pl.pallas_call(
        _merge_xattn_kernel,
        out_shape=jax.ShapeDtypeStruct((T, D_MODEL), F32),
        grid=(T // tm,),
        in_specs=[pl.BlockSpec((tm, D_MODEL), row),
                  pl.BlockSpec((None, MLSTM_V, tm), lambda i: (i // tiles_per_seq, 0, i % tiles_per_seq)),
                  pl.BlockSpec((tm, 2 * D_MODEL), row),
                  pl.BlockSpec((tm, D_RNN), lambda i: (i % tiles_per_seq, i // tiles_per_seq)),
                  pl.BlockSpec((N_MEM, 2 * D_MODEL), lambda i: (i // tiles_per_seq, 0)),
                  _resident((MLSTM_V, D_MODEL)), _resident((D_RNN, D_MODEL)),
                  _resident((D_MODEL, D_MODEL)), _resident((D_MODEL, D_MODEL)),
                  _resident((D_MODEL, D_MODEL)), _resident((2, D_MODEL)), _resident((2, D_MODEL))],
        out_specs=pl.BlockSpec((tm, D_MODEL), row),
        compiler_params=_params("arbitrary"),
        name="merge_xattn",
    )(x, hmt, mg, hr, kv, wpm, wpr, wout, wq, wo, lng, lnb)


def _prep_layer(p, l):
    w_in, b_in = p["w_in"][l], p["b_in"][l]
    cols = lambda lo, hi: w_in[:, lo:hi]
    bias = lambda lo, hi: b_in[lo:hi]
    return dict(
        ff1_in=p["ff1_in"][l].astype(BF16), ff1_out=p["ff1_out"][l].astype(BF16),
        ff2_in=p["ff2_in"][l].astype(BF16), ff2_out=p["ff2_out"][l].astype(BF16),
        ln_g=p["ln_g"][l], ln_b=p["ln_b"][l],
        w_q=cols(0, MLSTM_QK).astype(BF16), b_q=bias(0, MLSTM_QK)[None],
        cw_q=p["w_conv_qk"][l][:, :MLSTM_QK], cb_q=p["b_conv_qk"][l][None, :MLSTM_QK],
        w_k=cols(MLSTM_QK, OFF_V).astype(BF16), b_k=bias(MLSTM_QK, OFF_V)[None],
        cw_k=p["w_conv_qk"][l][:, MLSTM_QK:], cb_k=p["b_conv_qk"][l][None, MLSTM_QK:],
        w_vot=cols(OFF_V, OFF_GATE).astype(BF16).T, b_vot=bias(OFF_V, OFF_GATE)[:, None],
        w_mg=cols(OFF_MG, D_IN).astype(BF16), b_mg=bias(OFF_MG, D_IN)[None],
        w_xy=cols(OFF_XR, OFF_MG).astype(BF16), b_xy=bias(OFF_XR, OFF_MG)[None],
        w_g=cols(OFF_GATE, OFF_XR).astype(BF16), b_g=bias(OFF_GATE, OFF_XR)[None],
        cw_r=p["w_conv_r"][l], cb_r=p["b_conv_r"][l][None],
        lru_wa=(0.5 * p["lru_wa"][l]).astype(BF16), lru_ba=0.5 * p["lru_ba"][l],
        lru_wx=(0.5 * p["lru_wx"][l]).astype(BF16), lru_bx=0.5 * p["lru_bx"][l], lru_lam=p["lru_lam"][l],
        mh_gain_half=jnp.broadcast_to(0.5 * p["mh_gain"][l][:, None], (MLSTM_V, MLSTM_CHUNK)),
        w_pm=p["w_pm"][l].astype(BF16), w_pr=p["w_pr"][l].astype(BF16), w_out=p["w_out"][l].astype(BF16),
        xa_wq=p["xa_wq"][l].astype(BF16), xa_wkv=p["xa_wkv"][l].astype(BF16), xa_wo=p["xa_wo"][l].astype(BF16),
        mem_ln_g=p["mem_ln_g"][l][None], mem_ln_b=p["mem_ln_b"][l][None],
    )


def _trunk(x, mem, layers):
    B, S, _ = x.shape
    T = B * S
    x = x.reshape(T, D_MODEL)
    mem = mem.reshape(B * N_MEM, D_MODEL)
    for w in layers:
        x, xb = _ffn_ln(x, w["ff1_in"], w["ff1_out"], w["ln_g"][0:1], w["ln_b"][0:1])
        qt, k, vot = _qkvo(xb, w, S)
        mg, xy, gc, gr = _mgxy_gates(xb, w, S)
        hmt = _mlstm(qt, k, vot, gc, gr, w["mh_gain_half"], B, S)
        hr = _rglru(xy.reshape(S, B, D_RNN), w, B, S)
        kv = _mem_kv(mem, w["mem_ln_g"], w["mem_ln_b"], w["xa_wkv"])
        x = _merge_xattn(x, hmt, mg, hr.reshape(S, B * D_RNN), kv, w["w_pm"], w["w_pr"],
                         w["w_out"], w["xa_wq"], w["xa_wo"], w["ln_g"][1:3], w["ln_b"][1:3], S)
        x, _ = _ffn_ln(x, w["ff2_in"], w["ff2_out"], w["ln_g"][3:4], w["ln_b"][3:4])
    return x.reshape(B, S, D_MODEL)


def kernel(x_prompt, x_sample, mem_prompt, mem_sample, w_in, b_in, w_conv_qk, b_conv_qk, w_conv_r, b_conv_r,
           mh_gain, lru_wa, lru_ba, lru_wx, lru_bx, lru_lam, w_pm, w_pr, w_out, xa_wq, xa_wkv, xa_wo,
           mem_ln_g, mem_ln_b, ff1_in, ff1_out, ff2_in, ff2_out, ln_g, ln_b):
    p = dict(w_in=w_in, b_in=b_in, w_conv_qk=w_conv_qk, b_conv_qk=b_conv_qk, w_conv_r=w_conv_r,
             b_conv_r=b_conv_r, mh_gain=mh_gain, lru_wa=lru_wa, lru_ba=lru_ba, lru_wx=lru_wx,
             lru_bx=lru_bx, lru_lam=lru_lam, w_pm=w_pm, w_pr=w_pr, w_out=w_out, xa_wq=xa_wq,
             xa_wkv=xa_wkv, xa_wo=xa_wo, mem_ln_g=mem_ln_g, mem_ln_b=mem_ln_b, ff1_in=ff1_in,
             ff1_out=ff1_out, ff2_in=ff2_in, ff2_out=ff2_out, ln_g=ln_g, ln_b=ln_b)
    layers = [_prep_layer(p, l) for l in range(DEPTH)]
    return (_trunk(x_prompt, mem_prompt, layers), _trunk(x_sample, mem_sample, layers))
```

```python
import functools
import math

import jax
import jax.numpy as jnp
from jax import lax
from jax.experimental import pallas as pl
from jax.experimental.pallas import tpu as pltpu

F32 = jnp.float32
BF16 = jnp.bfloat16

D_MODEL = 1024
DEPTH = 2
N_MEM = 256
MLSTM_HEADS = 4
MLSTM_DK = D_MODEL // 4
MLSTM_DV = D_MODEL // 2
MLSTM_QK = MLSTM_HEADS * MLSTM_DK
MLSTM_V = MLSTM_HEADS * MLSTM_DV
D_RNN = D_MODEL
LRU_BLOCKS = 8
LRU_BW = D_RNN // LRU_BLOCKS
LRU_C = 8.0
CONV_W = 4
XATTN_HEADS = 4
XATTN_HD = D_MODEL // XATTN_HEADS
D_FF = 2816
DN_ALPHA = (2.0 * DEPTH) ** 0.25
LN_EPS = 1e-5

OFF_V = 2 * MLSTM_QK
OFF_O = OFF_V + MLSTM_V
OFF_GATE = OFF_O + MLSTM_V
OFF_XR = OFF_GATE + 4 * MLSTM_HEADS
OFF_YR = OFF_XR + D_RNN
OFF_MG = OFF_YR + D_RNN
D_IN = OFF_MG + 2 * D_MODEL
N_GATES = 4 * MLSTM_HEADS

VMEM_LIMIT_BYTES = 56 * 1024 * 1024
LANES = 128
SUBLANES = 8
BF16_ROWS = 16

TOKEN_TILE = 512
MGXY_TILE = 1024
MERGE_TILE = 512
MXU_DEPTH = 256
FF_CHUNKS = ((0, 6 * MXU_DEPTH), (6 * MXU_DEPTH, D_FF))
MLSTM_CHUNK = 256
MLSTM_ROWS = 2
LRU_TCHUNK = 128
LRU_GROUP = SUBLANES
LRU_UNROLL = 8
CONV_HALO = BF16_ROWS
QKVO_ROWS = 1024
MLSTM_RW = MLSTM_DV + LANES
LOG2E = math.log2(math.e)


def _dot(a, b):
    return jnp.dot(a, b, preferred_element_type=F32)


def _dot_nt(a, b):
    return lax.dot_general(a, b, (((1,), (1,)), ((), ())), preferred_element_type=F32)


def _dot_tn(a, b):
    return lax.dot_general(a, b, (((0,), (0,)), ((), ())), preferred_element_type=F32)


def _layer_norm(y, g, b):
    mu = jnp.mean(y, axis=-1, keepdims=True)
    d = y - mu
    var = jnp.mean(d * d, axis=-1, keepdims=True)
    return d * lax.rsqrt(var + LN_EPS) * g + b


def _sigmoid(x):
    return 0.5 + 0.5 * jnp.tanh(0.5 * x)


def _silu(x):
    h = 0.5 * x
    return h + h * jnp.tanh(h)


def _log_sigmoid(x):
    return jnp.minimum(x, 0.0) - jnp.log1p(jnp.exp(-jnp.abs(x)))


def _softplus(x):
    return jnp.maximum(x, 0.0) + jnp.log1p(jnp.exp(-jnp.abs(x)))


def _pack_bf16_pair(hi, lo):
    hi_bits = lax.bitcast_convert_type(hi.astype(BF16).astype(F32), jnp.uint32)
    lo_bits = lax.bitcast_convert_type(lo.astype(BF16).astype(F32), jnp.uint32)
    return hi_bits | (lo_bits >> 16)


def _unpack_bf16_pair(w):
    hi = lax.bitcast_convert_type(w & jnp.uint32(0xFFFF0000), F32)
    lo = lax.bitcast_convert_type(w << 16, F32)
    return hi, lo


def _split3(x):
    hi = x.astype(BF16)
    r = x - hi.astype(F32)
    mid = r.astype(BF16)
    lo = (r - mid.astype(F32)).astype(BF16)
    return hi, mid, lo


def _resident(shape):
    zeros = (0,) * len(shape)
    return pl.BlockSpec(shape, lambda *_: zeros, pipeline_mode=pl.Buffered(1))


def _params(*sem):
    return pltpu.CompilerParams(dimension_semantics=sem, vmem_limit_bytes=VMEM_LIMIT_BYTES)


def _ffn_ln_kernel(x_ref, win_ref, wout_ref, g_ref, b_ref, y_ref, yb_ref):
    half = x_ref.shape[0] // 2
    rows = [slice(0, half), slice(half, 2 * half)]
    ups = []
    for r in rows:
        xb = x_ref[r, :].astype(BF16)
        ups.append(([_dot(xb, win_ref[:, lo:hi]) for lo, hi in FF_CHUNKS],
                    [_dot(xb, win_ref[:, D_FF + lo:D_FF + hi]) for lo, hi in FF_CHUNKS]))
    for r, (gates, vals) in zip(rows, ups):
        acc = jnp.zeros((half, D_MODEL), F32)
        for (lo, hi), gate, up in zip(FF_CHUNKS, gates, vals):
            acc = acc + _dot((_silu(gate) * up).astype(BF16), wout_ref[lo:hi, :])
        y = _layer_norm(DN_ALPHA * x_ref[r, :] + 0.5 * acc, g_ref[...], b_ref[...])
        y_ref[r, :] = y
        yb_ref[r, :] = y.astype(BF16)


def _ffn_ln(x, w_in, w_out, g, b):
    T = x.shape[0]
    tm = TOKEN_TILE
    row = lambda i: (i, 0)
    return pl.pallas_call(
        _ffn_ln_kernel,
        out_shape=(jax.ShapeDtypeStruct((T, D_MODEL), F32), jax.ShapeDtypeStruct((T, D_MODEL), BF16)),
        grid=(T // tm,),
        in_specs=[pl.BlockSpec((tm, D_MODEL), row), _resident((D_MODEL, 2 * D_FF)),
                  _resident((D_FF, D_MODEL)), _resident((1, D_MODEL)), _resident((1, D_MODEL))],
        out_specs=(pl.BlockSpec((tm, D_MODEL), row), pl.BlockSpec((tm, D_MODEL), row)),
        compiler_params=_params("arbitrary"),
        name="ffn_ln",
    )(x, w_in, w_out, g, b)


def _qkvo_kernel(x_ref, xp_ref, xn_ref, wq_ref, bq_ref, cwq_ref, cbq_ref, wk_ref, bk_ref, cwk_ref, cbk_ref,
                 wvot_ref, bvot_ref, qt_ref, k_ref, vot_ref, xe_ref, xt_ref, *, tiles_per_seq):
    tm = x_ref.shape[0]
    H = CONV_HALO
    xt_ref[...] = x_ref[...].astype(F32).T.astype(BF16)
    xe_ref[0:H, :] = xp_ref[...]
    xe_ref[H:H + tm, :] = x_ref[...]
    xe_ref[H + tm:, :] = xn_ref[...]
    pos = pl.program_id(0) % tiles_per_seq
    first = pos == 0
    last = pos == tiles_per_seq - 1

    rows_ext = tm + 2 * H
    halo_row = lax.broadcasted_iota(jnp.int32, (rows_ext, 1), 0)
    outside = ((halo_row < H) & first) | ((halo_row >= H + tm) & last)

    def conv_silu(w_ref, b_ref, cw_ref, cb_ref):
        u = jnp.where(outside, 0.0, _dot(xe_ref[...], w_ref[...]) + b_ref[...])
        out = cb_ref[...] + u[H:H + tm, :] * cw_ref[1:2, :]
        for j, shift in ((0, 1), (2, rows_ext - 1), (3, rows_ext - 2)):
            out = out + pltpu.roll(u, shift, axis=0)[H:H + tm, :] * cw_ref[j:j + 1, :]
        return _silu(out)

    qt_ref[...] = (conv_silu(wq_ref, bq_ref, cwq_ref, cbq_ref) * (MLSTM_DK ** -0.5)).T.astype(qt_ref.dtype)
    k_ref[...] = conv_silu(wk_ref, bk_ref, cwk_ref, cbk_ref).astype(k_ref.dtype)
    for i in range(vot_ref.shape[0] // QKVO_ROWS):
        rows = slice(i * QKVO_ROWS, (i + 1) * QKVO_ROWS)
        vot_ref[rows, :] = (_dot(wvot_ref[rows, :], xt_ref[...]) + bvot_ref[rows, :]).astype(vot_ref.dtype)


def _qkvo(xb, w, seq_len):
    T = xb.shape[0]
    tm, H = TOKEN_TILE, CONV_HALO
    tiles_per_seq = seq_len // tm
    hpt = tm // H
    last_halo = T // H - 1
    conv_w = lambda: [_resident((D_MODEL, MLSTM_QK)), _resident((1, MLSTM_QK)),
                      _resident((CONV_W, MLSTM_QK)), _resident((1, MLSTM_QK))]
    return pl.pallas_call(
        functools.partial(_qkvo_kernel, tiles_per_seq=tiles_per_seq),
        out_shape=(jax.ShapeDtypeStruct((MLSTM_QK, T), BF16), jax.ShapeDtypeStruct((T, MLSTM_QK), BF16),
                   jax.ShapeDtypeStruct((2 * MLSTM_V, T), BF16)),
        grid=(T // tm,),
        in_specs=[pl.BlockSpec((tm, D_MODEL), lambda i: (i, 0)),
                  pl.BlockSpec((H, D_MODEL), lambda i: (jnp.maximum(i * hpt - 1, 0), 0)),
                  pl.BlockSpec((H, D_MODEL), lambda i: (jnp.minimum((i + 1) * hpt, last_halo), 0))]
        + conv_w() + conv_w() + [_resident((2 * MLSTM_V, D_MODEL)), _resident((2 * MLSTM_V, 1))],
        out_specs=(pl.BlockSpec((MLSTM_QK, tm), lambda i: (0, i)), pl.BlockSpec((tm, MLSTM_QK), lambda i: (i, 0)),
                   pl.BlockSpec((2 * MLSTM_V, tm), lambda i: (0, i))),
        scratch_shapes=[pltpu.VMEM((tm + 2 * H, D_MODEL), BF16), pltpu.VMEM((D_MODEL, tm), BF16)],
        compiler_params=_params("arbitrary"),
        name="qkvo",
    )(xb, xb, xb, w["w_q"], w["b_q"], w["cw_q"], w["cb_q"], w["w_k"], w["b_k"], w["cw_k"], w["cb_k"],
      w["w_vot"], w["b_vot"])


def _mgxy_gates_kernel(x_ref, wmg_ref, bmg_ref, wxy_ref, bxy_ref, w_ref, b_ref, mg_ref, xy_ref, gc_ref, gr_ref):
    L = MLSTM_CHUNK
    H = MLSTM_HEADS
    x = x_ref[...]
    g = _dot(x, w_ref[...]) + b_ref[...]
    ls = _log_sigmoid(g)
    pieces = [_split3(ls[k * L:(k + 1) * L, :]) for k in range(x.shape[0] // L)]
    mg_ref[...] = (_dot(x, wmg_ref[...]) + bmg_ref[...]).astype(mg_ref.dtype)
    row = lax.broadcasted_iota(jnp.int32, (L, L), 0)
    col = lax.broadcasted_iota(jnp.int32, (L, L), 1)
    tril = (col <= row).astype(BF16)
    pres = [sum(_dot(tril, p) for p in ps) for ps in pieces]
    xy = _dot(x, wxy_ref[...]) + bxy_ref[...]
    c_idx = lax.broadcasted_iota(jnp.int32, (L, N_GATES), 1)
    for k, pre in enumerate(pres):
        sl = slice(k * L, (k + 1) * L)
        suf = pre[L - 1:L, :] - pre + ls[sl, :]
        cum = jnp.where(c_idx >= 2 * H, suf, pre)
        gc = jnp.where(c_idx % (2 * H) >= H, cum, g[sl, :])
        gc_ref[sl, :] = gc
        wide = jnp.concatenate([gc, jnp.zeros((L, LANES - N_GATES), F32)], axis=1)
        gr_ref[:, sl] = wide.T[:N_GATES, :]
    xy_ref[...] = _pack_bf16_pair(xy[:, :D_RNN], xy[:, D_RNN:])


def _mgxy_gates(xb, w, seq_len):
    T = xb.shape[0]
    tm = MGXY_TILE
    tiles_per_seq = seq_len // tm
    row = lambda i: (i, 0)
    return pl.pallas_call(
        _mgxy_gates_kernel,
        out_shape=(jax.ShapeDtypeStruct((T, 2 * D_MODEL), BF16),
                   jax.ShapeDtypeStruct((seq_len, (T // seq_len) * D_RNN), jnp.uint32),
                   jax.ShapeDtypeStruct((T, N_GATES), F32), jax.ShapeDtypeStruct((N_GATES, T), F32)),
        grid=(T // tm,),
        in_specs=[pl.BlockSpec((tm, D_MODEL), row),
                  _resident((D_MODEL, 2 * D_MODEL)), _resident((1, 2 * D_MODEL)),
                  _resident((D_MODEL, 2 * D_RNN)), _resident((1, 2 * D_RNN)),
                  _resident((D_MODEL, N_GATES)), _resident((1, N_GATES))],
        out_specs=(pl.BlockSpec((tm, 2 * D_MODEL), row),
                   pl.BlockSpec((tm, D_RNN), lambda i: (i % tiles_per_seq, i // tiles_per_seq)),
                   pl.BlockSpec((tm, N_GATES), row), pl.BlockSpec((N_GATES, tm), lambda i: (0, i))),
        compiler_params=_params("arbitrary"),
        name="mgxy_gates",
    )(xb, w["w_mg"], w["b_mg"], w["w_xy"], w["b_xy"], w["w_g"], w["b_g"])


def _mlstm_kernel(*refs, reverse):
    R = MLSTM_ROWS
    per_row = 5 if reverse else 6
    row_refs = [refs[r * per_row:(r + 1) * per_row] for r in range(R)]
    rest = refs[R * per_row:]
    if reverse:
        out_ref, ct_ref, rt_ref, n_ref, m_ref = rest
    else:
        hbt_ref, gain_ref, out_ref, ct_ref, rt_ref, n_ref, m_ref = rest
    L = row_refs[0][1].shape[0]
    H, DK, DV = MLSTM_HEADS, MLSTM_DK, MLSTM_DV
    base = 2 * H if reverse else 0

    @pl.when(pl.program_id(1) == 0)
    def _():
        ct_ref[...] = jnp.zeros(ct_ref.shape, F32)
        n_ref[...] = jnp.zeros(n_ref.shape, F32)
        m_ref[...] = jnp.zeros(m_ref.shape, F32)

    key = lax.broadcasted_iota(jnp.int32, (L, L), 0)
    qry = lax.broadcasted_iota(jnp.int32, (L, L), 1)
    mask = (key >= qry) if reverse else (key <= qry)
    last = 0 if reverse else L - 1

    for r in range(R):
        qt_ref, k_ref, vt_ref, gc_ref, gr_ref = row_refs[r][:5]
        gc = gc_ref[:, base:base + 2 * H] * LOG2E
        gr = gr_ref[base:base + 2 * H, :] * LOG2E
        g_rows = [gr[H + h:H + h + 1, :] for h in range(H)]
        b_rows = [gr[h:h + 1, :] - gr[H + h:H + h + 1, :] for h in range(H)]
        b_cols = [gc[:, h:h + 1] - gc[:, H + h:H + h + 1] for h in range(H)]
        ms = [m_ref[r * H + h:r * H + h + 1, 0:1] for h in range(H)]
        qts = [qt_ref[h * DK:(h + 1) * DK, :] for h in range(H)]
        ks = [k_ref[:, h * DK:(h + 1) * DK] for h in range(H)]
        vts = [vt_ref[h * DV:(h + 1) * DV, :] for h in range(H)]

        sts = [_dot(ks[h], qts[h]) for h in range(H)]

        for h in range(H):
            u = r * H + h
            sl = slice(h * DV, (h + 1) * DV)
            log_d = jnp.where(mask, g_rows[h] + b_cols[h], -jnp.inf)
            log_inter = g_rows[h] + ms[h]
            m_t = jnp.maximum(log_inter, jnp.max(log_d, axis=0, keepdims=True))
            s = sts[h] * jnp.exp2(log_d - m_t)
            w_inter = jnp.exp2(log_inter - m_t)
            lhs = jnp.concatenate([s.astype(BF16), qts[h] * w_inter.astype(BF16)], axis=0)
            rt_ref[u, :DV, :L] = vts[h]
            rt_ref[u, DV:, :L] = jnp.ones((LANES, L), BF16)
            rt_ref[u, :DV, L:] = ct_ref[u].astype(BF16)
            rt_ref[u, DV:, L:] = jnp.broadcast_to(n_ref[u][0:1, :].astype(BF16), (LANES, DK))
            res = _dot(rt_ref[u], lhs)
            den = res[DV:DV + 1, :]
            hh = res[:DV, :] * (1.0 / jnp.maximum(jnp.abs(den), jnp.exp2(-m_t)))
            if reverse:
                out_ref[r, sl, :] = hh.astype(out_ref.dtype)
            else:
                ot_ref = row_refs[r][5]
                hh = hh + hbt_ref[r, sl, :].astype(F32)
                mu = jnp.mean(hh, axis=0, keepdims=True)
                d = hh - mu
                var = jnp.mean(d * d, axis=0, keepdims=True)
                hn = d * lax.rsqrt(var + LN_EPS) * gain_ref[sl, :]
                out_ref[r, sl, :] = (hn + hn * jnp.tanh(0.5 * ot_ref[sl, :].astype(F32))).astype(out_ref.dtype)

        for h in range(H):
            u = r * H + h
            g_tot = g_rows[h][:, last:last + 1]
            m_new = jnp.maximum(g_tot + ms[h], jnp.max(g_tot + b_rows[h], axis=-1, keepdims=True))
            wk = ks[h].astype(F32) * jnp.exp2(g_tot + b_cols[h] - m_new)
            decay = jnp.exp2(g_tot + ms[h] - m_new)
            ct_ref[u] = decay * ct_ref[u] + _dot(vts[h], wk.astype(BF16))
            n_ref[u] = decay * n_ref[u] + jnp.sum(wk, axis=0, keepdims=True)
            m_ref[u:u + 1, :] = jnp.broadcast_to(m_new, (1, m_ref.shape[1]))


def _mlstm_dir(qt, k, vot, gc, gr, hbt, gain_half, batch, seq_len, reverse):
    L, R, H = MLSTM_CHUNK, MLSTM_ROWS, MLSTM_HEADS
    nc = seq_len // L
    chunk = (lambda c: nc - 1 - c) if reverse else (lambda c: c)
    in_specs, args = [], []
    for r in range(R):
        pos = lambda b, c, r=r: (b * R + r) * nc + chunk(c)
        cols = lambda height, blk, pos=pos: pl.BlockSpec((height, L), lambda b, c: (blk, pos(b, c)))
        in_specs += [cols(MLSTM_QK, 0),
                     pl.BlockSpec((L, MLSTM_QK), lambda b, c, pos=pos: (pos(b, c), 0)),
                     cols(MLSTM_V, 0),
                     pl.BlockSpec((L, N_GATES), lambda b, c, pos=pos: (pos(b, c), 0)),
                     cols(N_GATES, 0)]
        args += [qt, k, vot, gc, gr]
        if not reverse:
            in_specs += [cols(MLSTM_V, 1)]
            args += [vot]
    slab = pl.BlockSpec((R, MLSTM_V, L), lambda b, c: (b, 0, chunk(c)))
    if not reverse:
        in_specs += [slab, _resident((MLSTM_V, L))]
        args += [hbt, gain_half]
    return pl.pallas_call(
        functools.partial(_mlstm_kernel, reverse=reverse),
        out_shape=jax.ShapeDtypeStruct((batch, MLSTM_V, seq_len), BF16),
        grid=(batch // R, nc),
        in_specs=in_specs,
        out_specs=slab,
        scratch_shapes=[pltpu.VMEM((R * H, MLSTM_DV, MLSTM_DK), F32),
                        pltpu.VMEM((R * H, MLSTM_RW, L + MLSTM_DK), BF16),
                        pltpu.VMEM((R * H, SUBLANES, MLSTM_DK), F32),
                        pltpu.VMEM((R * H, LANES), F32)],
        compiler_params=_params("arbitrary", "arbitrary"),
        name="mlstm_bwd" if reverse else "mlstm_fwd",
    )(*args)


def _mlstm(qt, k, vot, gc, gr, gain_half, batch, seq_len):
    hbt = _mlstm_dir(qt, k, vot, gc, gr, None, None, batch, seq_len, True)
    return _mlstm_dir(qt, k, vot, gc, gr, hbt, gain_half, batch, seq_len, False)


def _rglru_kernel(*refs, nc, reverse):
    if reverse:
        x_ref, xp_ref, xn_ref, cw_ref, cb_ref, wa_ref, ba_ref, wx_ref, bx_ref, lam_ref, o_ref, \
            a_ref, u_ref, state_ref, k_ref = refs
    else:
        x_ref, xp_ref, xn_ref, hb_ref, cw_ref, cb_ref, wa_ref, ba_ref, wx_ref, bx_ref, lam_ref, \
            o_ref, a_ref, u_ref, state_ref, k_ref, gy_ref = refs
    tc = x_ref.shape[0]
    c = pl.program_id(1)
    cc = nc - 1 - c if reverse else c

    @pl.when(c == 0)
    def _():
        state_ref[...] = jnp.zeros(state_ref.shape, F32)
        k_ref[...] = (-0.5 * LRU_C * LOG2E) * _softplus(-lam_ref[...])

    xr, yr = _unpack_bf16_pair(x_ref[...])
    prev = jnp.where(cc > 0, _unpack_bf16_pair(xp_ref[...])[0], 0.0)
    nxt = jnp.where(cc < nc - 1, _unpack_bf16_pair(xn_ref[...])[0], 0.0)
    xe = jnp.concatenate([prev, xr, nxt], axis=0)
    xc = cb_ref[...]
    for j in range(CONV_W):
        xc = xc + xe[j:j + tc] * cw_ref[j:j + 1, :]

    x2 = xc.reshape(tc * LRU_GROUP, D_RNN)
    xb = x2.astype(BF16)
    for n in range(LRU_BLOCKS):
        sl = slice(n * LRU_BW, (n + 1) * LRU_BW)
        t_a = jnp.tanh(_dot(xb[:, sl], wa_ref[n]) + ba_ref[:, sl])
        t_i = jnp.tanh(_dot(xb[:, sl], wx_ref[n]) + bx_ref[:, sl])
        a = jnp.exp2(k_ref[:, sl] + k_ref[:, sl] * t_a)
        y = 1.0 - a * a
        root = jnp.where(y > 0.0, y * lax.rsqrt(y), 0.0)
        hx = 0.5 * x2[:, sl]
        u = root * (hx + hx * t_i)
        a_ref[:, :, sl] = a.reshape(tc, LRU_GROUP, LRU_BW)
        u_ref[:, :, sl] = u.reshape(tc, LRU_GROUP, LRU_BW)

    if reverse:
        def step(k, h):
            t = tc - 1 - k
            h = a_ref[t] * h + u_ref[t]
            o_ref[t] = _pack_bf16_pair(h[:, :D_RNN // 2], h[:, D_RNN // 2:])
            return h
    else:
        gy_ref[...] = jax.nn.gelu(yr)

        def step(t, h):
            h = a_ref[t] * h + u_ref[t]
            hb = jnp.concatenate(_unpack_bf16_pair(hb_ref[t]), axis=1)
            o_ref[t] = (h + hb) * gy_ref[t]
            return h

    state_ref[...] = lax.fori_loop(0, tc, step, state_ref[...], unroll=LRU_UNROLL)


def _rglru_dir(xy, hb, cw, cb, wa, ba, wx, bx, lam, batch, seq_len, reverse):
    tc, G, C = LRU_TCHUNK, LRU_GROUP, D_RNN
    nc = seq_len // tc
    chunk = (lambda c: nc - 1 - c) if reverse else (lambda c: c)
    blk = lambda width: pl.BlockSpec((tc, G, width), lambda g, c: (chunk(c), g, 0))
    in_specs = [blk(C),
                pl.BlockSpec((1, G, C), lambda g, c: (jnp.maximum(chunk(c) * tc - 1, 0), g, 0)),
                pl.BlockSpec((2, G, C),
                             lambda g, c: (jnp.minimum((chunk(c) + 1) * (tc // 2), seq_len // 2 - 1), g, 0))]
    args = [xy, xy, xy]
    scratch = [pltpu.VMEM((tc, G, C), F32), pltpu.VMEM((tc, G, C), F32),
               pltpu.VMEM((G, C), F32), pltpu.VMEM((1, C), F32)]
    if not reverse:
        in_specs += [blk(C // 2)]
        args += [hb]
        scratch += [pltpu.VMEM((tc, G, C), F32)]
    in_specs += [_resident((CONV_W, C)), _resident((1, C)),
                 _resident((LRU_BLOCKS, LRU_BW, LRU_BW)), _resident((1, C)),
                 _resident((LRU_BLOCKS, LRU_BW, LRU_BW)), _resident((1, C)), _resident((1, C))]
    args += [cw, cb, wa, ba, wx, bx, lam]
    return pl.pallas_call(
        functools.partial(_rglru_kernel, nc=nc, reverse=reverse),
        out_shape=jax.ShapeDtypeStruct((seq_len, batch, C // 2), jnp.uint32) if reverse
        else jax.ShapeDtypeStruct((seq_len, batch, C), F32),
        grid=(batch // G, nc),
        in_specs=in_specs,
        out_specs=blk(C // 2) if reverse else blk(C),
        scratch_shapes=scratch,
        compiler_params=_params("arbitrary", "arbitrary"),
        name="rglru_bwd" if reverse else "rglru_fwd",
    )(*args)


def _rglru(xy, w, batch, seq_len):
    common = (w["cw_r"], w["cb_r"])
    hb = _rglru_dir(xy, None, *common, w["lru_wa"][1], w["lru_ba"][1:2], w["lru_wx"][1], w["lru_bx"][1:2],
                    w["lru_lam"][1:2], batch, seq_len, True)
    return _rglru_dir(xy, hb, *common, w["lru_wa"][0], w["lru_ba"][0:1], w["lru_wx"][0], w["lru_bx"][0:1],
                      w["lru_lam"][0:1], batch, seq_len, False)


def _kv_kernel(m_ref, g_ref, b_ref, w_ref, o_ref):
    m = _layer_norm(m_ref[...], g_ref[...], b_ref[...])
    o_ref[...] = _dot(m.astype(BF16), w_ref[...]).astype(o_ref.dtype)


def _mem_kv(mem, g, b, w_kv):
    R = mem.shape[0]
    return pl.pallas_call(
        _kv_kernel,
        out_shape=jax.ShapeDtypeStruct((R, 2 * D_MODEL), BF16),
        grid=(R // N_MEM,),
        in_specs=[pl.BlockSpec((N_MEM, D_MODEL), lambda i: (i, 0)), _resident((1, D_MODEL)),
                  _resident((1, D_MODEL)), _resident((D_MODEL, 2 * D_MODEL))],
        out_specs=pl.BlockSpec((N_MEM, 2 * D_MODEL), lambda i: (i, 0)),
        compiler_params=_params("arbitrary"),
        name="mem_kv",
    )(mem, g, b, w_kv)


def _merge_xattn_kernel(x_ref, hmt_ref, mg_ref, hr_ref, kv_ref,
                        wpm_ref, wpr_ref, wout_ref, wq_ref, wo_ref, lng_ref, lnb_ref, y_ref):
    half = x_ref.shape[0] // 2
    rows = [slice(0, half), slice(half, 2 * half)]
    pr = [_dot(hr_ref[r, :].astype(BF16), wpr_ref[...]) for r in rows]
    pm = [_dot_tn(hmt_ref[:, r], wpm_ref[...]) for r in rows]
    merged = []
    for r, a, b in zip(rows, pm, pr):
        g_m = _sigmoid(mg_ref[r, :D_MODEL].astype(F32))
        g_r = _sigmoid(mg_ref[r, D_MODEL:].astype(F32))
        merged.append((g_m * a + g_r * b).astype(BF16))
    mix = [_dot(m, wout_ref[...]) for m in merged]
    xs = [_layer_norm(DN_ALPHA * x_ref[r, :] + m, lng_ref[0:1, :], lnb_ref[0:1, :]) for r, m in zip(rows, mix)]

    qs = [_dot(x.astype(BF16), wq_ref[...]).astype(BF16) for x in xs]
    heads = [[], []]
    for h in range(XATTN_HEADS):
        sl = slice(h * XATTN_HD, (h + 1) * XATTN_HD)
        ss = [_dot_nt(q[:, sl], kv_ref[:, sl]) * (XATTN_HD ** -0.5) for q in qs]
        for i, s in enumerate(ss):
            e = jnp.exp(s - jnp.max(s, axis=-1, keepdims=True))
            p = e * (1.0 / jnp.sum(e, axis=-1, keepdims=True))
            oh = _dot(p.astype(BF16), kv_ref[:, D_MODEL + h * XATTN_HD:D_MODEL + (h + 1) * XATTN_HD])
            heads[i].append(oh.astype(BF16))
    for r, x, ohs in zip(rows, xs, heads):
        xa = _dot(jnp.concatenate(ohs, axis=1), wo_ref[...])
        y_ref[r, :] = _layer_norm(DN_ALPHA * x + xa, lng_ref[1:2, :], lnb_ref[1:2, :])


def _merge_xattn(x, hmt, mg, hr, kv, wpm, wpr, wout, wq, wo, lng, lnb, seq_len):
    T = x.shape[0]
    tm = MERGE_TILE
    tiles_per_seq = seq_len // tm
    row = lambda i: (i, 0)
    return pl.pallas_call(
        _merge_xattn_kernel,
        out_shape=jax.ShapeDtypeStruct((T, D_MODEL), F32),
        grid=(T // tm,),
        in_specs=[pl.BlockSpec((tm, D_MODEL), row),
                  pl.BlockSpec((None, MLSTM_V, tm), lambda i: (i // tiles_per_seq, 0, i % tiles_per_seq)),
                  pl.BlockSpec((tm, 2 * D_MODEL), row),
                  pl.BlockSpec((tm, D_RNN), lambda i: (i % tiles_per_seq, i // tiles_per_seq)),
                  pl.BlockSpec((N_MEM, 2 * D_MODEL), lambda i: (i // tiles_per_seq, 0)),
                  _resident((MLSTM_V, D_MODEL)), _resident((D_RNN, D_MODEL)),
                  _resident((D_MODEL, D_MODEL)), _resident((D_MODEL, D_MODEL)),
                  _resident((D_MODEL, D_MODEL)), _resident((2, D_MODEL)), _resident((2, D_MODEL))],
        out_specs=pl.BlockSpec((tm, D_MODEL), row),
        compiler_params=_params("arbitrary"),
        name="merge_xattn",
    )(x, hmt, mg, hr, kv, wpm, wpr, wout, wq, wo, lng, lnb)


def _prep_layer(p, l):
    w_in, b_in = p["w_in"][l], p["b_in"][l]
    cols = lambda lo, hi: w_in[:, lo:hi]
    bias = lambda lo, hi: b_in[lo:hi]
    return dict(
        ff1_in=p["ff1_in"][l].astype(BF16), ff1_out=p["ff1_out"][l].astype(BF16),
        ff2_in=p["ff2_in"][l].astype(BF16), ff2_out=p["ff2_out"][l].astype(BF16),
        ln_g=p["ln_g"][l], ln_b=p["ln_b"][l],
        w_q=cols(0, MLSTM_QK).astype(BF16), b_q=bias(0, MLSTM_QK)[None],
        cw_q=p["w_conv_qk"][l][:, :MLSTM_QK], cb_q=p["b_conv_qk"][l][None, :MLSTM_QK],
        w_k=cols(MLSTM_QK, OFF_V).astype(BF16), b_k=bias(MLSTM_QK, OFF_V)[None],
        cw_k=p["w_conv_qk"][l][:, MLSTM_QK:], cb_k=p["b_conv_qk"][l][None, MLSTM_QK:],
        w_vot=cols(OFF_V, OFF_GATE).astype(BF16).T, b_vot=bias(OFF_V, OFF_GATE)[:, None],
        w_mg=cols(OFF_MG, D_IN).astype(BF16), b_mg=bias(OFF_MG, D_IN)[None],
        w_xy=cols(OFF_XR, OFF_MG).astype(BF16), b_xy=bias(OFF_XR, OFF_MG)[None],
        w_g=cols(OFF_GATE, OFF_XR).astype(BF16), b_g=bias(OFF_GATE, OFF_XR)[None],
        cw_r=p["w_conv_r"][l], cb_r=p["b_conv_r"][l][None],
        lru_wa=(0.5 * p["lru_wa"][l]).astype(BF16), lru_ba=0.5 * p["lru_ba"][l],
        lru_wx=(0.5 * p["lru_wx"][l]).astype(BF16), lru_bx=0.5 * p["lru_bx"][l], lru_lam=p["lru_lam"][l],
        mh_gain_half=jnp.broadcast_to(0.5 * p["mh_gain"][l][:, None], (MLSTM_V, MLSTM_CHUNK)),
        w_pm=p["w_pm"][l].astype(BF16), w_pr=p["w_pr"][l].astype(BF16), w_out=p["w_out"][l].astype(BF16),
        xa_wq=p["xa_wq"][l].astype(BF16), xa_wkv=p["xa_wkv"][l].astype(BF16), xa_wo=p["xa_wo"][l].astype(BF16),
        mem_ln_g=p["mem_ln_g"][l][None], mem_ln_b=p["mem_ln_b"][l][None],
    )


def _trunk(x, mem, layers):
    B, S, _ = x.shape
    T = B * S
    x = x.reshape(T, D_MODEL)
    mem = mem.reshape(B * N_MEM, D_MODEL)
    for w in layers:
        x, xb = _ffn_ln(x, w["ff1_in"], w["ff1_out"], w["ln_g"][0:1], w["ln_b"][0:1])
        qt, k, vot = _qkvo(xb, w, S)
        mg, xy, gc, gr = _mgxy_gates(xb, w, S)
        hmt = _mlstm(qt, k, vot, gc, gr, w["mh_gain_half"], B, S)
        hr = _rglru(xy.reshape(S, B, D_RNN), w, B, S)
        kv = _mem_kv(mem, w["mem_ln_g"], w["mem_ln_b"], w["xa_wkv"])
        x = _merge_xattn(x, hmt, mg, hr.reshape(S, B * D_RNN), kv, w["w_pm"], w["w_pr"],
                         w["w_out"], w["xa_wq"], w["xa_wo"], w["ln_g"][1:3], w["ln_b"][1:3], S)
        x, _ = _ffn_ln(x, w["ff2_in"], w["ff2_out"], w["ln_g"][3:4], w["ln_b"][3:4])
    return x.reshape(B, S, D_MODEL)


def kernel(x_prompt, x_sample, mem_prompt, mem_sample, w_in, b_in, w_conv_qk, b_conv_qk, w_conv_r, b_conv_r,
           mh_gain, lru_wa, lru_ba, lru_wx, lru_bx, lru_lam, w_pm, w_pr, w_out, xa_wq, xa_wkv, xa_wo,
           mem_ln_g, mem_ln_b, ff1_in, ff1_out, ff2_in, ff2_out, ln_g, ln_b):
    p = dict(w_in=w_in, b_in=b_in, w_conv_qk=w_conv_qk, b_conv_qk=b_conv_qk, w_conv_r=w_conv_r,
             b_conv_r=b_conv_r, mh_gain=mh_gain, lru_wa=lru_wa, lru_ba=lru_ba, lru_wx=lru_wx,
             lru_bx=lru_bx, lru_lam=lru_lam, w_pm=w_pm, w_pr=w_pr, w_out=w_out, xa_wq=xa_wq,
             xa_wkv=xa_wkv, xa_wo=xa_wo, mem_ln_g=mem_ln_g, mem_ln_b=mem_ln_b, ff1_in=ff1_in,
             ff1_out=ff1_out, ff2_in=ff2_in, ff2_out=ff2_out, ln_g=ln_g, ln_b=ln_b)
    layers = [_prep_layer(p, l) for l in range(DEPTH)]
    return (_trunk(x_prompt, mem_prompt, layers), _trunk(x_sample, mem_sample, layers))
```

```python
import functools
import math

import jax
import jax.numpy as jnp
from jax import lax
from jax.experimental import pallas as pl
from jax.experimental.pallas import tpu as pltpu

F32 = jnp.float32
BF16 = jnp.bfloat16

D_MODEL = 1024
DEPTH = 2
N_MEM = 256
MLSTM_HEADS = 4
MLSTM_DK = D_MODEL // 4
MLSTM_DV = D_MODEL // 2
MLSTM_QK = MLSTM_HEADS * MLSTM_DK
MLSTM_V = MLSTM_HEADS * MLSTM_DV
D_RNN = D_MODEL
LRU_BLOCKS = 8
LRU_BW = D_RNN // LRU_BLOCKS
LRU_C = 8.0
CONV_W = 4
XATTN_HEADS = 4
XATTN_HD = D_MODEL // XATTN_HEADS
D_FF = 2816
DN_ALPHA = (2.0 * DEPTH) ** 0.25
LN_EPS = 1e-5

OFF_V = 2 * MLSTM_QK
OFF_O = OFF_V + MLSTM_V
OFF_GATE = OFF_O + MLSTM_V
OFF_XR = OFF_GATE + 4 * MLSTM_HEADS
OFF_YR = OFF_XR + D_RNN
OFF_MG = OFF_YR + D_RNN
D_IN = OFF_MG + 2 * D_MODEL
N_GATES = 4 * MLSTM_HEADS

VMEM_LIMIT_BYTES = 56 * 1024 * 1024
LANES = 128
SUBLANES = 8
BF16_ROWS = 16

TOKEN_TILE = 512
MGXY_TILE = 1024
MERGE_TILE = 512
MXU_DEPTH = 256
FF_CHUNKS = ((0, 6 * MXU_DEPTH), (6 * MXU_DEPTH, D_FF))
MLSTM_CHUNK = 256
MLSTM_ROWS = 2
LRU_TCHUNK = 128
LRU_GROUP = SUBLANES
LRU_UNROLL = 8
CONV_HALO = BF16_ROWS
QKVO_ROWS = 1024
MLSTM_RW = MLSTM_DV + LANES
LOG2E = math.log2(math.e)


def _dot(a, b):
    return jnp.dot(a, b, preferred_element_type=F32)


def _dot_nt(a, b):
    return lax.dot_general(a, b, (((1,), (1,)), ((), ())), preferred_element_type=F32)


def _dot_tn(a, b):
    return lax.dot_general(a, b, (((0,), (0,)), ((), ())), preferred_element_type=F32)


def _layer_norm(y, g, b):
    mu = jnp.mean(y, axis=-1, keepdims=True)
    d = y - mu
    var = jnp.mean(d * d, axis=-1, keepdims=True)
    return d * lax.rsqrt(var + LN_EPS) * g + b


def _sigmoid(x):
    return 0.5 + 0.5 * jnp.tanh(0.5 * x)


def _silu(x):
    h = 0.5 * x
    return h + h * jnp.tanh(h)


def _log_sigmoid(x):
    return jnp.minimum(x, 0.0) - jnp.log1p(jnp.exp(-jnp.abs(x)))


def _softplus(x):
    return jnp.maximum(x, 0.0) + jnp.log1p(jnp.exp(-jnp.abs(x)))


def _pack_bf16_pair(hi, lo):
    hi_bits = lax.bitcast_convert_type(hi.astype(BF16).astype(F32), jnp.uint32)
    lo_bits = lax.bitcast_convert_type(lo.astype(BF16).astype(F32), jnp.uint32)
    return hi_bits | (lo_bits >> 16)


def _unpack_bf16_pair(w):
    hi = lax.bitcast_convert_type(w & jnp.uint32(0xFFFF0000), F32)
    lo = lax.bitcast_convert_type(w << 16, F32)
    return hi, lo


def _split3(x):
    hi = x.astype(BF16)
    r = x - hi.astype(F32)
    mid = r.astype(BF16)
    lo = (r - mid.astype(F32)).astype(BF16)
    return hi, mid, lo


def _resident(shape):
    zeros = (0,) * len(shape)
    return pl.BlockSpec(shape, lambda *_: zeros, pipeline_mode=pl.Buffered(1))


def _params(*sem):
    return pltpu.CompilerParams(dimension_semantics=sem, vmem_limit_bytes=VMEM_LIMIT_BYTES)


def _ffn_ln_kernel(x_ref, win_ref, wout_ref, g_ref, b_ref, y_ref, yb_ref=None):
    half = x_ref.shape[0] // 2
    rows = [slice(0, half), slice(half, 2 * half)]
    ups = []
    for r in rows:
        xb = x_ref[r, :].astype(BF16)
        ups.append(([_dot(xb, win_ref[:, lo:hi]) for lo, hi in FF_CHUNKS],
                    [_dot(xb, win_ref[:, D_FF + lo:D_FF + hi]) for lo, hi in FF_CHUNKS]))
    for r, (gates, vals) in zip(rows, ups):
        acc = jnp.zeros((half, D_MODEL), F32)
        for (lo, hi), gate, up in zip(FF_CHUNKS, gates, vals):
            acc = acc + _dot((_silu(gate) * up).astype(BF16), wout_ref[lo:hi, :])
        y = _layer_norm(DN_ALPHA * x_ref[r, :] + 0.5 * acc, g_ref[...], b_ref[...])
        y_ref[r, :] = y
        if yb_ref is not None:
            yb_ref[r, :] = y.astype(BF16)


def _ffn_ln(x, w_in, w_out, g, b, with_bf16_copy):
    T = x.shape[0]
    tm = TOKEN_TILE
    row = lambda i: (i, 0)
    dtypes = (F32, BF16) if with_bf16_copy else (F32,)
    return pl.pallas_call(
        _ffn_ln_kernel,
        out_shape=tuple(jax.ShapeDtypeStruct((T, D_MODEL), d) for d in dtypes),
        grid=(T // tm,),
        in_specs=[pl.BlockSpec((tm, D_MODEL), row), _resident((D_MODEL, 2 * D_FF)),
                  _resident((D_FF, D_MODEL)), _resident((1, D_MODEL)), _resident((1, D_MODEL))],
        out_specs=tuple(pl.BlockSpec((tm, D_MODEL), row) for _ in dtypes),
        compiler_params=_params("arbitrary"),
        name="ffn_ln",
    )(x, w_in, w_out, g, b)


def _qkvo_kernel(x_ref, xp_ref, xn_ref, wq_ref, bq_ref, cwq_ref, cbq_ref, wk_ref, bk_ref, cwk_ref, cbk_ref,
                 wvot_ref, bvot_ref, qt_ref, k_ref, vot_ref, xe_ref, xt_ref, *, tiles_per_seq):
    tm = x_ref.shape[0]
    H = CONV_HALO
    xt_ref[...] = x_ref[...].astype(F32).T.astype(BF16)
    xe_ref[0:H, :] = xp_ref[...]
    xe_ref[H:H + tm, :] = x_ref[...]
    xe_ref[H + tm:, :] = xn_ref[...]
    pos = pl.program_id(0) % tiles_per_seq
    first = pos == 0
    last = pos == tiles_per_seq - 1

    rows_ext = tm + 2 * H
    halo_row = lax.broadcasted_iota(jnp.int32, (rows_ext, 1), 0)
    outside = ((halo_row < H) & first) | ((halo_row >= H + tm) & last)

    def conv_silu(w_ref, b_ref, cw_ref, cb_ref):
        u = jnp.where(outside, 0.0, _dot(xe_ref[...], w_ref[...]) + b_ref[...])
        out = cb_ref[...] + u[H:H + tm, :] * cw_ref[1:2, :]
        for j, shift in ((0, 1), (2, rows_ext - 1), (3, rows_ext - 2)):
            out = out + pltpu.roll(u, shift, axis=0)[H:H + tm, :] * cw_ref[j:j + 1, :]
        return _silu(out)

    qt_ref[...] = (conv_silu(wq_ref, bq_ref, cwq_ref, cbq_ref) * (MLSTM_DK ** -0.5)).T.astype(qt_ref.dtype)
    k_ref[...] = conv_silu(wk_ref, bk_ref, cwk_ref, cbk_ref).astype(k_ref.dtype)
    for i in range(vot_ref.shape[0] // QKVO_ROWS):
        rows = slice(i * QKVO_ROWS, (i + 1) * QKVO_ROWS)
        vot_ref[rows, :] = (_dot(wvot_ref[rows, :], xt_ref[...]) + bvot_ref[rows, :]).astype(vot_ref.dtype)


def _qkvo(xb, w, seq_len):
    T = xb.shape[0]
    tm, H = TOKEN_TILE, CONV_HALO
    tiles_per_seq = seq_len // tm
    hpt = tm // H
    last_halo = T // H - 1
    conv_w = lambda: [_resident((D_MODEL, MLSTM_QK)), _resident((1, MLSTM_QK)),
                      _resident((CONV_W, MLSTM_QK)), _resident((1, MLSTM_QK))]
    return pl.pallas_call(
        functools.partial(_qkvo_kernel, tiles_per_seq=tiles_per_seq),
        out_shape=(jax.ShapeDtypeStruct((MLSTM_QK, T), BF16), jax.ShapeDtypeStruct((T, MLSTM_QK), BF16),
                   jax.ShapeDtypeStruct((2 * MLSTM_V, T), BF16)),
        grid=(T // tm,),
        in_specs=[pl.BlockSpec((tm, D_MODEL), lambda i: (i, 0)),
                  pl.BlockSpec((H, D_MODEL), lambda i: (jnp.maximum(i * hpt - 1, 0), 0)),
                  pl.BlockSpec((H, D_MODEL), lambda i: (jnp.minimum((i + 1) * hpt, last_halo), 0))]
        + conv_w() + conv_w() + [_resident((2 * MLSTM_V, D_MODEL)), _resident((2 * MLSTM_V, 1))],
        out_specs=(pl.BlockSpec((MLSTM_QK, tm), lambda i: (0, i)), pl.BlockSpec((tm, MLSTM_QK), lambda i: (i, 0)),
                   pl.BlockSpec((2 * MLSTM_V, tm), lambda i: (0, i))),
        scratch_shapes=[pltpu.VMEM((tm + 2 * H, D_MODEL), BF16), pltpu.VMEM((D_MODEL, tm), BF16)],
        compiler_params=_params("arbitrary"),
        name="qkvo",
    )(xb, xb, xb, w["w_q"], w["b_q"], w["cw_q"], w["cb_q"], w["w_k"], w["b_k"], w["cw_k"], w["cb_k"],
      w["w_vot"], w["b_vot"])


def _mgxy_gates_kernel(x_ref, wmg_ref, bmg_ref, wxy_ref, bxy_ref, w_ref, b_ref, mg_ref, xy_ref, gc_ref, gr_ref):
    L = MLSTM_CHUNK
    H = MLSTM_HEADS
    x = x_ref[...]
    g = _dot(x, w_ref[...]) + b_ref[...]
    ls = _log_sigmoid(g)
    pieces = [_split3(ls[k * L:(k + 1) * L, :]) for k in range(x.shape[0] // L)]
    mg_ref[...] = (_dot(x, wmg_ref[...]) + bmg_ref[...]).astype(mg_ref.dtype)
    row = lax.broadcasted_iota(jnp.int32, (L, L), 0)
    col = lax.broadcasted_iota(jnp.int32, (L, L), 1)
    tril = (col <= row).astype(BF16)
    pres = [sum(_dot(tril, p) for p in ps) for ps in pieces]
    xy = _dot(x, wxy_ref[...]) + bxy_ref[...]
    c_idx = lax.broadcasted_iota(jnp.int32, (L, N_GATES), 1)
    for k, pre in enumerate(pres):
        sl = slice(k * L, (k + 1) * L)
        suf = pre[L - 1:L, :] - pre + ls[sl, :]
        cum = jnp.where(c_idx >= 2 * H, suf, pre)
        gc = jnp.where(c_idx % (2 * H) >= H, cum, g[sl, :])
        gc_ref[sl, :] = gc
        wide = jnp.concatenate([gc, jnp.zeros((L, LANES - N_GATES), F32)], axis=1)
        gr_ref[:, sl] = wide.T[:N_GATES, :]
    xy_ref[...] = _pack_bf16_pair(xy[:, :D_RNN], xy[:, D_RNN:])


def _mgxy_gates(xb, w, seq_len):
    T = xb.shape[0]
    tm = MGXY_TILE
    tiles_per_seq = seq_len // tm
    row = lambda i: (i, 0)
    return pl.pallas_call(
        _mgxy_gates_kernel,
        out_shape=(jax.ShapeDtypeStruct((T, 2 * D_MODEL), BF16),
                   jax.ShapeDtypeStruct((seq_len, (T // seq_len) * D_RNN), jnp.uint32),
                   jax.ShapeDtypeStruct((T, N_GATES), F32), jax.ShapeDtypeStruct((N_GATES, T), F32)),
        grid=(T // tm,),
        in_specs=[pl.BlockSpec((tm, D_MODEL), row),
                  _resident((D_MODEL, 2 * D_MODEL)), _resident((1, 2 * D_MODEL)),
                  _resident((D_MODEL, 2 * D_RNN)), _resident((1, 2 * D_RNN)),
                  _resident((D_MODEL, N_GATES)), _resident((1, N_GATES))],
        out_specs=(pl.BlockSpec((tm, 2 * D_MODEL), row),
                   pl.BlockSpec((tm, D_RNN), lambda i: (i % tiles_per_seq, i // tiles_per_seq)),
                   pl.BlockSpec((tm, N_GATES), row), pl.BlockSpec((N_GATES, tm), lambda i: (0, i))),
        compiler_params=_params("arbitrary"),
        name="mgxy_gates",
    )(xb, w["w_mg"], w["b_mg"], w["w_xy"], w["b_xy"], w["w_g"], w["b_g"])


def _mlstm_kernel(*refs, reverse):
    R = MLSTM_ROWS
    per_row = 5 if reverse else 6
    row_refs = [refs[r * per_row:(r + 1) * per_row] for r in range(R)]
    rest = refs[R * per_row:]
    if reverse:
        out_ref, ct_ref, rt_ref, n_ref, m_ref = rest
    else:
        hbt_ref, gain_ref, out_ref, ct_ref, rt_ref, n_ref, m_ref = rest
    L = row_refs[0][1].shape[0]
    H, DK, DV = MLSTM_HEADS, MLSTM_DK, MLSTM_DV
    base = 2 * H if reverse else 0

    @pl.when(pl.program_id(1) == 0)
    def _():
        ct_ref[...] = jnp.zeros(ct_ref.shape, F32)
        n_ref[...] = jnp.zeros(n_ref.shape, F32)
        m_ref[...] = jnp.zeros(m_ref.shape, F32)

    key = lax.broadcasted_iota(jnp.int32, (L, L), 0)
    qry = lax.broadcasted_iota(jnp.int32, (L, L), 1)
    mask = (key >= qry) if reverse else (key <= qry)
    last = 0 if reverse else L - 1

    for r in range(R):
        qt_ref, k_ref, vt_ref, gc_ref, gr_ref = row_refs[r][:5]
        gc = gc_ref[:, base:base + 2 * H] * LOG2E
        gr = gr_ref[base:base + 2 * H, :] * LOG2E
        g_rows = [gr[H + h:H + h + 1, :] for h in range(H)]
        b_rows = [gr[h:h + 1, :] - gr[H + h:H + h + 1, :] for h in range(H)]
        b_cols = [gc[:, h:h + 1] - gc[:, H + h:H + h + 1] for h in range(H)]
        ms = [m_ref[r * H + h:r * H + h + 1, 0:1] for h in range(H)]
        qts = [qt_ref[h * DK:(h + 1) * DK, :] for h in range(H)]
        ks = [k_ref[:, h * DK:(h + 1) * DK] for h in range(H)]
        vts = [vt_ref[h * DV:(h + 1) * DV, :] for h in range(H)]

        sts = [_dot(ks[h], qts[h]) for h in range(H)]

        for h in range(H):
            u = r * H + h
            sl = slice(h * DV, (h + 1) * DV)
            log_d = jnp.where(mask, g_rows[h] + b_cols[h], -jnp.inf)
            log_inter = g_rows[h] + ms[h]
            m_t = jnp.maximum(log_inter, jnp.max(log_d, axis=0, keepdims=True))
            s = sts[h] * jnp.exp2(log_d - m_t)
            w_inter = jnp.exp2(log_inter - m_t)
            lhs = jnp.concatenate([s.astype(BF16), qts[h] * w_inter.astype(BF16)], axis=0)
            rt_ref[u, :DV, :L] = vts[h]
            rt_ref[u, DV:, :L] = jnp.ones((LANES, L), BF16)
            rt_ref[u, :DV, L:] = ct_ref[u].astype(BF16)
            rt_ref[u, DV:, L:] = jnp.broadcast_to(n_ref[u][0:1, :].astype(BF16), (LANES, DK))
            res = _dot(rt_ref[u], lhs)
            den = res[DV:DV + 1, :]
            hh = res[:DV, :] * (1.0 / jnp.maximum(jnp.abs(den), jnp.exp2(-m_t)))
            if reverse:
                out_ref[r, sl, :] = hh.astype(out_ref.dtype)
            else:
                ot_ref = row_refs[r][5]
                hh = hh + hbt_ref[r, sl, :].astype(F32)
                mu = jnp.mean(hh, axis=0, keepdims=True)
                d = hh - mu
                var = jnp.mean(d * d, axis=0, keepdims=True)
                hn = d * lax.rsqrt(var + LN_EPS) * gain_ref[sl, :]
                out_ref[r, sl, :] = (hn + hn * jnp.tanh(0.5 * ot_ref[sl, :].astype(F32))).astype(out_ref.dtype)

        for h in range(H):
            u = r * H + h
            g_tot = g_rows[h][:, last:last + 1]
            m_new = jnp.maximum(g_tot + ms[h], jnp.max(g_tot + b_rows[h], axis=-1, keepdims=True))
            wk = ks[h].astype(F32) * jnp.exp2(g_tot + b_cols[h] - m_new)
            decay = jnp.exp2(g_tot + ms[h] - m_new)
            ct_ref[u] = decay * ct_ref[u] + _dot(vts[h], wk.astype(BF16))
            n_ref[u] = decay * n_ref[u] + jnp.sum(wk, axis=0, keepdims=True)
            m_ref[u:u + 1, :] = jnp.broadcast_to(m_new, (1, m_ref.shape[1]))


def _mlstm_dir(qt, k, vot, gc, gr, hbt, gain_half, batch, seq_len, reverse):
    L, R, H = MLSTM_CHUNK, MLSTM_ROWS, MLSTM_HEADS
    nc = seq_len // L
    chunk = (lambda c: nc - 1 - c) if reverse else (lambda c: c)
    in_specs, args = [], []
    for r in range(R):
        pos = lambda b, c, r=r: (b * R + r) * nc + chunk(c)
        cols = lambda height, blk, pos=pos: pl.BlockSpec((height, L), lambda b, c: (blk, pos(b, c)))
        in_specs += [cols(MLSTM_QK, 0),
                     pl.BlockSpec((L, MLSTM_QK), lambda b, c, pos=pos: (pos(b, c), 0)),
                     cols(MLSTM_V, 0),
                     pl.BlockSpec((L, N_GATES), lambda b, c, pos=pos: (pos(b, c), 0)),
                     cols(N_GATES, 0)]
        args += [qt, k, vot, gc, gr]
        if not reverse:
            in_specs += [cols(MLSTM_V, 1)]
            args += [vot]
    slab = pl.BlockSpec((R, MLSTM_V, L), lambda b, c: (b, 0, chunk(c)))
    if not reverse:
        in_specs += [slab, _resident((MLSTM_V, L))]
        args += [hbt, gain_half]
    return pl.pallas_call(
        functools.partial(_mlstm_kernel, reverse=reverse),
        out_shape=jax.ShapeDtypeStruct((batch, MLSTM_V, seq_len), BF16),
        grid=(batch // R, nc),
        in_specs=in_specs,
        out_specs=slab,
        scratch_shapes=[pltpu.VMEM((R * H, MLSTM_DV, MLSTM_DK), F32),
                        pltpu.VMEM((R * H, MLSTM_RW, L + MLSTM_DK), BF16),
                        pltpu.VMEM((R * H, SUBLANES, MLSTM_DK), F32),
                        pltpu.VMEM((R * H, LANES), F32)],
        compiler_params=_params("arbitrary", "arbitrary"),
        name="mlstm_bwd" if reverse else "mlstm_fwd",
    )(*args)


def _mlstm(qt, k, vot, gc, gr, gain_half, batch, seq_len):
    hbt = _mlstm_dir(qt, k, vot, gc, gr, None, None, batch, seq_len, True)
    return _mlstm_dir(qt, k, vot, gc, gr, hbt, gain_half, batch, seq_len, False)


def _rglru_kernel(*refs, nc, reverse):
    if reverse:
        x_ref, xp_ref, xn_ref, cw_ref, cb_ref, wa_ref, ba_ref, wx_ref, bx_ref, lam_ref, o_ref, \
            a_ref, u_ref, state_ref, k_ref = refs
    else:
        x_ref, xp_ref, xn_ref, hb_ref, cw_ref, cb_ref, wa_ref, ba_ref, wx_ref, bx_ref, lam_ref, \
            o_ref, a_ref, u_ref, state_ref, k_ref, gy_ref = refs
    tc = x_ref.shape[0]
    c = pl.program_id(1)
    cc = nc - 1 - c if reverse else c

    @pl.when(c == 0)
    def _():
        state_ref[...] = jnp.zeros(state_ref.shape, F32)
        k_ref[...] = (-0.5 * LRU_C * LOG2E) * _softplus(-lam_ref[...])

    xr, yr = _unpack_bf16_pair(x_ref[...])
    prev = jnp.where(cc > 0, _unpack_bf16_pair(xp_ref[...])[0], 0.0)
    nxt = jnp.where(cc < nc - 1, _unpack_bf16_pair(xn_ref[...])[0], 0.0)
    xe = jnp.concatenate([prev, xr, nxt], axis=0)
    xc = cb_ref[...]
    for j in range(CONV_W):
        xc = xc + xe[j:j + tc] * cw_ref[j:j + 1, :]

    x2 = xc.reshape(tc * LRU_GROUP, D_RNN)
    xb = x2.astype(BF16)
    for n in range(LRU_BLOCKS):
        sl = slice(n * LRU_BW, (n + 1) * LRU_BW)
        t_a = jnp.tanh(_dot(xb[:, sl], wa_ref[n]) + ba_ref[:, sl])
        t_i = jnp.tanh(_dot(xb[:, sl], wx_ref[n]) + bx_ref[:, sl])
        a = jnp.exp2(k_ref[:, sl] + k_ref[:, sl] * t_a)
        y = 1.0 - a * a
        root = jnp.where(y > 0.0, y * lax.rsqrt(y), 0.0)
        hx = 0.5 * x2[:, sl]
        u = root * (hx + hx * t_i)
        a_ref[:, :, sl] = a.reshape(tc, LRU_GROUP, LRU_BW)
        u_ref[:, :, sl] = u.reshape(tc, LRU_GROUP, LRU_BW)

    if reverse:
        def step(k, h):
            t = tc - 1 - k
            h = a_ref[t] * h + u_ref[t]
            o_ref[t] = _pack_bf16_pair(h[:, :D_RNN // 2], h[:, D_RNN // 2:])
            return h
    else:
        gy_ref[...] = jax.nn.gelu(yr)

        def step(t, h):
            h = a_ref[t] * h + u_ref[t]
            hb = jnp.concatenate(_unpack_bf16_pair(hb_ref[t]), axis=1)
            o_ref[t] = (h + hb) * gy_ref[t]
            return h

    state_ref[...] = lax.fori_loop(0, tc, step, state_ref[...], unroll=LRU_UNROLL)


def _rglru_dir(xy, hb, cw, cb, wa, ba, wx, bx, lam, batch, seq_len, reverse):
    tc, G, C = LRU_TCHUNK, LRU_GROUP, D_RNN
    nc = seq_len // tc
    chunk = (lambda c: nc - 1 - c) if reverse else (lambda c: c)
    blk = lambda width: pl.BlockSpec((tc, G, width), lambda g, c: (chunk(c), g, 0))
    in_specs = [blk(C),
                pl.BlockSpec((1, G, C), lambda g, c: (jnp.maximum(chunk(c) * tc - 1, 0), g, 0)),
                pl.BlockSpec((2, G, C),
                             lambda g, c: (jnp.minimum((chunk(c) + 1) * (tc // 2), seq_len // 2 - 1), g, 0))]
    args = [xy, xy, xy]
    scratch = [pltpu.VMEM((tc, G, C), F32), pltpu.VMEM((tc, G, C), F32),
               pltpu.VMEM((G, C), F32), pltpu.VMEM((1, C), F32)]
    if not reverse:
        in_specs += [blk(C // 2)]
        args += [hb]
        scratch += [pltpu.VMEM((tc, G, C), F32)]
    in_specs += [_resident((CONV_W, C)), _resident((1, C)),
                 _resident((LRU_BLOCKS, LRU_BW, LRU_BW)), _resident((1, C)),
                 _resident((LRU_BLOCKS, LRU_BW, LRU_BW)), _resident((1, C)), _resident((1, C))]
    args += [cw, cb, wa, ba, wx, bx, lam]
    return pl.pallas_call(
        functools.partial(_rglru_kernel, nc=nc, reverse=reverse),
        out_shape=jax.ShapeDtypeStruct((seq_len, batch, C // 2), jnp.uint32) if reverse
        else jax.ShapeDtypeStruct((seq_len, batch, C), F32),
        grid=(batch // G, nc),
        in_specs=in_specs,
        out_specs=blk(C // 2) if reverse else blk(C),
        scratch_shapes=scratch,
        compiler_params=_params("arbitrary", "arbitrary"),
        name="rglru_bwd" if reverse else "rglru_fwd",
    )(*args)


def _rglru(xy, w, batch, seq_len):
    common = (w["cw_r"], w["cb_r"])
    hb = _rglru_dir(xy, None, *common, w["lru_wa"][1], w["lru_ba"][1:2], w["lru_wx"][1], w["lru_bx"][1:2],
                    w["lru_lam"][1:2], batch, seq_len, True)
    return _rglru_dir(xy, hb, *common, w["lru_wa"][0], w["lru_ba"][0:1], w["lru_wx"][0], w["lru_bx"][0:1],
                      w["lru_lam"][0:1], batch, seq_len, False)


def _kv_kernel(m_ref, g_ref, b_ref, w_ref, o_ref):
    m = _layer_norm(m_ref[...], g_ref[...], b_ref[...])
    o_ref[...] = _dot(m.astype(BF16), w_ref[...]).astype(o_ref.dtype)


def _mem_kv(mem, g, b, w_kv):
    R = mem.shape[0]
    return pl.pallas_call(
        _kv_kernel,
        out_shape=jax.ShapeDtypeStruct((R, 2 * D_MODEL), BF16),
        grid=(R // N_MEM,),
        in_specs=[pl.BlockSpec((N_MEM, D_MODEL), lambda i: (i, 0)), _resident((1, D_MODEL)),
                  _resident((1, D_MODEL)), _resident((D_MODEL, 2 * D_MODEL))],
        out_specs=pl.BlockSpec((N_MEM, 2 * D_MODEL), lambda i: (i, 0)),
        compiler_params=_params("arbitrary"),
        name="mem_kv",
    )(mem, g, b, w_kv)


def _merge_xattn_kernel(x_ref, hmt_ref, mg_ref, hr_ref, kv_ref,
                        wpm_ref, wpr_ref, wout_ref, wq_ref, wo_ref, lng_ref, lnb_ref, y_ref):
    half = x_ref.shape[0] // 2
    rows = [slice(0, half), slice(half, 2 * half)]
    pr = [_dot(hr_ref[r, :].astype(BF16), wpr_ref[...]) for r in rows]
    pm = [_dot_tn(hmt_ref[:, r], wpm_ref[...]) for r in rows]
    merged = []
    for r, a, b in zip(rows, pm, pr):
        g_m = _sigmoid(mg_ref[r, :D_MODEL].astype(F32))
        g_r = _sigmoid(mg_ref[r, D_MODEL:].astype(F32))
        merged.append((g_m * a + g_r * b).astype(BF16))
    mix = [_dot(m, wout_ref[...]) for m in merged]
    xs = [_layer_norm(DN_ALPHA * x_ref[r, :] + m, lng_ref[0:1, :], lnb_ref[0:1, :]) for r, m in zip(rows, mix)]

    qs = [_dot(x.astype(BF16), wq_ref[...]).astype(BF16) for x in xs]
    heads = [[], []]
    for h in range(XATTN_HEADS):
        sl = slice(h * XATTN_HD, (h + 1) * XATTN_HD)
        ss = [_dot_nt(q[:, sl], kv_ref[:, sl]) * (XATTN_HD ** -0.5) for q in qs]
        for i, s in enumerate(ss):
            e = jnp.exp(s - jnp.max(s, axis=-1, keepdims=True))
            p = e * (1.0 / jnp.sum(e, axis=-1, keepdims=True))
            oh = _dot(p.astype(BF16), kv_ref[:, D_MODEL + h * XATTN_HD:D_MODEL + (h + 1) * XATTN_HD])
            heads[i].append(oh.astype(BF16))
    for r, x, ohs in zip(rows, xs, heads):
        xa = _dot(jnp.concatenate(ohs, axis=1), wo_ref[...])
        y_ref[r, :] = _layer_norm(DN_ALPHA * x + xa, lng_ref[1:2, :], lnb_ref[1:2, :])


def _merge_xattn(x, hmt, mg, hr, kv, wpm, wpr, wout, wq, wo, lng, lnb, seq_len):
    T = x.shape[0]
    tm = MERGE_TILE
    tiles_per_seq = seq_len // tm
    row = lambda i: (i, 0)
    return pl.pallas_call(
        _merge_xattn_kernel,
        out_shape=jax.ShapeDtypeStruct((T, D_MODEL), F32),
        grid=(T // tm,),
        in_specs=[pl.BlockSpec((tm, D_MODEL), row),
                  pl.BlockSpec((None, MLSTM_V, tm), lambda i: (i // tiles_per_seq, 0, i % tiles_per_seq)),
                  pl.BlockSpec((tm, 2 * D_MODEL), row),
                  pl.BlockSpec((tm, D_RNN), lambda i: (i % tiles_per_seq, i // tiles_per_seq)),
                  pl.BlockSpec((N_MEM, 2 * D_MODEL), lambda i: (i // tiles_per_seq, 0)),
                  _resident((MLSTM_V, D_MODEL)), _resident((D_RNN, D_MODEL)),
                  _resident((D_MODEL, D_MODEL)), _resident((D_MODEL, D_MODEL)),
                  _resident((D_MODEL, D_MODEL)), _resident((2, D_MODEL)), _resident((2, D_MODEL))],
        out_specs=pl.BlockSpec((tm, D_MODEL), row),
        compiler_params=_params("arbitrary"),
        name="merge_xattn",
    )(x, hmt, mg, hr, kv, wpm, wpr, wout, wq, wo, lng, lnb)


def _prep_layer(p, l):
    w_in, b_in = p["w_in"][l], p["b_in"][l]
    cols = lambda lo, hi: w_in[:, lo:hi]
    bias = lambda lo, hi: b_in[lo:hi]
    return dict(
        ff1_in=p["ff1_in"][l].astype(BF16), ff1_out=p["ff1_out"][l].astype(BF16),
        ff2_in=p["ff2_in"][l].astype(BF16), ff2_out=p["ff2_out"][l].astype(BF16),
        ln_g=p["ln_g"][l], ln_b=p["ln_b"][l],
        w_q=cols(0, MLSTM_QK).astype(BF16), b_q=bias(0, MLSTM_QK)[None],
        cw_q=p["w_conv_qk"][l][:, :MLSTM_QK], cb_q=p["b_conv_qk"][l][None, :MLSTM_QK],
        w_k=cols(MLSTM_QK, OFF_V).astype(BF16), b_k=bias(MLSTM_QK, OFF_V)[None],
        cw_k=p["w_conv_qk"][l][:, MLSTM_QK:], cb_k=p["b_conv_qk"][l][None, MLSTM_QK:],
        w_vot=cols(OFF_V, OFF_GATE).astype(BF16).T, b_vot=bias(OFF_V, OFF_GATE)[:, None],
        w_mg=cols(OFF_MG, D_IN).astype(BF16), b_mg=bias(OFF_MG, D_IN)[None],
        w_xy=cols(OFF_XR, OFF_MG).astype(BF16), b_xy=bias(OFF_XR, OFF_MG)[None],
        w_g=cols(OFF_GATE, OFF_XR).astype(BF16), b_g=bias(OFF_GATE, OFF_XR)[None],
        cw_r=p["w_conv_r"][l], cb_r=p["b_conv_r"][l][None],
        lru_wa=(0.5 * p["lru_wa"][l]).astype(BF16), lru_ba=0.5 * p["lru_ba"][l],
        lru_wx=(0.5 * p["lru_wx"][l]).astype(BF16), lru_bx=0.5 * p["lru_bx"][l], lru_lam=p["lru_lam"][l],
        mh_gain_half=jnp.broadcast_to(0.5 * p["mh_gain"][l][:, None], (MLSTM_V, MLSTM_CHUNK)),
        w_pm=p["w_pm"][l].astype(BF16), w_pr=p["w_pr"][l].astype(BF16), w_out=p["w_out"][l].astype(BF16),
        xa_wq=p["xa_wq"][l].astype(BF16), xa_wkv=p["xa_wkv"][l].astype(BF16), xa_wo=p["xa_wo"][l].astype(BF16),
        mem_ln_g=p["mem_ln_g"][l][None], mem_ln_b=p["mem_ln_b"][l][None],
    )


def _trunk(x, mem, layers):
    B, S, _ = x.shape
    T = B * S
    x = x.reshape(T, D_MODEL)
    mem = mem.reshape(B * N_MEM, D_MODEL)
    for w in layers:
        x, xb = _ffn_ln(x, w["ff1_in"], w["ff1_out"], w["ln_g"][0:1], w["ln_b"][0:1], True)
        qt, k, vot = _qkvo(xb, w, S)
        mg, xy, gc, gr = _mgxy_gates(xb, w, S)
        hmt = _mlstm(qt, k, vot, gc, gr, w["mh_gain_half"], B, S)
        hr = _rglru(xy.reshape(S, B, D_RNN), w, B, S)
        kv = _mem_kv(mem, w["mem_ln_g"], w["mem_ln_b"], w["xa_wkv"])
        x = _merge_xattn(x, hmt, mg, hr.reshape(S, B * D_RNN), kv, w["w_pm"], w["w_pr"],
                         w["w_out"], w["xa_wq"], w["xa_wo"], w["ln_g"][1:3], w["ln_b"][1:3], S)
        (x,) = _ffn_ln(x, w["ff2_in"], w["ff2_out"], w["ln_g"][3:4], w["ln_b"][3:4], False)
    return x.reshape(B, S, D_MODEL)


def kernel(x_prompt, x_sample, mem_prompt, mem_sample, w_in, b_in, w_conv_qk, b_conv_qk, w_conv_r, b_conv_r,
           mh_gain, lru_wa, lru_ba, lru_wx, lru_bx, lru_lam, w_pm, w_pr, w_out, xa_wq, xa_wkv, xa_wo,
           mem_ln_g, mem_ln_b, ff1_in, ff1_out, ff2_in, ff2_out, ln_g, ln_b):
    p = dict(w_in=w_in, b_in=b_in, w_conv_qk=w_conv_qk, b_conv_qk=b_conv_qk, w_conv_r=w_conv_r,
             b_conv_r=b_conv_r, mh_gain=mh_gain, lru_wa=lru_wa, lru_ba=lru_ba, lru_wx=lru_wx,
             lru_bx=lru_bx, lru_lam=lru_lam, w_pm=w_pm, w_pr=w_pr, w_out=w_out, xa_wq=xa_wq,
             xa_wkv=xa_wkv, xa_wo=xa_wo, mem_ln_g=mem_ln_g, mem_ln_b=mem_ln_b, ff1_in=ff1_in,
             ff1_out=ff1_out, ff2_in=ff2_in, ff2_out=ff2_out, ln_g=ln_g, ln_b=ln_b)
    layers = [_prep_layer(p, l) for l in range(DEPTH)]
    return (_trunk(x_prompt, mem_prompt, layers), _trunk(x_sample, mem_sample, layers))
```
